```python
import math
import jax, jax.numpy as jnp
from jax import lax
import numpy as np


D_MODEL = 2048
BATCH = 32
SEQ = 256
DEPTH = 2
DEC_BATCH = 8
DEC_SEQ = 2048
PAST_LEN = 512

GRID_W = 64
D_RNN = D_MODEL
LRU_BLOCKS = 16
LRU_BLOCK = D_RNN // LRU_BLOCKS
CONV_W = 4
CONV_PAD_LEFT = 2
LRU_C = 8.0
DA_HEADS = 8
DA_QK_DIM = 64
DA_V_DIM = 128
ROT_AXIS_DIM = DA_QK_DIM // 2
ROPE_BASE = 10000.0
D_FF = ((8 * D_MODEL // 3 + 127) // 128) * 128
N_MOD = 9
Q_BLOCK = 128
NORM_EPS = 1e-6
QK_COLS = DA_HEADS * 2 * DA_QK_DIM
V_COLS = DA_HEADS * DA_V_DIM
IN_COLS = 2 * D_RNN + 2 * QK_COLS + V_COLS + 2 * D_MODEL
IN_SPLITS = (D_RNN, 2 * D_RNN, 2 * D_RNN + QK_COLS, 2 * D_RNN + 2 * QK_COLS, 2 * D_RNN + 2 * QK_COLS + V_COLS)

kernel_name = 'hybrid_rglru_diffattn_prefix_dit_step'


def rms_norm(x, g):
    xf = x.astype(jnp.float32)
    y = xf * lax.rsqrt(jnp.mean(xf * xf, axis=-1, keepdims=True) + NORM_EPS)
    return y.astype(x.dtype) * g


def swiglu(h, w1, w2):
    g, u = jnp.split(h @ w1, 2, axis=-1)
    return (jax.nn.silu(g) * u) @ w2


def axial_rope_tables(n_tokens):
    n_rows = n_tokens // GRID_W
    t = jnp.arange(n_rows * GRID_W)
    row = (t // GRID_W).astype(jnp.float32)
    col = (t % GRID_W).astype(jnp.float32)
    freqs = 1.0 / (ROPE_BASE ** (jnp.arange(0, ROT_AXIS_DIM, 2, dtype=jnp.float32) / ROT_AXIS_DIM))
    ang = jnp.stack([row[:, None] * freqs, col[:, None] * freqs], axis=1)
    return jnp.cos(ang), jnp.sin(ang)


def apply_axial_rope(x, cos, sin):
    shp = x.shape
    xa = x.reshape(shp[:-1] + (2, 2, ROT_AXIS_DIM // 2))
    c = cos[None, :, None, None].astype(x.dtype)
    s = sin[None, :, None, None].astype(x.dtype)
    x1, x2 = xa[..., 0, :], xa[..., 1, :]
    out = jnp.stack([x1 * c - x2 * s, x2 * c + x1 * s], axis=-2)
    return out.reshape(shp)


def diff_attention(q, k, v, lam):
    b, tq = q.shape[0], q.shape[1]
    nb = tq // Q_BLOCK
    qb = jnp.moveaxis(q.reshape(b, nb, Q_BLOCK, DA_HEADS, 2, DA_QK_DIM), 1, 0)
    scale = DA_QK_DIM ** -0.5
    kf = k.astype(jnp.float32)

    def one_block(qblk):
        s = jnp.einsum('bqhmd,bkhmd->bhmqk', qblk.astype(jnp.float32), kf) * scale
        p = jax.nn.softmax(s, axis=-1)
        p_diff = p[:, :, 0] - lam * p[:, :, 1]
        return jnp.einsum('bhqk,bkhd->bqhd', p_diff.astype(v.dtype), v)

    o = lax.map(one_block, qb)
    return jnp.moveaxis(o, 0, 1).reshape(b, tq, DA_HEADS, DA_V_DIM)


def block_diag(x, w, b):
    xs = x.reshape(x.shape[:-1] + (LRU_BLOCKS, LRU_BLOCK))
    return jnp.einsum('btnd,nde->btne', xs, w).reshape(x.shape) + b


def centred_depthwise_conv(x, w, b):
    t = x.shape[1]
    xp = jnp.pad(x, ((0, 0), (CONV_PAD_LEFT, CONV_W - 1 - CONV_PAD_LEFT), (0, 0)))
    return sum(w[j] * xp[:, j:j + t] for j in range(CONV_W)) + b


def rglru_coeffs(y, wa, ba, wx, bx, lam_param):
    gate_a = jax.nn.sigmoid(block_diag(y, wa, ba)).astype(jnp.float32)
    gate_x = jax.nn.sigmoid(block_diag(y, wx, bx)).astype(jnp.float32)
    log_a = -LRU_C * gate_a * jax.nn.softplus(-lam_param.astype(jnp.float32))
    a = jnp.exp(log_a)
    mult = jnp.sqrt(-jnp.expm1(2.0 * log_a))
    return a, mult * gate_x * y.astype(jnp.float32)


def _combine(left, right):
    a1, b1 = left
    a2, b2 = right
    return a1 * a2, a2 * b1 + b2


def linear_scan(a, bterm, h0, reverse):
    a_cum, b_cum = lax.associative_scan(_combine, (a, bterm), axis=1, reverse=reverse)
    return a_cum * h0.astype(jnp.float32)[:, None, :] + b_cum


def mixer(h, lp, l, ctx_k, ctx_v, ctx_state):
    latent = ctx_k is not None
    bsz, t, _ = h.shape
    x_r, g_r, q, k, v, g_m = jnp.split(h @ lp['w_in'], IN_SPLITS, axis=-1)
    y = centred_depthwise_conv(x_r, lp['conv_w'], lp['conv_b'])
    a_f, b_f = rglru_coeffs(y, lp['lru_wa'][0], lp['lru_ba'][0], lp['lru_wx'][0], lp['lru_bx'][0], lp['lru_lambda'][0])
    a_b, b_b = rglru_coeffs(y, lp['lru_wa'][1], lp['lru_ba'][1], lp['lru_wx'][1], lp['lru_bx'][1], lp['lru_lambda'][1])
    if latent:
        h0_f, h0_b = ctx_state[:, 0], ctx_state[:, 1]
    else:
        h0_f = jnp.zeros((bsz, D_RNN), jnp.float32)
        h0_b = h0_f
    hf = linear_scan(a_f, b_f, h0_f, False)
    hb = linear_scan(a_b, b_b, h0_b, True)
    rec = (hf + hb).astype(h.dtype) * jax.nn.gelu(g_r)
    q = q.reshape(bsz, t, DA_HEADS, 2, DA_QK_DIM)
    k = k.reshape(bsz, t, DA_HEADS, 2, DA_QK_DIM)
    v = v.reshape(bsz, t, DA_HEADS, DA_V_DIM)
    lam_init = 0.8 - 0.6 * math.exp(-0.3 * l)
    lam = (jnp.exp(jnp.sum(lp['lam_q1'].astype(jnp.float32) * lp['lam_k1'].astype(jnp.float32)))
           - jnp.exp(jnp.sum(lp['lam_q2'].astype(jnp.float32) * lp['lam_k2'].astype(jnp.float32))) + lam_init)
    if latent:
        cos, sin = axial_rope_tables(t)
        q_r = apply_axial_rope(q, cos, sin)
        k_r = apply_axial_rope(k, cos, sin)
        keys = jnp.concatenate([ctx_k.reshape(bsz, -1, DA_HEADS, 2, DA_QK_DIM), k_r], axis=1)
        vals = jnp.concatenate([ctx_v, v], axis=1)
        o = diff_attention(q_r, keys, vals, lam)
    else:
        o = diff_attention(q, k, v, lam)
    o = (rms_norm(o, lp['attn_subln']) * (1.0 - lam_init)).reshape(bsz, t, V_COLS)
    gm1, gm2 = jnp.split(jax.nn.sigmoid(g_m), 2, axis=-1)
    out = (gm1 * (rec @ lp['p_lru']) + gm2 * (o @ lp['p_attn'])) @ lp['w_out']
    if latent:
        return out, None
    ctx = (k.reshape(bsz, t, DA_HEADS, 2 * DA_QK_DIM), v,
           jnp.stack([hf[:, -1], hb[:, 0]], axis=1).astype(h.dtype))
    return out, ctx


def trunk_layer(x, mod, lp, l, ctx_k, ctx_v, ctx_state):
    sh1, sc1, gt1, sh2, sc2, gt2, sh3, sc3, gt3 = jnp.split(mod, N_MOD, axis=-1)
    h = rms_norm(x, lp['g_pre'][0]) * (1 + sc1) + sh1
    x = x + 0.5 * gt1 * rms_norm(swiglu(h, lp['ffn_w1'][0], lp['ffn_w2'][0]), lp['g_post'][0])
    h = rms_norm(x, lp['g_pre'][1]) * (1 + sc2) + sh2
    m, ctx = mixer(h, lp, l, ctx_k, ctx_v, ctx_state)
    x = x + gt2 * rms_norm(m, lp['g_post'][1])
    h = rms_norm(x, lp['g_pre'][2]) * (1 + sc3) + sh3
    x = x + 0.5 * gt3 * rms_norm(swiglu(h, lp['ffn_w1'][1], lp['ffn_w2'][1]), lp['g_post'][2])
    return x, ctx


def setup_inputs(seed: int = 0) -> dict:
    key = jax.random.key(seed)
    ks = jax.random.split(key, 32)
    f32 = jnp.float32

    def nrm(k, shape, scale):
        return jax.random.normal(k, shape, f32) * scale

    u = jax.random.uniform(ks[31], (DEPTH, 2, D_RNN), f32, 0.9, 0.999)
    a_base = u ** (1.0 / LRU_C)
    lru_lambda = jnp.log(a_base) - jnp.log1p(-a_base)
    return {
        'x_prompt': nrm(ks[0], (BATCH, SEQ, D_MODEL), 1.0),
        'x_sample': nrm(ks[1], (DEC_BATCH, DEC_SEQ, D_MODEL), 1.0),
        'c': nrm(ks[2], (DEC_BATCH, D_MODEL), 1.0),
        'cache_k': nrm(ks[3], (DEC_BATCH, DEPTH, PAST_LEN, DA_HEADS, 2 * DA_QK_DIM), 1.0),
        'cache_v': nrm(ks[4], (DEC_BATCH, DEPTH, PAST_LEN, DA_HEADS, DA_V_DIM), 1.0),
        'state_lru': nrm(ks[5], (DEC_BATCH, DEPTH, 2, D_RNN), 0.5),
        'c_ctx': nrm(ks[6], (D_MODEL,), 1.0),
        'w_mod': nrm(ks[7], (DEPTH, D_MODEL, N_MOD * D_MODEL), 0.5 * D_MODEL ** -0.5),
        'b_mod': nrm(ks[8], (DEPTH, N_MOD * D_MODEL), 0.02),
        'g_pre': 1.0 + nrm(ks[9], (DEPTH, 3, D_MODEL), 0.05),
        'g_post': 1.0 + nrm(ks[10], (DEPTH, 3, D_MODEL), 0.05),
        'ffn_w1': nrm(ks[11], (DEPTH, 2, D_MODEL, 2 * D_FF), D_MODEL ** -0.5),
        'ffn_w2': nrm(ks[12], (DEPTH, 2, D_FF, D_MODEL), D_FF ** -0.5),
        'w_in': nrm(ks[13], (DEPTH, D_MODEL, IN_COLS), D_MODEL ** -0.5),
        'conv_w': nrm(ks[14], (DEPTH, CONV_W, D_RNN), CONV_W ** -0.5),
        'conv_b': nrm(ks[15], (DEPTH, D_RNN), 0.02),
        'lru_wa': nrm(ks[16], (DEPTH, 2, LRU_BLOCKS, LRU_BLOCK, LRU_BLOCK), LRU_BLOCK ** -0.5),
        'lru_ba': nrm(ks[17], (DEPTH, 2, D_RNN), 0.02),
        'lru_wx': nrm(ks[18], (DEPTH, 2, LRU_BLOCKS, LRU_BLOCK, LRU_BLOCK), LRU_BLOCK ** -0.5),
        'lru_bx': nrm(ks[19], (DEPTH, 2, D_RNN), 0.02),
        'lru_lambda': lru_lambda,
        'lam_q1': nrm(ks[20], (DEPTH, DA_QK_DIM), 0.1),
        'lam_k1': nrm(ks[21], (DEPTH, DA_QK_DIM), 0.1),
        'lam_q2': nrm(ks[22], (DEPTH, DA_QK_DIM), 0.1),
        'lam_k2': nrm(ks[23], (DEPTH, DA_QK_DIM), 0.1),
        'attn_subln': 1.0 + nrm(ks[24], (DEPTH, DA_V_DIM), 0.05),
        'p_lru': nrm(ks[25], (DEPTH, D_RNN, D_MODEL), D_RNN ** -0.5),
        'p_attn': nrm(ks[26], (DEPTH, V_COLS, D_MODEL), V_COLS ** -0.5),
        'w_out': nrm(ks[27], (DEPTH, D_MODEL, D_MODEL), D_MODEL ** -0.5),
    }


def reference(x_prompt, x_sample, c, cache_k, cache_v, state_lru, c_ctx, w_mod, b_mod, g_pre, g_post,
              ffn_w1, ffn_w2, w_in, conv_w, conv_b, lru_wa, lru_ba, lru_wx, lru_bx, lru_lambda,
              lam_q1, lam_k1, lam_q2, lam_k2, attn_subln, p_lru, p_attn, w_out):
    def layer_params(l):
        return {'g_pre': g_pre[l], 'g_post': g_post[l], 'ffn_w1': ffn_w1[l], 'ffn_w2': ffn_w2[l],
                'w_in': w_in[l], 'conv_w': conv_w[l], 'conv_b': conv_b[l], 'lru_wa': lru_wa[l],
                'lru_ba': lru_ba[l], 'lru_wx': lru_wx[l], 'lru_bx': lru_bx[l], 'lru_lambda': lru_lambda[l],
                'lam_q1': lam_q1[l], 'lam_k1': lam_k1[l], 'lam_q2': lam_q2[l], 'lam_k2': lam_k2[l],
                'attn_subln': attn_subln[l], 'p_lru': p_lru[l], 'p_attn': p_attn[l], 'w_out': w_out[l]}

    x = x_prompt
    ks_, vs_, ss_ = [], [], []
    silu_ctx = jax.nn.silu(c_ctx)
    for l in range(DEPTH):
        mod = (silu_ctx @ w_mod[l] + b_mod[l])[None, None, :]
        x, (k_l, v_l, s_l) = trunk_layer(x, mod, layer_params(l), l, None, None, None)
        ks_.append(k_l)
        vs_.append(v_l)
        ss_.append(s_l)
    y_prompt = x
    new_cache_k = jnp.stack(ks_, axis=1)
    new_cache_v = jnp.stack(vs_, axis=1)
    new_state_lru = jnp.stack(ss_, axis=1)

    x = x_sample
    silu_c = jax.nn.silu(c)
    for l in range(DEPTH):
        mod = (silu_c @ w_mod[l] + b_mod[l])[:, None, :]
        x, _ = trunk_layer(x, mod, layer_params(l), l, cache_k[:, l], cache_v[:, l], state_lru[:, l])
    y_sample = x
    return (y_prompt, y_sample, new_cache_k, new_cache_v, new_state_lru)
```

```python
import functools
import math

import jax
import jax.numpy as jnp
from jax import lax
from jax.experimental import pallas as pl
from jax.experimental.pallas import tpu as pltpu

NORM_EPS = 1e-6
LRU_C = 8.0
N_MOD = 9
GRID_W = 64
ROPE_BASE = 10000.0
QK_DIM = 64
CONV_PAD_LEFT = 2

LANES = 128
MOD_ROWS = 16
VMEM_LIMIT = 52 * 1024 * 1024

_BF16 = jnp.bfloat16
_F32 = jnp.float32


def _rms(x, g):
    ms = jnp.mean(x * x, axis=-1, keepdims=True)
    return x * lax.rsqrt(ms + NORM_EPS) * g


def _params(*sem):
    return pltpu.CompilerParams(dimension_semantics=sem, vmem_limit_bytes=VMEM_LIMIT)


def _mod_kernel(c_ref, w_ref, b_ref, o_ref):
    c = c_ref[...]
    s = (c * jax.nn.sigmoid(c)).astype(_BF16)
    o_ref[...] = jnp.dot(s, w_ref[...].astype(_BF16), preferred_element_type=_F32) + b_ref[...]


def _modulation(cond, w_mod, b_mod, tn=1024):
    depth, d, n = w_mod.shape
    return pl.pallas_call(
        _mod_kernel,
        out_shape=jax.ShapeDtypeStruct((depth, MOD_ROWS, n), _F32),
        grid=(depth, n // tn),
        in_specs=[
            pl.BlockSpec((MOD_ROWS, d), lambda l, j: (0, 0)),
            pl.BlockSpec((None, d, tn), lambda l, j: (l, 0, j)),
            pl.BlockSpec((None, 1, tn), lambda l, j: (l, 0, j)),
        ],
        out_specs=pl.BlockSpec((None, MOD_ROWS, tn), lambda l, j: (l, 0, j)),
        compiler_params=_params("parallel", "parallel"),
        name="modulation",
    )(cond, w_mod, b_mod.reshape(depth, 1, n))


def _ffn_kernel(x_ref, mod_ref, gpre_ref, gpost_ref, w1g_ref, w1u_ref, w2_ref, o_ref, h_ref, acc_ref,
                *, mod_base):
    j = pl.program_id(1)

    @pl.when(j == 0)
    def _():
        sh = mod_ref[mod_base:mod_base + 1, :]
        sc = mod_ref[mod_base + 1:mod_base + 2, :]
        h = _rms(x_ref[...], gpre_ref[...]) * (1.0 + sc) + sh
        h_ref[...] = h.astype(h_ref.dtype)
        acc_ref[...] = jnp.zeros_like(acc_ref)

    h = h_ref[...]
    g = jnp.dot(h, w1g_ref[...], preferred_element_type=_F32)
    u = jnp.dot(h, w1u_ref[...], preferred_element_type=_F32)
    act = (g * jax.nn.sigmoid(g) * u).astype(_BF16)
    acc_ref[...] += jnp.dot(act, w2_ref[...], preferred_element_type=_F32)

    @pl.when(j == pl.num_programs(1) - 1)
    def _():
        gt = mod_ref[mod_base + 2:mod_base + 3, :]
        o_ref[...] = x_ref[...] + 0.5 * gt * _rms(acc_ref[...], gpost_ref[...])


def _ffn(x, mod, group_of_tile, gpre, gpost, w1, w2, l, i, mod_base, tm, tf):
    n, d = x.shape
    ffp = w2.shape[2]
    nff = ffp // tf
    return pl.pallas_call(
        functools.partial(_ffn_kernel, mod_base=mod_base),
        out_shape=jax.ShapeDtypeStruct((n, d), _F32),
        grid=(n // tm, nff),
        in_specs=[
            pl.BlockSpec((tm, d), lambda r, j: (r, 0)),
            pl.BlockSpec((None, N_MOD, d), lambda r, j: (group_of_tile(r, tm), 0, 0)),
            pl.BlockSpec((1, d), lambda r, j: (0, 0)),
            pl.BlockSpec((1, d), lambda r, j: (0, 0)),
            pl.BlockSpec((None, None, d, tf), lambda r, j: (l, i, 0, j)),
            pl.BlockSpec((None, None, d, tf), lambda r, j: (l, i, 0, j + nff)),
            pl.BlockSpec((None, None, tf, d), lambda r, j: (l, i, j, 0)),
        ],
        out_specs=pl.BlockSpec((tm, d), lambda r, j: (r, 0)),
        scratch_shapes=[pltpu.VMEM((tm, d), _BF16), pltpu.VMEM((tm, d), _F32)],
        compiler_params=_params("parallel", "arbitrary"),
        name=f"ffn_l{l}_{i}",
    )(x, mod, gpre, gpost, w1, w1, w2)


def _inproj_kernel(x_ref, mod_ref, gpre_ref, w_ref, o_ref, h_ref, *, mod_base):
    @pl.when(pl.program_id(1) == 0)
    def _():
        sh = mod_ref[mod_base:mod_base + 1, :]
        sc = mod_ref[mod_base + 1:mod_base + 2, :]
        h = _rms(x_ref[...], gpre_ref[...]) * (1.0 + sc) + sh
        h_ref[...] = h.astype(h_ref.dtype)

    o_ref[...] = jnp.dot(h_ref[...], w_ref[...], preferred_element_type=_F32)


def _inproj(x, mod, group_of_tile, gpre, w_in, l, mod_base, tm, tn):
    n, d = x.shape
    cols = w_in.shape[2]
    return pl.pallas_call(
        functools.partial(_inproj_kernel, mod_base=mod_base),
        out_shape=jax.ShapeDtypeStruct((n, cols), _F32),
        grid=(n // tm, cols // tn),
        in_specs=[
            pl.BlockSpec((tm, d), lambda r, j: (r, 0)),
            pl.BlockSpec((None, N_MOD, d), lambda r, j: (group_of_tile(r, tm), 0, 0)),
            pl.BlockSpec((1, d), lambda r, j: (0, 0)),
            pl.BlockSpec((None, d, tn), lambda r, j: (l, 0, j)),
        ],
        out_specs=pl.BlockSpec((tm, tn), lambda r, j: (r, j)),
        scratch_shapes=[pltpu.VMEM((tm, d), _BF16)],
        compiler_params=_params("parallel", "arbitrary"),
        name=f"inproj_l{l}",
    )(x, mod, gpre, w_in)


def _lru_kernel(xr_ref, gr_ref, cw_ref, cb_ref, wg_ref, bg_ref, lam_ref, h0_ref, rec_ref, st_ref,
                y_ref, af_ref, bf_ref, ab_ref, bb_ref, *, seq, cb, row_chunk):
    x = xr_ref[...]
    row = lax.broadcasted_iota(jnp.int32, (seq, cb), 0)
    y = cb_ref[...] + cw_ref[CONV_PAD_LEFT:CONV_PAD_LEFT + 1, :] * x
    for tap in range(cw_ref.shape[0]):
        off = tap - CONV_PAD_LEFT
        if off == 0:
            continue
        shifted = pltpu.roll(x, (-off) % seq, 0)
        valid = (row >= -off) if off < 0 else (row < seq - off)
        y = y + cw_ref[tap:tap + 1, :] * jnp.where(valid, shifted, 0.0)
    y_ref[...] = y

    sp = jax.nn.softplus(-lam_ref[...])

    def coeffs(c, carry):
        r0 = pl.multiple_of(c * row_chunk, row_chunk)
        for nb in range(cb // LANES):
            cs = slice(nb * LANES, (nb + 1) * LANES)
            yb = y_ref[pl.ds(r0, row_chunk), cs]
            g4 = jnp.dot(yb.astype(_BF16), wg_ref[nb], preferred_element_type=_F32)
            for direction, (a_ref, b_ref) in enumerate(((af_ref, bf_ref), (ab_ref, bb_ref))):
                ga = jax.nn.sigmoid(g4[:, (2 * direction) * LANES:(2 * direction + 1) * LANES]
                                    + bg_ref[2 * direction:2 * direction + 1, cs])
                gx = jax.nn.sigmoid(g4[:, (2 * direction + 1) * LANES:(2 * direction + 2) * LANES]
                                    + bg_ref[2 * direction + 1:2 * direction + 2, cs])
                log_a = -LRU_C * ga * sp[direction:direction + 1, cs]
                a = jnp.exp(log_a)
                a_ref[pl.ds(r0, row_chunk), cs] = a
                one_minus_a2 = jnp.tanh(-log_a) * (1.0 + a * a)
                b_ref[pl.ds(r0, row_chunk), cs] = jnp.sqrt(one_minus_a2) * gx * yb
        return carry

    lax.fori_loop(0, seq // row_chunk, coeffs, 0)

    unroll = 8

    def scan(c, carry):
        hf, hb = carry
        for r in range(unroll):
            t = c * unroll + r
            hf = af_ref[pl.ds(t, 1), :] * hf + bf_ref[pl.ds(t, 1), :]
            bf_ref[pl.ds(t, 1), :] = hf
            tb = seq - 1 - t
            hb = ab_ref[pl.ds(tb, 1), :] * hb + bb_ref[pl.ds(tb, 1), :]
            bb_ref[pl.ds(tb, 1), :] = hb
        return hf, hb

    hf, hb = lax.fori_loop(0, seq // unroll, scan, (h0_ref[0:1, :], h0_ref[1:2, :]))
    st_ref[0:1, :] = hf
    st_ref[1:2, :] = hb
    rec_ref[...] = ((bf_ref[...] + bb_ref[...]) * jax.nn.gelu(gr_ref[...])).astype(rec_ref.dtype)


def _lru(proj, conv_w, conv_b, wg, bg, lam, h0, l, first_row, seq, cb, d_rnn):
    n_seq = h0.shape[0]
    rb0 = first_row // seq
    ncb = d_rnn // cb
    row_chunk = min(seq, 256)
    return pl.pallas_call(
        functools.partial(_lru_kernel, seq=seq, cb=cb, row_chunk=row_chunk),
        out_shape=(jax.ShapeDtypeStruct((n_seq * seq, d_rnn), _BF16),
                   jax.ShapeDtypeStruct((n_seq, 2, d_rnn), _F32)),
        grid=(n_seq, ncb),
        in_specs=[
            pl.BlockSpec((seq, cb), lambda s, c: (rb0 + s, c)),
            pl.BlockSpec((seq, cb), lambda s, c: (rb0 + s, ncb + c)),
            pl.BlockSpec((None, conv_w.shape[1], cb), lambda s, c: (l, 0, c)),
            pl.BlockSpec((None, 1, cb), lambda s, c: (l, 0, c)),
            pl.BlockSpec((None, cb // LANES, LANES, 4 * LANES), lambda s, c: (l, c, 0, 0)),
            pl.BlockSpec((None, 4, cb), lambda s, c: (l, 0, c)),
            pl.BlockSpec((None, 2, cb), lambda s, c: (l, 0, c)),
            pl.BlockSpec((None, 2, cb), lambda s, c: (s, 0, c)),
        ],
        out_specs=(pl.BlockSpec((seq, cb), lambda s, c: (s, c)),
                   pl.BlockSpec((None, 2, cb), lambda s, c: (s, 0, c))),
        scratch_shapes=[pltpu.VMEM((seq, cb), _F32)] * 5,
        compiler_params=_params("parallel", "parallel"),
        name=f"rglru_l{l}_t{seq}",
    )(proj, proj, conv_w, conv_b, wg, bg, lam, h0)


def _rope(x, cos, sin_signed):
    lane = lax.broadcasted_iota(jnp.int32, x.shape, 1)
    half = QK_DIM // 4
    partner = jnp.where((lane & half) == 0, pltpu.roll(x, LANES - half, 1), pltpu.roll(x, half, 1))
    return x * cos + partner * sin_signed


def _attn_kernel(*refs, past, seq, tq, rope, lam_init):
    if rope:
        (q_ref, k_ref, v_ref, ck_ref, cv_ref, cq_ref, sq_ref, ckk_ref, skk_ref,
         lam_ref, g_ref, o_ref, kbuf, vbuf) = refs
    else:
        q_ref, k_ref, v_ref, lam_ref, g_ref, o_ref, kbuf, vbuf = refs

    @pl.when(pl.program_id(2) == 0)
    def _():
        k = k_ref[...]
        if rope:
            k = _rope(k, ckk_ref[...], skk_ref[...])
            kbuf[0:past, :] = ck_ref[...].astype(kbuf.dtype)
            vbuf[0:past, :] = cv_ref[...].astype(vbuf.dtype)
        kbuf[past:past + seq, :] = k.astype(kbuf.dtype)
        vbuf[past:past + seq, :] = v_ref[...].astype(vbuf.dtype)

    lv = lam_ref[...]
    lam = (jnp.exp(jnp.sum(lv[0:1] * lv[1:2], axis=-1, keepdims=True))
           - jnp.exp(jnp.sum(lv[2:3] * lv[3:4], axis=-1, keepdims=True)) + lam_init)

    q = q_ref[...]
    if rope:
        q = _rope(q, cq_ref[...], sq_ref[...])
    q = q * (QK_DIM ** -0.5)
    lane = lax.broadcasted_iota(jnp.int32, q.shape, 1)
    qm = jnp.concatenate([jnp.where(lane < QK_DIM, q, 0.0), jnp.where(lane >= QK_DIM, q, 0.0)], axis=0)
    s = lax.dot_general(qm.astype(_BF16), kbuf[...], (((1,), (1,)), ((), ())),
                        preferred_element_type=_F32)
    e = jnp.exp(s - jnp.max(s, axis=-1, keepdims=True))
    p = e / jnp.sum(e, axis=-1, keepdims=True)
    p_diff = p[0:tq] - lam * p[tq:2 * tq]
    o = jnp.dot(p_diff.astype(_BF16), vbuf[...], preferred_element_type=_F32)
    o_ref[...] = (_rms(o, g_ref[...]) * (1.0 - lam_init)).astype(o_ref.dtype)


def _attention(proj, lamv, gsub, l, first_row, n_seq, seq, tq, heads, col_q, lam_init, ctx=None):
    rope = ctx is not None
    rq0, rk0 = first_row // tq, first_row // seq
    nq = seq // tq
    hq, hk, hv = col_q // LANES, col_q // LANES + heads, col_q // LANES + 2 * heads
    in_specs = [
        pl.BlockSpec((tq, LANES), lambda b, h, t: (rq0 + b * nq + t, hq + h)),
        pl.BlockSpec((seq, LANES), lambda b, h, t: (rk0 + b, hk + h)),
        pl.BlockSpec((seq, LANES), lambda b, h, t: (rk0 + b, hv + h)),
    ]
    args = [proj, proj, proj]
    past = 0
    if rope:
        cache_k, cache_v, cos, sin = ctx
        past = cache_k.shape[2]
        in_specs += [
            pl.BlockSpec((None, None, past, LANES), lambda b, h, t: (b, l, 0, h)),
            pl.BlockSpec((None, None, past, LANES), lambda b, h, t: (b, l, 0, h)),
            pl.BlockSpec((tq, LANES), lambda b, h, t: (t, 0)),
            pl.BlockSpec((tq, LANES), lambda b, h, t: (t, 0)),
            pl.BlockSpec((seq, LANES), lambda b, h, t: (0, 0)),
            pl.BlockSpec((seq, LANES), lambda b, h, t: (0, 0)),
        ]
        args += [cache_k, cache_v, cos, sin, cos, sin]
    in_specs += [
        pl.BlockSpec((None, 4, QK_DIM), lambda b, h, t: (l, 0, 0)),
        pl.BlockSpec((None, 1, LANES), lambda b, h, t: (l, 0, 0)),
    ]
    args += [lamv, gsub]
    return pl.pallas_call(
        functools.partial(_attn_kernel, past=past, seq=seq, tq=tq, rope=rope, lam_init=lam_init),
        out_shape=jax.ShapeDtypeStruct((n_seq * seq, heads * LANES), _BF16),
        grid=(n_seq, heads, nq),
        in_specs=in_specs,
        out_specs=pl.BlockSpec((tq, LANES), lambda b, h, t: (b * nq + t, h)),
        scratch_shapes=[pltpu.VMEM((past + seq, LANES), _BF16)] * 2,
        compiler_params=_params("parallel", "parallel", "arbitrary"),
        name=f"diffattn_l{l}_t{seq}",
    )(*args)


def _merge_kernel(x_ref, mod_ref, gpost_ref, rec_ref, att_ref, gm1_ref, gm2_ref, plru_ref, pattn_ref,
                  wout_ref, o_ref, acc_ref, *, mod_base):
    j = pl.program_id(1)

    @pl.when(j == 0)
    def _():
        acc_ref[...] = jnp.zeros_like(acc_ref)

    a = jnp.dot(rec_ref[...], plru_ref[...], preferred_element_type=_F32)
    b = jnp.dot(att_ref[...], pattn_ref[...], preferred_element_type=_F32)
    merged = jax.nn.sigmoid(gm1_ref[...]) * a + jax.nn.sigmoid(gm2_ref[...]) * b
    acc_ref[...] += jnp.dot(merged.astype(_BF16), wout_ref[...], preferred_element_type=_F32)

    @pl.when(j == pl.num_programs(1) - 1)
    def _():
        gt = mod_ref[mod_base + 2:mod_base + 3, :]
        o_ref[...] = x_ref[...] + gt * _rms(acc_ref[...], gpost_ref[...])


def _merge(x, mod, group_of_tile, gpost, rec, att, proj, col_gm, p_lru, p_attn, w_out, l, mod_base, tm, tn):
    n, d = x.shape
    d_rnn, v_cols = rec.shape[1], att.shape[1]
    g1, g2 = col_gm // tn, (col_gm + d) // tn
    return pl.pallas_call(
        functools.partial(_merge_kernel, mod_base=mod_base),
        out_shape=jax.ShapeDtypeStruct((n, d), _F32),
        grid=(n // tm, d // tn),
        in_specs=[
            pl.BlockSpec((tm, d), lambda r, j: (r, 0)),
            pl.BlockSpec((None, N_MOD, d), lambda r, j: (group_of_tile(r, tm), 0, 0)),
            pl.BlockSpec((1, d), lambda r, j: (0, 0)),
            pl.BlockSpec((tm, d_rnn), lambda r, j: (r, 0)),
            pl.BlockSpec((tm, v_cols), lambda r, j: (r, 0)),
            pl.BlockSpec((tm, tn), lambda r, j: (r, g1 + j)),
            pl.BlockSpec((tm, tn), lambda r, j: (r, g2 + j)),
            pl.BlockSpec((None, d_rnn, tn), lambda r, j: (l, 0, j)),
            pl.BlockSpec((None, v_cols, tn), lambda r, j: (l, 0, j)),
            pl.BlockSpec((None, tn, d), lambda r, j: (l, j, 0)),
        ],
        out_specs=pl.BlockSpec((tm, d), lambda r, j: (r, 0)),
        scratch_shapes=[pltpu.VMEM((tm, d), _F32)],
        compiler_params=_params("parallel", "arbitrary"),
        name=f"merge_l{l}",
    )(x, mod, gpost, rec, att, proj, proj, p_lru, p_attn, w_out)


def _rope_tables(n_tokens):
    t = jnp.arange(n_tokens)
    row = (t // GRID_W).astype(_F32)
    col = (t % GRID_W).astype(_F32)
    n_freq = QK_DIM // 4
    freqs = 1.0 / (ROPE_BASE ** (jnp.arange(0, 2 * n_freq, 2, dtype=_F32) / (2 * n_freq)))
    ar, ac = row[:, None] * freqs, col[:, None] * freqs
    cos = jnp.concatenate([jnp.cos(ar), jnp.cos(ar), jnp.cos(ac), jnp.cos(ac)], axis=-1)
    sin = jnp.concatenate([-jnp.sin(ar), jnp.sin(ar), -jnp.sin(ac), jnp.sin(ac)], axis=-1)
    reps = LANES // QK_DIM
    return jnp.tile(cos, (1, reps)), jnp.tile(sin, (1, reps))


def kernel(x_prompt, x_sample, c, cache_k, cache_v, state_lru, c_ctx, w_mod, b_mod, g_pre, g_post, ffn_w1, ffn_w2, w_in, conv_w, conv_b, lru_wa, lru_ba, lru_wx, lru_bx, lru_lambda, lam_q1, lam_k1, lam_q2, lam_k2, attn_subln, p_lru, p_attn, w_out):
    batch, seq, d = x_prompt.shape
    dec_batch, dec_seq, _ = x_sample.shape
    depth = w_mod.shape[0]
    d_rnn = conv_w.shape[2]
    heads = cache_k.shape[3]
    past = cache_k.shape[2]
    v_cols = heads * cache_v.shape[4]
    qk_cols = heads * cache_k.shape[4]
    d_ff = ffn_w2.shape[2]
    n_prompt, n_sample = batch * seq, dec_batch * dec_seq
    assert cache_k.shape[4] == LANES and cache_v.shape[4] == LANES and 1 + dec_batch <= MOD_ROWS
    col_q = 2 * d_rnn
    col_gm = col_q + 2 * qk_cols + v_cols

    tm, tf, tn_in, tn_merge, tq = 512, 512, 1024, 512, 256
    assert n_prompt % tm == 0 and dec_seq % tm == 0 and seq % tq == 0 and dec_seq % tq == 0

    def group_of_tile(r, rows):
        first = n_prompt // rows
        return jnp.where(r < first, 0, 1 + (r - first) // (dec_seq // rows))

    ffp = -(-d_ff // tf) * tf
    padc = ((0, 0), (0, 0), (0, 0), (0, ffp - d_ff))
    w1 = jnp.concatenate([jnp.pad(ffn_w1[..., :d_ff], padc), jnp.pad(ffn_w1[..., d_ff:], padc)],
                         axis=-1).astype(_BF16)
    w2 = jnp.pad(ffn_w2, ((0, 0), (0, 0), (0, ffp - d_ff), (0, 0))).astype(_BF16)
    w_in_b, p_lru_b, p_attn_b, w_out_b = (w.astype(_BF16) for w in (w_in, p_lru, p_attn, w_out))
    wg = jnp.concatenate([lru_wa[:, 0], lru_wx[:, 0], lru_wa[:, 1], lru_wx[:, 1]], axis=-1).astype(_BF16)
    bg = jnp.stack([lru_ba[:, 0], lru_bx[:, 0], lru_ba[:, 1], lru_bx[:, 1]], axis=1)
    lamv = jnp.stack([lam_q1, lam_k1, lam_q2, lam_k2], axis=1)
    gsub = attn_subln.reshape(depth, 1, -1)
    conv_b3 = conv_b.reshape(depth, 1, d_rnn)
    cache_k2 = cache_k.reshape(dec_batch, depth, past, qk_cols)
    cache_v2 = cache_v.reshape(dec_batch, depth, past, v_cols)
    cos, sin = _rope_tables(dec_seq)
    zero_state = jnp.zeros((batch, 2, d_rnn), _F32)

    cond = jnp.concatenate([c_ctx[None], c, jnp.zeros((MOD_ROWS - 1 - dec_batch, d), _F32)], axis=0)
    mod_all = _modulation(cond, w_mod, b_mod).reshape(depth, MOD_ROWS, N_MOD, d)

    x = jnp.concatenate([x_prompt.reshape(n_prompt, d), x_sample.reshape(n_sample, d)], axis=0)
    new_k, new_v, new_s = [], [], []
    for l in range(depth):
        mod = mod_all[l]
        gpre = [g_pre[l, i].reshape(1, d) for i in range(3)]
        gpost = [g_post[l, i].reshape(1, d) for i in range(3)]
        lam_init = 0.8 - 0.6 * math.exp(-0.3 * l)

        x = _ffn(x, mod, group_of_tile, gpre[0], gpost[0], w1, w2, l, 0, 0, tm, tf)
        proj = _inproj(x, mod, group_of_tile, gpre[1], w_in_b, l, 3, tm, tn_in)
        rec_p, st_p = _lru(proj, conv_w, conv_b3, wg, bg, lru_lambda, zero_state, l, 0, seq, d_rnn, d_rnn)
        rec_s, _ = _lru(proj, conv_w, conv_b3, wg, bg, lru_lambda, state_lru[:, l], l, n_prompt, dec_seq,
                        2 * LANES, d_rnn)
        att_p = _attention(proj, lamv, gsub, l, 0, batch, seq, tq, heads, col_q, lam_init)
        att_s = _attention(proj, lamv, gsub, l, n_prompt, dec_batch, dec_seq, tq, heads, col_q, lam_init,
                           ctx=(cache_k2, cache_v2, cos, sin))
        rec = jnp.concatenate([rec_p, rec_s], axis=0)
        att = jnp.concatenate([att_p, att_s], axis=0)
        x = _merge(x, mod, group_of_tile, gpost[1], rec, att, proj, col_gm, p_lru_b, p_attn_b, w_out_b,
                   l, 3, tm, tn_merge)
        x = _ffn(x, mod, group_of_tile, gpre[2], gpost[2], w1, w2, l, 1, 6, tm, tf)

        new_k.append(proj[:n_prompt, col_q + qk_cols:col_q + 2 * qk_cols].reshape(batch, seq, heads, -1))
        new_v.append(proj[:n_prompt, col_q + 2 * qk_cols:col_gm].reshape(batch, seq, heads, -1))
        new_s.append(st_p)

    y_prompt = x[:n_prompt].reshape(batch, seq, d)
    y_sample = x[n_prompt:].reshape(dec_batch, dec_seq, d)
    return (y_prompt, y_sample, jnp.stack(new_k, axis=1), jnp.stack(new_v, axis=1), jnp.stack(new_s, axis=1))
```

```python
import functools
import math

import jax
import jax.numpy as jnp
from jax import lax
from jax.experimental import pallas as pl
from jax.experimental.pallas import tpu as pltpu

NORM_EPS = 1e-6
LRU_C = 8.0
N_MOD = 9
GRID_W = 64
ROPE_BASE = 10000.0
QK_DIM = 64
CONV_PAD_LEFT = 2

LANES = 128
SUBLANES = 8
MOD_ROWS = 16
VMEM_LIMIT = 52 * 1024 * 1024

_BF16 = jnp.bfloat16
_F32 = jnp.float32


def _rms(x, g):
    ms = jnp.mean(x * x, axis=-1, keepdims=True)
    return x * lax.rsqrt(ms + NORM_EPS) * g


def _params(*sem, vmem=VMEM_LIMIT):
    return pltpu.CompilerParams(dimension_semantics=sem, vmem_limit_bytes=vmem)


def _mod_kernel(c_ref, w_ref, b_ref, o_ref):
    c = c_ref[...]
    s = (c * jax.nn.sigmoid(c)).astype(_BF16)
    o_ref[...] = jnp.dot(s, w_ref[...].astype(_BF16), preferred_element_type=_F32) + b_ref[...]


def _modulation(cond, w_mod, b_mod, tn=1024):
    depth, d, n = w_mod.shape
    return pl.pallas_call(
        _mod_kernel,
        out_shape=jax.ShapeDtypeStruct((depth, MOD_ROWS, n), _F32),
        grid=(depth, n // tn),
        in_specs=[
            pl.BlockSpec((MOD_ROWS, d), lambda l, j: (0, 0)),
            pl.BlockSpec((None, d, tn), lambda l, j: (l, 0, j)),
            pl.BlockSpec((None, 1, tn), lambda l, j: (l, 0, j)),
        ],
        out_specs=pl.BlockSpec((None, MOD_ROWS, tn), lambda l, j: (l, 0, j)),
        compiler_params=_params("parallel", "parallel"),
        name="modulation",
    )(cond, w_mod, b_mod.reshape(depth, 1, n))


def _ffn_kernel(x_ref, mod_ref, gpre_ref, gpost_ref, w1g_ref, w1u_ref, w2_ref, o_ref, h_ref, acc_ref,
                *, mod_base):
    j = pl.program_id(1)

    @pl.when(j == 0)
    def _():
        sh = mod_ref[mod_base:mod_base + 1, :]
        sc = mod_ref[mod_base + 1:mod_base + 2, :]
        h = _rms(x_ref[...], gpre_ref[...]) * (1.0 + sc) + sh
        h_ref[...] = h.astype(h_ref.dtype)
        acc_ref[...] = jnp.zeros_like(acc_ref)

    h = h_ref[...]
    g = jnp.dot(h, w1g_ref[...], preferred_element_type=_F32)
    u = jnp.dot(h, w1u_ref[...], preferred_element_type=_F32)
    act = (g * jax.nn.sigmoid(g) * u).astype(_BF16)
    acc_ref[...] += jnp.dot(act, w2_ref[...], preferred_element_type=_F32)

    @pl.when(j == pl.num_programs(1) - 1)
    def _():
        gt = mod_ref[mod_base + 2:mod_base + 3, :]
        o_ref[...] = x_ref[...] + 0.5 * gt * _rms(acc_ref[...], gpost_ref[...])


def _ffn(x, mod, group_of_tile, gpre, gpost, w1, w2, l, i, mod_base, tm, tf):
    n, d = x.shape
    ffp = w2.shape[2]
    nff = ffp // tf
    return pl.pallas_call(
        functools.partial(_ffn_kernel, mod_base=mod_base),
        out_shape=jax.ShapeDtypeStruct((n, d), _F32),
        grid=(n // tm, nff),
        in_specs=[
            pl.BlockSpec((tm, d), lambda r, j: (r, 0)),
            pl.BlockSpec((None, N_MOD, d), lambda r, j: (group_of_tile(r, tm), 0, 0)),
            pl.BlockSpec((1, d), lambda r, j: (0, 0)),
            pl.BlockSpec((1, d), lambda r, j: (0, 0)),
            pl.BlockSpec((None, None, d, tf), lambda r, j: (l, i, 0, j)),
            pl.BlockSpec((None, None, d, tf), lambda r, j: (l, i, 0, j + nff)),
            pl.BlockSpec((None, None, tf, d), lambda r, j: (l, i, j, 0)),
        ],
        out_specs=pl.BlockSpec((tm, d), lambda r, j: (r, 0)),
        scratch_shapes=[pltpu.VMEM((tm, d), _BF16), pltpu.VMEM((tm, d), _F32)],
        compiler_params=_params("parallel", "arbitrary"),
        name=f"ffn_l{l}_{i}",
    )(x, mod, gpre, gpost, w1, w1, w2)


def _inproj_kernel(x_ref, mod_ref, gpre_ref, w_ref, o_ref, h_ref, *, mod_base):
    @pl.when(pl.program_id(1) == 0)
    def _():
        sh = mod_ref[mod_base:mod_base + 1, :]
        sc = mod_ref[mod_base + 1:mod_base + 2, :]
        h = _rms(x_ref[...], gpre_ref[...]) * (1.0 + sc) + sh
        h_ref[...] = h.astype(h_ref.dtype)

    o_ref[...] = jnp.dot(h_ref[...], w_ref[...], preferred_element_type=_F32)


def _inproj(x, mod, group_of_tile, gpre, w_in, l, mod_base, tm, tn):
    n, d = x.shape
    cols = w_in.shape[2]
    return pl.pallas_call(
        functools.partial(_inproj_kernel, mod_base=mod_base),
        out_shape=jax.ShapeDtypeStruct((n, cols), _F32),
        grid=(n // tm, cols // tn),
        in_specs=[
            pl.BlockSpec((tm, d), lambda r, j: (r, 0)),
            pl.BlockSpec((None, N_MOD, d), lambda r, j: (group_of_tile(r, tm), 0, 0)),
            pl.BlockSpec((1, d), lambda r, j: (0, 0)),
            pl.BlockSpec((None, d, tn), lambda r, j: (l, 0, j)),
        ],
        out_specs=pl.BlockSpec((tm, tn), lambda r, j: (r, j)),
        scratch_shapes=[pltpu.VMEM((tm, d), _BF16)],
        compiler_params=_params("parallel", "arbitrary"),
        name=f"inproj_l{l}",
    )(x, mod, gpre, w_in)


LRU_TILES = SUBLANES // 2
CONV_HALO = SUBLANES


def _sigmoid(x):
    return 0.5 * jnp.tanh(0.5 * x) + 0.5


def _lru_kernel(*refs, seq, row_chunk, unroll, aliased):
    if aliased:
        refs = refs[:8] + refs[9:]
    (xr_ref, gr_ref, cw_ref, cb_ref, wg_ref, bg_ref, lam_ref, h0_ref, rec_ref, st_ref,
     xp_ref, a_ref, b_ref, h_ref) = refs
    nt = LRU_TILES
    cg = nt * LANES
    zeros = jnp.zeros((CONV_HALO, cg), _F32)
    xp_ref[0:CONV_HALO, :] = zeros
    xp_ref[CONV_HALO + seq:2 * CONV_HALO + seq, :] = zeros
    xp_ref[CONV_HALO:CONV_HALO + seq, :] = xr_ref[...]

    neg_log_a_scale = (0.5 * LRU_C) * jax.nn.softplus(-lam_ref[...])
    exp2_scale = -math.log2(math.e) * neg_log_a_scale

    def coeffs(c, carry):
        r0 = c * row_chunk
        for n in range(nt):
            cs = slice(n * LANES, (n + 1) * LANES)
            y = cb_ref[:, cs]
            for tap in range(cw_ref.shape[0]):
                y = y + cw_ref[tap:tap + 1, cs] * xp_ref[pl.ds(r0 + CONV_HALO + tap - CONV_PAD_LEFT, row_chunk), cs]
            g4 = jnp.dot(y.astype(_BF16), wg_ref[n], preferred_element_type=_F32)
            half_y = 0.5 * y
            for direction in range(2):
                ua = 1.0 + jnp.tanh(g4[:, (2 * direction) * LANES:(2 * direction + 1) * LANES]
                                    + bg_ref[2 * direction:2 * direction + 1, cs])
                ux = 1.0 + jnp.tanh(g4[:, (2 * direction + 1) * LANES:(2 * direction + 2) * LANES]
                                    + bg_ref[2 * direction + 1:2 * direction + 2, cs])
                a = jnp.exp2(ua * exp2_scale[direction:direction + 1, cs])
                one_minus_a2 = jnp.tanh(ua * neg_log_a_scale[direction:direction + 1, cs]) * (1.0 + a * a)
                mult = jnp.where(one_minus_a2 > 0.0, one_minus_a2 * lax.rsqrt(one_minus_a2), 0.0)
                dense_rows = pl.ds(r0 * SUBLANES + nt * direction + n, row_chunk, stride=SUBLANES)
                a_ref[dense_rows, :] = a
                b_ref[dense_rows, :] = mult * ux * half_y
        return carry

    for c in range(seq // row_chunk):
        coeffs(c, 0)

    is_fwd = lax.broadcasted_iota(jnp.int32, (SUBLANES, LANES), 0) < nt

    def scan(c, h):
        for r in range(unroll):
            j = c * unroll + r
            jf = pl.multiple_of(j * SUBLANES, SUBLANES)
            jb = pl.multiple_of((seq - 1 - j) * SUBLANES, SUBLANES)
            a = jnp.where(is_fwd, a_ref[pl.ds(jf, SUBLANES), :], a_ref[pl.ds(jb, SUBLANES), :])
            b = jnp.where(is_fwd, b_ref[pl.ds(jf, SUBLANES), :], b_ref[pl.ds(jb, SUBLANES), :])
            h = a * h + b
            h_ref[pl.ds(jf, nt), :] = h[0:nt]
            h_ref[pl.ds(jb + nt, nt), :] = h[nt:2 * nt]
        return h

    h0 = jnp.concatenate([h0_ref[d:d + 1, n * LANES:(n + 1) * LANES] for d in range(2) for n in range(nt)], axis=0)
    h = lax.fori_loop(0, seq // unroll, scan, h0)
    for n in range(nt):
        st_ref[0:1, n * LANES:(n + 1) * LANES] = h[n:n + 1]
        st_ref[1:2, n * LANES:(n + 1) * LANES] = h[nt + n:nt + n + 1]

    def gate(c, carry):
        r0 = pl.multiple_of(c * row_chunk, row_chunk)
        for n in range(nt):
            cs = slice(n * LANES, (n + 1) * LANES)
            hf = h_ref[pl.ds(r0 * SUBLANES + n, row_chunk, stride=SUBLANES), :]
            hb = h_ref[pl.ds(r0 * SUBLANES + nt + n, row_chunk, stride=SUBLANES), :]
            rec_ref[pl.ds(r0, row_chunk), cs] = (
                (hf + hb) * jax.nn.gelu(gr_ref[pl.ds(r0, row_chunk), cs])).astype(rec_ref.dtype)
        return carry

    lax.fori_loop(0, seq // row_chunk, gate, 0)


def _lru(proj, rec, conv_w, conv_b, wg, bg, lam, h0, l, first_row, seq, d_rnn):
    n_seq = h0.shape[0]
    rb0 = first_row // seq
    cg = LRU_TILES * LANES
    ncg = d_rnn // cg
    row_chunk = min(seq, 256)
    aliased = rec is not None
    in_specs = [
        pl.BlockSpec((seq, cg), lambda s, c: (rb0 + s, c)),
        pl.BlockSpec((seq, cg), lambda s, c: (rb0 + s, ncg + c)),
        pl.BlockSpec((None, conv_w.shape[1], cg), lambda s, c: (l, 0, c)),
        pl.BlockSpec((None, 1, cg), lambda s, c: (l, 0, c)),
        pl.BlockSpec((None, LRU_TILES, LANES, 4 * LANES), lambda s, c: (l, c, 0, 0)),
        pl.BlockSpec((None, 4, cg), lambda s, c: (l, 0, c)),
        pl.BlockSpec((None, 2, cg), lambda s, c: (l, 0, c)),
        pl.BlockSpec((None, 2, cg), lambda s, c: (s, 0, c)),
    ]
    args = [proj, proj, conv_w, conv_b, wg, bg, lam, h0]
    if aliased:
        in_specs.append(pl.BlockSpec(memory_space=pl.ANY))
        args.append(rec)
    return pl.pallas_call(
        functools.partial(_lru_kernel, seq=seq, row_chunk=row_chunk, unroll=8, aliased=aliased),
        out_shape=(jax.ShapeDtypeStruct((proj.shape[0], d_rnn), _BF16),
                   jax.ShapeDtypeStruct((n_seq, 2, d_rnn), _F32)),
        grid=(n_seq, ncg),
        in_specs=in_specs,
        out_specs=(pl.BlockSpec((seq, cg), lambda s, c: (rb0 + s, c)),
                   pl.BlockSpec((None, 2, cg), lambda s, c: (s, 0, c))),
        scratch_shapes=[pltpu.VMEM((seq + 2 * CONV_HALO, cg), _F32)] + [pltpu.VMEM((seq * SUBLANES, LANES), _F32)] * 3,
        input_output_aliases={8: 0} if aliased else {},
        compiler_params=_params("parallel", "parallel", vmem=58 * 1024 * 1024),
        name=f"rglru_l{l}_t{seq}",
    )(*args)


Q_SCALE = QK_DIM ** -0.5 * math.log2(math.e)


def _rope(x, cos, sin_signed):
    lane = lax.broadcasted_iota(jnp.int32, x.shape, 1)
    half = QK_DIM // 4
    partner = jnp.where((lane & half) == 0, pltpu.roll(x, LANES - half, 1), pltpu.roll(x, half, 1))
    return x * cos + partner * sin_signed


def _lambda(lam_ref, lam_init):
    lv = lam_ref[...]
    return (jnp.exp(jnp.sum(lv[0:1] * lv[1:2], axis=-1, keepdims=True))
            - jnp.exp(jnp.sum(lv[2:3] * lv[3:4], axis=-1, keepdims=True)) + lam_init)


def _scores(q, kb):
    lane = lax.broadcasted_iota(jnp.int32, q.shape, 1)
    qm = jnp.concatenate([jnp.where(lane < QK_DIM, q, 0.0), jnp.where(lane >= QK_DIM, q, 0.0)], axis=0)
    return lax.dot_general(qm.astype(_BF16), kb, (((1,), (1,)), ((), ())), preferred_element_type=_F32)


def _softmax_diff_pv(s, vb1, lam, g, lam_init):
    tq = s.shape[0] // 2
    e = jnp.exp2(s - jnp.max(s, axis=-1, keepdims=True)).astype(_BF16)
    ov = jnp.dot(e, vb1, preferred_element_type=_F32)
    o = (ov[0:tq, 0:LANES] / ov[0:tq, LANES:2 * LANES]
         - lam * (ov[tq:2 * tq, 0:LANES] / ov[tq:2 * tq, LANES:2 * LANES]))
    return _rms(o, g) * (1.0 - lam_init)


def _attn_ctx_kernel(q_ref, k_ref, v_ref, ck_ref, cv_ref, cq_ref, sq_ref, ckk_ref, skk_ref, lam_ref, g_ref,
                     att_in_ref, o_ref, kbuf, vbuf, *, past, seq, sub, lam_init):
    del att_in_ref

    @pl.when(pl.program_id(2) == 0)
    def _():
        kbuf[0:past, :] = ck_ref[...].astype(kbuf.dtype)
        kbuf[past:past + seq, :] = _rope(k_ref[...], ckk_ref[...], skk_ref[...]).astype(kbuf.dtype)
        vbuf[0:past, 0:LANES] = cv_ref[...].astype(vbuf.dtype)
        vbuf[past:past + seq, 0:LANES] = v_ref[...].astype(vbuf.dtype)
        vbuf[:, LANES:2 * LANES] = jnp.ones((past + seq, LANES), vbuf.dtype)

    lam = _lambda(lam_ref, lam_init)

    def scores(i):
        rows = slice(i * sub, (i + 1) * sub)
        return _scores(_rope(q_ref[rows, :], cq_ref[rows, :], sq_ref[rows, :]) * Q_SCALE, kbuf[...])

    s = scores(0)
    for i in range(q_ref.shape[0] // sub):
        s_next = scores(i + 1) if (i + 1) * sub < q_ref.shape[0] else None
        o = _softmax_diff_pv(s, vbuf[...], lam, g_ref[...], lam_init)
        o_ref[i * sub:(i + 1) * sub, :] = o.astype(o_ref.dtype)
        s = s_next


def _attn_prompt_kernel(*refs, heads, lam_init, aliased):
    if aliased:
        refs = refs[:5] + refs[7:]
    q_ref, k_ref, v_ref, lam_ref, g_ref, o_ref, nk_ref, nv_ref = refs
    nk_ref[...] = k_ref[...]
    nv_ref[...] = v_ref[...]
    lam = _lambda(lam_ref, lam_init)
    ones = jnp.ones((k_ref.shape[0], LANES), _BF16)
    for h in range(heads):
        hs = slice(h * LANES, (h + 1) * LANES)
        vb1 = jnp.concatenate([v_ref[:, hs].astype(_BF16), ones], axis=1)
        s = _scores(q_ref[:, hs] * Q_SCALE, k_ref[:, hs].astype(_BF16))
        o = _softmax_diff_pv(s, vb1, lam, g_ref[...], lam_init)
        o_ref[:, hs] = o.astype(o_ref.dtype)


def _attention_prompt(proj, lamv, gsub, caches, l, depth, n_seq, seq, heads, col_q, lam_init):
    width = heads * LANES
    cq = col_q // width
    aliased = caches is not None
    in_specs = [
        pl.BlockSpec((seq, width), lambda b: (b, cq)),
        pl.BlockSpec((seq, width), lambda b: (b, cq + 1)),
        pl.BlockSpec((seq, width), lambda b: (b, cq + 2)),
        pl.BlockSpec((None, 4, QK_DIM), lambda b: (l, 0, 0)),
        pl.BlockSpec((None, 1, LANES), lambda b: (l, 0, 0)),
    ]
    args = [proj, proj, proj, lamv, gsub]
    if aliased:
        in_specs += [pl.BlockSpec(memory_space=pl.ANY)] * 2
        args += list(caches)
    cache_sds = jax.ShapeDtypeStruct((n_seq, depth, seq, width), _F32)
    return pl.pallas_call(
        functools.partial(_attn_prompt_kernel, heads=heads, lam_init=lam_init, aliased=aliased),
        out_shape=(jax.ShapeDtypeStruct((proj.shape[0], width), _BF16), cache_sds, cache_sds),
        grid=(n_seq,),
        in_specs=in_specs,
        out_specs=(pl.BlockSpec((seq, width), lambda b: (b, 0)),
                   pl.BlockSpec((None, None, seq, width), lambda b: (b, l, 0, 0)),
                   pl.BlockSpec((None, None, seq, width), lambda b: (b, l, 0, 0))),
        input_output_aliases={5: 1, 6: 2} if aliased else {},
        compiler_params=_params("parallel"),
        name=f"diffattn_prompt_l{l}",
    )(*args)


def _attention_ctx(proj, att, lamv, gsub, cache_k, cache_v, cos, sin, l, first_row, n_seq, seq, tq, heads,
                   col_q, lam_init):
    past = cache_k.shape[2]
    rq0, rk0 = first_row // tq, first_row // seq
    nq = seq // tq
    hq, hk, hv = col_q // LANES, col_q // LANES + heads, col_q // LANES + 2 * heads
    in_specs = [
        pl.BlockSpec((tq, LANES), lambda b, h, t: (rq0 + b * nq + t, hq + h)),
        pl.BlockSpec((seq, LANES), lambda b, h, t: (rk0 + b, hk + h)),
        pl.BlockSpec((seq, LANES), lambda b, h, t: (rk0 + b, hv + h)),
        pl.BlockSpec((None, None, past, LANES), lambda b, h, t: (b, l, 0, h)),
        pl.BlockSpec((None, None, past, LANES), lambda b, h, t: (b, l, 0, h)),
        pl.BlockSpec((tq, LANES), lambda b, h, t: (t, 0)),
        pl.BlockSpec((tq, LANES), lambda b, h, t: (t, 0)),
        pl.BlockSpec((seq, LANES), lambda b, h, t: (0, 0)),
        pl.BlockSpec((seq, LANES), lambda b, h, t: (0, 0)),
        pl.BlockSpec((None, 4, QK_DIM), lambda b, h, t: (l, 0, 0)),
        pl.BlockSpec((None, 1, LANES), lambda b, h, t: (l, 0, 0)),
        pl.BlockSpec(memory_space=pl.ANY),
    ]
    return pl.pallas_call(
        functools.partial(_attn_ctx_kernel, past=past, seq=seq, sub=min(tq, 256), lam_init=lam_init),
        out_shape=jax.ShapeDtypeStruct(att.shape, att.dtype),
        grid=(n_seq, heads, nq),
        in_specs=in_specs,
        out_specs=pl.BlockSpec((tq, LANES), lambda b, h, t: (rq0 + b * nq + t, h)),
        scratch_shapes=[pltpu.VMEM((past + seq, LANES), _BF16), pltpu.VMEM((past + seq, 2 * LANES), _BF16)],
        input_output_aliases={11: 0},
        compiler_params=_params("parallel", "parallel", "arbitrary"),
        name=f"diffattn_ctx_l{l}",
    )(proj, proj, proj, cache_k, cache_v, cos, sin, cos, sin, lamv, gsub, att)


def _merge_kernel(x_ref, mod_ref, gpost_ref, rec_ref, att_ref, gm1_ref, gm2_ref, plru_ref, pattn_ref,
                  wout_ref, o_ref, acc_ref, *, mod_base):
    j = pl.program_id(1)

    @pl.when(j == 0)
    def _():
        acc_ref[...] = jnp.zeros_like(acc_ref)

    a = jnp.dot(rec_ref[...], plru_ref[...], preferred_element_type=_F32)
    b = jnp.dot(att_ref[...], pattn_ref[...], preferred_element_type=_F32)
    merged = _sigmoid(gm1_ref[...]) * a + _sigmoid(gm2_ref[...]) * b
    acc_ref[...] += jnp.dot(merged.astype(_BF16), wout_ref[...], preferred_element_type=_F32)

    @pl.when(j == pl.num_programs(1) - 1)
    def _():
        gt = mod_ref[mod_base + 2:mod_base + 3, :]
        o_ref[...] = x_ref[...] + gt * _rms(acc_ref[...], gpost_ref[...])


def _merge(x, mod, group_of_tile, gpost, rec, att, proj, col_gm, p_lru, p_attn, w_out, l, mod_base, tm, tn):
    n, d = x.shape
    d_rnn, v_cols = rec.shape[1], att.shape[1]
    g1, g2 = col_gm // tn, (col_gm + d) // tn
    return pl.pallas_call(
        functools.partial(_merge_kernel, mod_base=mod_base),
        out_shape=jax.ShapeDtypeStruct((n, d), _F32),
        grid=(n // tm, d // tn),
        in_specs=[
            pl.BlockSpec((tm, d), lambda r, j: (r, 0)),
            pl.BlockSpec((None, N_MOD, d), lambda r, j: (group_of_tile(r, tm), 0, 0)),
            pl.BlockSpec((1, d), lambda r, j: (0, 0)),
            pl.BlockSpec((tm, d_rnn), lambda r, j: (r, 0)),
            pl.BlockSpec((tm, v_cols), lambda r, j: (r, 0)),
            pl.BlockSpec((tm, tn), lambda r, j: (r, g1 + j)),
            pl.BlockSpec((tm, tn), lambda r, j: (r, g2 + j)),
            pl.BlockSpec((None, d_rnn, tn), lambda r, j: (l, 0, j)),
            pl.BlockSpec((None, v_cols, tn), lambda r, j: (l, 0, j)),
            pl.BlockSpec((None, tn, d), lambda r, j: (l, j, 0)),
        ],
        out_specs=pl.BlockSpec((tm, d), lambda r, j: (r, 0)),
        scratch_shapes=[pltpu.VMEM((tm, d), _F32)],
        compiler_params=_params("parallel", "arbitrary"),
        name=f"merge_l{l}",
    )(x, mod, gpost, rec, att, proj, proj, p_lru, p_attn, w_out)


def _rope_tables(n_tokens):
    t = jnp.arange(n_tokens)
    row = (t // GRID_W).astype(_F32)
    col = (t % GRID_W).astype(_F32)
    n_freq = QK_DIM // 4
    freqs = 1.0 / (ROPE_BASE ** (jnp.arange(0, 2 * n_freq, 2, dtype=_F32) / (2 * n_freq)))
    ar, ac = row[:, None] * freqs, col[:, None] * freqs
    cos = jnp.concatenate([jnp.cos(ar), jnp.cos(ar), jnp.cos(ac), jnp.cos(ac)], axis=-1)
    sin = jnp.concatenate([-jnp.sin(ar), jnp.sin(ar), -jnp.sin(ac), jnp.sin(ac)], axis=-1)
    reps = LANES // QK_DIM
    return jnp.tile(cos, (1, reps)), jnp.tile(sin, (1, reps))


def _largest_tile(candidates, *extents):
    return next(t for t in candidates if all(e % t == 0 for e in extents))


def kernel(x_prompt, x_sample, c, cache_k, cache_v, state_lru, c_ctx, w_mod, b_mod, g_pre, g_post, ffn_w1, ffn_w2, w_in, conv_w, conv_b, lru_wa, lru_ba, lru_wx, lru_bx, lru_lambda, lam_q1, lam_k1, lam_q2, lam_k2, attn_subln, p_lru, p_attn, w_out):
    batch, seq, d = x_prompt.shape
    dec_batch, dec_seq, _ = x_sample.shape
    depth = w_mod.shape[0]
    d_rnn = conv_w.shape[2]
    heads = cache_k.shape[3]
    past = cache_k.shape[2]
    v_cols = heads * cache_v.shape[4]
    qk_cols = heads * cache_k.shape[4]
    d_ff = ffn_w2.shape[2]
    n_prompt, n_sample = batch * seq, dec_batch * dec_seq
    assert cache_k.shape[4] == LANES and cache_v.shape[4] == LANES and 1 + dec_batch <= MOD_ROWS
    col_q = 2 * d_rnn
    col_gm = col_q + 2 * qk_cols + v_cols

    tm = _largest_tile((512, 256), n_prompt, dec_seq)
    tm_in = _largest_tile((1024, 512, 256), n_prompt, dec_seq)
    tq = _largest_tile((1024, 512, 256, 128), dec_seq)
    tf, tn_in, tn_merge = 512, 1024, 512

    def group_of_tile(r, rows):
        first = n_prompt // rows
        return jnp.where(r < first, 0, 1 + (r - first) // (dec_seq // rows))

    ffp = -(-d_ff // tf) * tf
    padc = ((0, 0), (0, 0), (0, 0), (0, ffp - d_ff))
    w1 = jnp.concatenate([jnp.pad(ffn_w1[..., :d_ff], padc), jnp.pad(ffn_w1[..., d_ff:], padc)],
                         axis=-1).astype(_BF16)
    w2 = jnp.pad(ffn_w2, ((0, 0), (0, 0), (0, ffp - d_ff), (0, 0))).astype(_BF16)
    w_in_b, p_lru_b, p_attn_b, w_out_b = (w.astype(_BF16) for w in (w_in, p_lru, p_attn, w_out))
    wg = (0.5 * jnp.concatenate([lru_wa[:, 0], lru_wx[:, 0], lru_wa[:, 1], lru_wx[:, 1]], axis=-1)).astype(_BF16)
    bg = 0.5 * jnp.stack([lru_ba[:, 0], lru_bx[:, 0], lru_ba[:, 1], lru_bx[:, 1]], axis=1)
    lamv = jnp.stack([lam_q1, lam_k1, lam_q2, lam_k2], axis=1)
    gsub = attn_subln.reshape(depth, 1, -1)
    conv_b3 = conv_b.reshape(depth, 1, d_rnn)
    cache_k2 = cache_k.reshape(dec_batch, depth, past, qk_cols)
    cache_v2 = cache_v.reshape(dec_batch, depth, past, v_cols)
    cos, sin = _rope_tables(dec_seq)
    zero_state = jnp.zeros((batch, 2, d_rnn), _F32)

    cond = jnp.concatenate([c_ctx[None], c, jnp.zeros((MOD_ROWS - 1 - dec_batch, d), _F32)], axis=0)
    mod_all = _modulation(cond, w_mod, b_mod).reshape(depth, MOD_ROWS, N_MOD, d)

    x = jnp.concatenate([x_prompt.reshape(n_prompt, d), x_sample.reshape(n_sample, d)], axis=0)
    caches, new_s = None, []
    for l in range(depth):
        mod = mod_all[l]
        gpre = [g_pre[l, i].reshape(1, d) for i in range(3)]
        gpost = [g_post[l, i].reshape(1, d) for i in range(3)]
        lam_init = 0.8 - 0.6 * math.exp(-0.3 * l)

        x = _ffn(x, mod, group_of_tile, gpre[0], gpost[0], w1, w2, l, 0, 0, tm, tf)
        proj = _inproj(x, mod, group_of_tile, gpre[1], w_in_b, l, 3, tm_in, tn_in)
        rec, st_p = _lru(proj, None, conv_w, conv_b3, wg, bg, lru_lambda, zero_state, l, 0, seq, d_rnn)
        rec, _ = _lru(proj, rec, conv_w, conv_b3, wg, bg, lru_lambda, state_lru[:, l], l, n_prompt, dec_seq, d_rnn)
        att, *caches = _attention_prompt(proj, lamv, gsub, caches, l, depth, batch, seq, heads, col_q, lam_init)
        att = _attention_ctx(proj, att, lamv, gsub, cache_k2, cache_v2, cos, sin, l, n_prompt, dec_batch, dec_seq,
                             tq, heads, col_q, lam_init)
        x = _merge(x, mod, group_of_tile, gpost[1], rec, att, proj, col_gm, p_lru_b, p_attn_b, w_out_b,
                   l, 3, tm, tn_merge)
        x = _ffn(x, mod, group_of_tile, gpre[2], gpost[2], w1, w2, l, 1, 6, tm, tf)
        new_s.append(st_p)

    y_prompt = x[:n_prompt].reshape(batch, seq, d)
    y_sample = x[n_prompt:].reshape(dec_batch, dec_seq, d)
    new_k, new_v = (a.reshape(batch, depth, seq, heads, -1) for a in caches)
    return (y_prompt, y_sample, new_k, new_v, jnp.stack(new_s, axis=1))
```

```python
import functools
import math

import jax
import jax.numpy as jnp
from jax import lax
from jax.experimental import pallas as pl
from jax.experimental.pallas import tpu as pltpu

NORM_EPS = 1e-6
LRU_C = 8.0
N_MOD = 9
GRID_W = 64
ROPE_BASE = 10000.0
QK_DIM = 64
CONV_PAD_LEFT = 2

LANES = 128
SUBLANES = 8
MOD_ROWS = 16
VMEM_LIMIT = 52 * 1024 * 1024

_BF16 = jnp.bfloat16
_F32 = jnp.float32


def _rms(x, g):
    ms = jnp.mean(x * x, axis=-1, keepdims=True)
    return x * lax.rsqrt(ms + NORM_EPS) * g


def _params(*sem, vmem=VMEM_LIMIT):
    return pltpu.CompilerParams(dimension_semantics=sem, vmem_limit_bytes=vmem)


def _mod_kernel(c_ref, w_ref, b_ref, o_ref):
    c = c_ref[...]
    s = (c * jax.nn.sigmoid(c)).astype(_BF16)
    o_ref[...] = jnp.dot(s, w_ref[...].astype(_BF16), preferred_element_type=_F32) + b_ref[...]


def _modulation(cond, w_mod, b_mod, tn=1024):
    depth, d, n = w_mod.shape
    return pl.pallas_call(
        _mod_kernel,
        out_shape=jax.ShapeDtypeStruct((depth, MOD_ROWS, n), _F32),
        grid=(depth, n // tn),
        in_specs=[
            pl.BlockSpec((MOD_ROWS, d), lambda l, j: (0, 0)),
            pl.BlockSpec((None, d, tn), lambda l, j: (l, 0, j)),
            pl.BlockSpec((None, 1, tn), lambda l, j: (l, 0, j)),
        ],
        out_specs=pl.BlockSpec((None, MOD_ROWS, tn), lambda l, j: (l, 0, j)),
        compiler_params=_params("parallel", "parallel"),
        name="modulation",
    )(cond, w_mod, b_mod.reshape(depth, 1, n))


NORM_ROWS = 16


def _for_row_chunks(n_rows, fn):
    def body(i, carry):
        fn(pl.ds(pl.multiple_of(i * NORM_ROWS, NORM_ROWS), NORM_ROWS))
        return carry

    lax.fori_loop(0, n_rows // NORM_ROWS, body, 0, unroll=2)


def _row_rstd(x_ref, rstd_ref):
    n_tiles = x_ref.shape[1] // LANES

    def lane_partial_sums(rows):
        sq = jnp.square(x_ref[rows, :])
        part = sq[:, 0:LANES]
        for t in range(1, n_tiles):
            part = part + sq[:, t * LANES:(t + 1) * LANES]
        rstd_ref[rows, :] = part

    _for_row_chunks(x_ref.shape[0], lane_partial_sums)
    ms = jnp.sum(rstd_ref[...], axis=-1, keepdims=True) * (1.0 / x_ref.shape[1])
    rstd_ref[...] = jnp.broadcast_to(lax.rsqrt(ms + NORM_EPS), rstd_ref.shape)


def _modulated_norm_rows(x_ref, h_ref, rstd_ref, vec_ref, g_ref, mod_ref, mod_base):
    _row_rstd(x_ref, rstd_ref)
    n_tiles = x_ref.shape[1] // LANES
    sh = mod_ref[mod_base:mod_base + 1, :]
    sc = mod_ref[mod_base + 1:mod_base + 2, :]
    vec_ref[0] = jnp.broadcast_to(g_ref[...] * (1.0 + sc), vec_ref.shape[1:])
    vec_ref[1] = jnp.broadcast_to(sh, vec_ref.shape[1:])

    def normalise(rows):
        rstd = pltpu.repeat(rstd_ref[rows, :], n_tiles, axis=1)
        h_ref[rows, :] = (x_ref[rows, :] * rstd * vec_ref[0] + vec_ref[1]).astype(h_ref.dtype)

    _for_row_chunks(x_ref.shape[0], normalise)


def _gated_residual_rows(x_ref, acc_ref, o_ref, rstd_ref, vec_ref, g_ref, mod_ref, gate_row, gate_scale):
    _row_rstd(acc_ref, rstd_ref)
    n_tiles = x_ref.shape[1] // LANES
    gt = mod_ref[gate_row:gate_row + 1, :]
    if gate_scale != 1.0:
        gt = gate_scale * gt
    vec_ref[0] = jnp.broadcast_to(gt * g_ref[...], vec_ref.shape[1:])

    def residual(rows):
        rstd = pltpu.repeat(rstd_ref[rows, :], n_tiles, axis=1)
        o_ref[rows, :] = x_ref[rows, :] + acc_ref[rows, :] * rstd * vec_ref[0]

    _for_row_chunks(x_ref.shape[0], residual)


def _on_part(r, n_first, refs_first, refs_second, fn):
    if all(a is b for a, b in zip(refs_first, refs_second)):
        fn(*refs_first)
        return
    pl.when(r < n_first)(lambda: fn(*refs_first))
    pl.when(r >= n_first)(lambda: fn(*refs_second))


def _ffn_kernel(*refs, mod_base, n_first, split_in, split_out):
    refs = list(refs)
    xa_ref = refs.pop(0)
    xb_ref = refs.pop(0) if split_in else xa_ref
    mod_ref, gpre_ref, gpost_ref, w1g_ref, w1u_ref, w2_ref, oa_ref = refs[:7]
    ob_ref = refs[7] if split_out else oa_ref
    h_ref, acc_ref, rstd_ref, vec_ref = refs[-4:]
    r, j = pl.program_id(0), pl.program_id(1)

    @pl.when(j == 0)
    def _():
        _on_part(r, n_first, (xa_ref,), (xb_ref,),
                 lambda x_ref: _modulated_norm_rows(x_ref, h_ref, rstd_ref, vec_ref, gpre_ref, mod_ref,
                                                    mod_base))
        acc_ref[...] = jnp.zeros_like(acc_ref)

    h = h_ref[...]
    g = jnp.dot(h, w1g_ref[...], preferred_element_type=_F32)
    u = jnp.dot(h, w1u_ref[...], preferred_element_type=_F32)
    act = (g * jax.nn.sigmoid(g) * u).astype(_BF16)
    acc_ref[...] += jnp.dot(act, w2_ref[...], preferred_element_type=_F32)

    @pl.when(j == pl.num_programs(1) - 1)
    def _():
        _on_part(r, n_first, (xa_ref, oa_ref), (xb_ref, ob_ref),
                 lambda x_ref, o_ref: _gated_residual_rows(x_ref, acc_ref, o_ref, rstd_ref, vec_ref, gpost_ref,
                                                           mod_ref, mod_base + 2, 0.5))


def _ffn(xs, mod, group_of_tile, gpre, gpost, w1, w2, l, i, mod_base, tm, tf, n_first, split_out):
    d = xs[0].shape[1]
    n = sum(x.shape[0] for x in xs)
    split_in = len(xs) == 2
    nff = w2.shape[2] // tf

    def first(r, j):
        return jnp.minimum(r, n_first - 1), 0

    def second(r, j):
        return jnp.maximum(r - n_first, 0), 0

    def whole(r, j):
        return r, 0

    x_specs = [pl.BlockSpec((tm, d), first), pl.BlockSpec((tm, d), second)] if split_in else [pl.BlockSpec((tm, d), whole)]
    if split_out:
        out_shape = (jax.ShapeDtypeStruct((n_first * tm, d), _F32), jax.ShapeDtypeStruct((n - n_first * tm, d), _F32))
        out_specs = (pl.BlockSpec((tm, d), first), pl.BlockSpec((tm, d), second))
    else:
        out_shape = jax.ShapeDtypeStruct((n, d), _F32)
        out_specs = pl.BlockSpec((tm, d), whole)
    return pl.pallas_call(
        functools.partial(_ffn_kernel, mod_base=mod_base, n_first=n_first, split_in=split_in, split_out=split_out),
        out_shape=out_shape,
        grid=(n // tm, nff),
        in_specs=x_specs + [
            pl.BlockSpec((None, N_MOD, d), lambda r, j: (group_of_tile(r, tm), 0, 0)),
            pl.BlockSpec((1, d), lambda r, j: (0, 0)),
            pl.BlockSpec((1, d), lambda r, j: (0, 0)),
            pl.BlockSpec((None, None, d, tf), lambda r, j: (l, i, 0, j)),
            pl.BlockSpec((None, None, d, tf), lambda r, j: (l, i, 0, j + nff)),
            pl.BlockSpec((None, None, tf, d), lambda r, j: (l, i, j, 0)),
        ],
        out_specs=out_specs,
        scratch_shapes=[pltpu.VMEM((tm, d), _BF16), pltpu.VMEM((tm, d), _F32), pltpu.VMEM((tm, LANES), _F32),
                        pltpu.VMEM((2, NORM_ROWS, d), _F32)],
        compiler_params=_params("parallel", "arbitrary"),
        name=f"ffn_l{l}_{i}",
    )(*xs, mod, gpre, gpost, w1, w1, w2)


def _inproj_kernel(x_ref, mod_ref, gpre_ref, w_ref, o_ref, h_ref, rstd_ref, vec_ref, *, mod_base):
    @pl.when(pl.program_id(1) == 0)
    def _():
        _modulated_norm_rows(x_ref, h_ref, rstd_ref, vec_ref, gpre_ref, mod_ref, mod_base)

    o_ref[...] = jnp.dot(h_ref[...], w_ref[...], preferred_element_type=_F32)


def _inproj(x, mod, group_of_tile, gpre, w_in, l, mod_base, tm, tn):
    n, d = x.shape
    cols = w_in.shape[2]
    return pl.pallas_call(
        functools.partial(_inproj_kernel, mod_base=mod_base),
        out_shape=jax.ShapeDtypeStruct((n, cols), _F32),
        grid=(n // tm, cols // tn),
        in_specs=[
            pl.BlockSpec((tm, d), lambda r, j: (r, 0)),
            pl.BlockSpec((None, N_MOD, d), lambda r, j: (group_of_tile(r, tm), 0, 0)),
            pl.BlockSpec((1, d), lambda r, j: (0, 0)),
            pl.BlockSpec((None, d, tn), lambda r, j: (l, 0, j)),
        ],
        out_specs=pl.BlockSpec((tm, tn), lambda r, j: (r, j)),
        scratch_shapes=[pltpu.VMEM((tm, d), _BF16), pltpu.VMEM((tm, LANES), _F32),
                        pltpu.VMEM((2, NORM_ROWS, d), _F32)],
        compiler_params=_params("parallel", "arbitrary"),
        name=f"inproj_l{l}",
    )(x, mod, gpre, w_in)


LRU_TILES = SUBLANES // 2
CONV_HALO = SUBLANES


def _sigmoid(x):
    return 0.5 * jnp.tanh(0.5 * x) + 0.5


def _lru_kernel(*refs, seq, row_chunk, unroll, aliased):
    if aliased:
        refs = refs[:8] + refs[9:]
    (xr_ref, gr_ref, cw_ref, cb_ref, wg_ref, bg_ref, lam_ref, h0_ref, rec_ref, st_ref,
     xp_ref, a_ref, b_ref, h_ref) = refs
    nt = LRU_TILES
    cg = nt * LANES
    zeros = jnp.zeros((CONV_HALO, cg), _F32)
    xp_ref[0:CONV_HALO, :] = zeros
    xp_ref[CONV_HALO + seq:2 * CONV_HALO + seq, :] = zeros
    xp_ref[CONV_HALO:CONV_HALO + seq, :] = xr_ref[...]

    neg_log_a_scale = (0.5 * LRU_C) * jax.nn.softplus(-lam_ref[...])
    exp2_scale = -math.log2(math.e) * neg_log_a_scale

    def coeffs(c, carry):
        r0 = c * row_chunk
        for n in range(nt):
            cs = slice(n * LANES, (n + 1) * LANES)
            y = cb_ref[:, cs]
            for tap in range(cw_ref.shape[0]):
                y = y + cw_ref[tap:tap + 1, cs] * xp_ref[pl.ds(r0 + CONV_HALO + tap - CONV_PAD_LEFT, row_chunk), cs]
            g4 = jnp.dot(y.astype(_BF16), wg_ref[n], preferred_element_type=_F32)
            half_y = 0.5 * y
            for direction in range(2):
                ua = 1.0 + jnp.tanh(g4[:, (2 * direction) * LANES:(2 * direction + 1) * LANES]
                                    + bg_ref[2 * direction:2 * direction + 1, cs])
                ux = 1.0 + jnp.tanh(g4[:, (2 * direction + 1) * LANES:(2 * direction + 2) * LANES]
                                    + bg_ref[2 * direction + 1:2 * direction + 2, cs])
                a = jnp.exp2(ua * exp2_scale[direction:direction + 1, cs])
                one_minus_a2 = jnp.tanh(ua * neg_log_a_scale[direction:direction + 1, cs]) * (1.0 + a * a)
                mult = jnp.where(one_minus_a2 > 0.0, one_minus_a2 * lax.rsqrt(one_minus_a2), 0.0)
                dense_rows = pl.ds(r0 * SUBLANES + nt * direction + n, row_chunk, stride=SUBLANES)
                a_ref[dense_rows, :] = a
                b_ref[dense_rows, :] = mult * ux * half_y
        return carry

    for c in range(seq // row_chunk):
        coeffs(c, 0)

    is_fwd = lax.broadcasted_iota(jnp.int32, (SUBLANES, LANES), 0) < nt

    def scan(c, h):
        for r in range(unroll):
            j = c * unroll + r
            jf = pl.multiple_of(j * SUBLANES, SUBLANES)
            jb = pl.multiple_of((seq - 1 - j) * SUBLANES, SUBLANES)
            a = jnp.where(is_fwd, a_ref[pl.ds(jf, SUBLANES), :], a_ref[pl.ds(jb, SUBLANES), :])
            b = jnp.where(is_fwd, b_ref[pl.ds(jf, SUBLANES), :], b_ref[pl.ds(jb, SUBLANES), :])
            h = a * h + b
            h_ref[pl.ds(jf, nt), :] = h[0:nt]
            h_ref[pl.ds(jb + nt, nt), :] = h[nt:2 * nt]
        return h

    h0 = jnp.concatenate([h0_ref[d:d + 1, n * LANES:(n + 1) * LANES] for d in range(2) for n in range(nt)], axis=0)
    h = lax.fori_loop(0, seq // unroll, scan, h0)
    for n in range(nt):
        st_ref[0:1, n * LANES:(n + 1) * LANES] = h[n:n + 1]
        st_ref[1:2, n * LANES:(n + 1) * LANES] = h[nt + n:nt + n + 1]

    def gate(c, carry):
        r0 = pl.multiple_of(c * row_chunk, row_chunk)
        for n in range(nt):
            cs = slice(n * LANES, (n + 1) * LANES)
            hf = h_ref[pl.ds(r0 * SUBLANES + n, row_chunk, stride=SUBLANES), :]
            hb = h_ref[pl.ds(r0 * SUBLANES + nt + n, row_chunk, stride=SUBLANES), :]
            rec_ref[pl.ds(r0, row_chunk), cs] = (
                (hf + hb) * jax.nn.gelu(gr_ref[pl.ds(r0, row_chunk), cs])).astype(rec_ref.dtype)
        return carry

    lax.fori_loop(0, seq // row_chunk, gate, 0)


def _lru(proj, rec, conv_w, conv_b, wg, bg, lam, h0, l, first_row, seq, d_rnn):
    n_seq = h0.shape[0]
    rb0 = first_row // seq
    cg = LRU_TILES * LANES
    ncg = d_rnn // cg
    row_chunk = min(seq, 256)
    aliased = rec is not None
    in_specs = [
        pl.BlockSpec((seq, cg), lambda s, c: (rb0 + s, c)),
        pl.BlockSpec((seq, cg), lambda s, c: (rb0 + s, ncg + c)),
        pl.BlockSpec((None, conv_w.shape[1], cg), lambda s, c: (l, 0, c)),
        pl.BlockSpec((None, 1, cg), lambda s, c: (l, 0, c)),
        pl.BlockSpec((None, LRU_TILES, LANES, 4 * LANES), lambda s, c: (l, c, 0, 0)),
        pl.BlockSpec((None, 4, cg), lambda s, c: (l, 0, c)),
        pl.BlockSpec((None, 2, cg), lambda s, c: (l, 0, c)),
        pl.BlockSpec((None, 2, cg), lambda s, c: (s, 0, c)),
    ]
    args = [proj, proj, conv_w, conv_b, wg, bg, lam, h0]
    if aliased:
        in_specs.append(pl.BlockSpec(memory_space=pl.ANY))
        args.append(rec)
    return pl.pallas_call(
        functools.partial(_lru_kernel, seq=seq, row_chunk=row_chunk, unroll=8, aliased=aliased),
        out_shape=(jax.ShapeDtypeStruct((proj.shape[0], d_rnn), _BF16),
                   jax.ShapeDtypeStruct((n_seq, 2, d_rnn), _F32)),
        grid=(n_seq, ncg),
        in_specs=in_specs,
        out_specs=(pl.BlockSpec((seq, cg), lambda s, c: (rb0 + s, c)),
                   pl.BlockSpec((None, 2, cg), lambda s, c: (s, 0, c))),
        scratch_shapes=[pltpu.VMEM((seq + 2 * CONV_HALO, cg), _F32)] + [pltpu.VMEM((seq * SUBLANES, LANES), _F32)] * 3,
        input_output_aliases={8: 0} if aliased else {},
        compiler_params=_params("parallel", "parallel", vmem=58 * 1024 * 1024),
        name=f"rglru_l{l}_t{seq}",
    )(*args)


Q_SCALE = QK_DIM ** -0.5 * math.log2(math.e)


def _rope(x, cos, sin_signed):
    lane = lax.broadcasted_iota(jnp.int32, x.shape, 1)
    half = QK_DIM // 4
    partner = jnp.where((lane & half) == 0, pltpu.roll(x, LANES - half, 1), pltpu.roll(x, half, 1))
    return x * cos + partner * sin_signed


def _lambda(lam_ref, lam_init):
    lv = lam_ref[...]
    return (jnp.exp(jnp.sum(lv[0:1] * lv[1:2], axis=-1, keepdims=True))
            - jnp.exp(jnp.sum(lv[2:3] * lv[3:4], axis=-1, keepdims=True)) + lam_init)


def _scores(q, kb):
    lane = lax.broadcasted_iota(jnp.int32, q.shape, 1)
    qm = jnp.concatenate([jnp.where(lane < QK_DIM, q, 0.0), jnp.where(lane >= QK_DIM, q, 0.0)], axis=0)
    return lax.dot_general(qm.astype(_BF16), kb, (((1,), (1,)), ((), ())), preferred_element_type=_F32)


def _softmax_diff_pv(s, vb1, lam, g, lam_init):
    tq = s.shape[0] // 2
    e = jnp.exp2(s - jnp.max(s, axis=-1, keepdims=True)).astype(_BF16)
    ov = jnp.dot(e, vb1, preferred_element_type=_F32)
    o = (ov[0:tq, 0:LANES] / ov[0:tq, LANES:2 * LANES]
         - lam * (ov[tq:2 * tq, 0:LANES] / ov[tq:2 * tq, LANES:2 * LANES]))
    return _rms(o, g) * (1.0 - lam_init)


def _attn_ctx_kernel(q_ref, k_ref, v_ref, ck_ref, cv_ref, cq_ref, sq_ref, ckk_ref, skk_ref, lam_ref, g_ref,
                     att_in_ref, o_ref, kbuf, vbuf, *, past, seq, sub, lam_init):
    del att_in_ref

    @pl.when(pl.program_id(2) == 0)
    def _():
        kbuf[0:past, :] = ck_ref[...].astype(kbuf.dtype)
        kbuf[past:past + seq, :] = _rope(k_ref[...], ckk_ref[...], skk_ref[...]).astype(kbuf.dtype)
        vbuf[0:past, 0:LANES] = cv_ref[...].astype(vbuf.dtype)
        vbuf[past:past + seq, 0:LANES] = v_ref[...].astype(vbuf.dtype)
        vbuf[:, LANES:2 * LANES] = jnp.ones((past + seq, LANES), vbuf.dtype)

    lam = _lambda(lam_ref, lam_init)

    def scores(i):
        rows = slice(i * sub, (i + 1) * sub)
        return _scores(_rope(q_ref[rows, :], cq_ref[rows, :], sq_ref[rows, :]) * Q_SCALE, kbuf[...])

    s = scores(0)
    for i in range(q_ref.shape[0] // sub):
        s_next = scores(i + 1) if (i + 1) * sub < q_ref.shape[0] else None
        o = _softmax_diff_pv(s, vbuf[...], lam, g_ref[...], lam_init)
        o_ref[i * sub:(i + 1) * sub, :] = o.astype(o_ref.dtype)
        s = s_next


def _attn_prompt_kernel(*refs, heads, lam_init, aliased):
    if aliased:
        refs = refs[:5] + refs[7:]
    q_ref, k_ref, v_ref, lam_ref, g_ref, o_ref, nk_ref, nv_ref = refs
    nk_ref[...] = k_ref[...]
    nv_ref[...] = v_ref[...]
    lam = _lambda(lam_ref, lam_init)
    ones = jnp.ones((k_ref.shape[0], LANES), _BF16)
    for h in range(heads):
        hs = slice(h * LANES, (h + 1) * LANES)
        vb1 = jnp.concatenate([v_ref[:, hs].astype(_BF16), ones], axis=1)
        s = _scores(q_ref[:, hs] * Q_SCALE, k_ref[:, hs].astype(_BF16))
        o = _softmax_diff_pv(s, vb1, lam, g_ref[...], lam_init)
        o_ref[:, hs] = o.astype(o_ref.dtype)


def _attention_prompt(proj, lamv, gsub, caches, l, depth, n_seq, seq, heads, col_q, lam_init):
    width = heads * LANES
    cq = col_q // width
    aliased = caches is not None
    in_specs = [
        pl.BlockSpec((seq, width), lambda b: (b, cq)),
        pl.BlockSpec((seq, width), lambda b: (b, cq + 1)),
        pl.BlockSpec((seq, width), lambda b: (b, cq + 2)),
        pl.BlockSpec((None, 4, QK_DIM), lambda b: (l, 0, 0)),
        pl.BlockSpec((None, 1, LANES), lambda b: (l, 0, 0)),
    ]
    args = [proj, proj, proj, lamv, gsub]
    if aliased:
        in_specs += [pl.BlockSpec(memory_space=pl.ANY)] * 2
        args += list(caches)
    cache_sds = jax.ShapeDtypeStruct((n_seq, depth, seq, width), _F32)
    return pl.pallas_call(
        functools.partial(_attn_prompt_kernel, heads=heads, lam_init=lam_init, aliased=aliased),
        out_shape=(jax.ShapeDtypeStruct((proj.shape[0], width), _BF16), cache_sds, cache_sds),
        grid=(n_seq,),
        in_specs=in_specs,
        out_specs=(pl.BlockSpec((seq, width), lambda b: (b, 0)),
                   pl.BlockSpec((None, None, seq, width), lambda b: (b, l, 0, 0)),
                   pl.BlockSpec((None, None, seq, width), lambda b: (b, l, 0, 0))),
        input_output_aliases={5: 1, 6: 2} if aliased else {},
        compiler_params=_params("parallel"),
        name=f"diffattn_prompt_l{l}",
    )(*args)


def _attention_ctx(proj, att, lamv, gsub, cache_k, cache_v, cos, sin, l, first_row, n_seq, seq, tq, heads,
                   col_q, lam_init):
    past = cache_k.shape[2]
    rq0, rk0 = first_row // tq, first_row // seq
    nq = seq // tq
    hq, hk, hv = col_q // LANES, col_q // LANES + heads, col_q // LANES + 2 * heads
    in_specs = [
        pl.BlockSpec((tq, LANES), lambda b, h, t: (rq0 + b * nq + t, hq + h)),
        pl.BlockSpec((seq, LANES), lambda b, h, t: (rk0 + b, hk + h)),
        pl.BlockSpec((seq, LANES), lambda b, h, t: (rk0 + b, hv + h)),
        pl.BlockSpec((None, None, past, LANES), lambda b, h, t: (b, l, 0, h)),
        pl.BlockSpec((None, None, past, LANES), lambda b, h, t: (b, l, 0, h)),
        pl.BlockSpec((tq, LANES), lambda b, h, t: (t, 0)),
        pl.BlockSpec((tq, LANES), lambda b, h, t: (t, 0)),
        pl.BlockSpec((seq, LANES), lambda b, h, t: (0, 0)),
        pl.BlockSpec((seq, LANES), lambda b, h, t: (0, 0)),
        pl.BlockSpec((None, 4, QK_DIM), lambda b, h, t: (l, 0, 0)),
        pl.BlockSpec((None, 1, LANES), lambda b, h, t: (l, 0, 0)),
        pl.BlockSpec(memory_space=pl.ANY),
    ]
    return pl.pallas_call(
        functools.partial(_attn_ctx_kernel, past=past, seq=seq, sub=min(tq, 256), lam_init=lam_init),
        out_shape=jax.ShapeDtypeStruct(att.shape, att.dtype),
        grid=(n_seq, heads, nq),
        in_specs=in_specs,
        out_specs=pl.BlockSpec((tq, LANES), lambda b, h, t: (rq0 + b * nq + t, h)),
        scratch_shapes=[pltpu.VMEM((past + seq, LANES), _BF16), pltpu.VMEM((past + seq, 2 * LANES), _BF16)],
        input_output_aliases={11: 0},
        compiler_params=_params("parallel", "parallel", "arbitrary"),
        name=f"diffattn_ctx_l{l}",
    )(proj, proj, proj, cache_k, cache_v, cos, sin, cos, sin, lamv, gsub, att)


def _merge_kernel(x_ref, mod_ref, gpost_ref, rec_ref, att_ref, gm1_ref, gm2_ref, plru_ref, pattn_ref,
                  wout_ref, o_ref, acc_ref, rstd_ref, vec_ref, *, mod_base):
    j = pl.program_id(1)

    @pl.when(j == 0)
    def _():
        acc_ref[...] = jnp.zeros_like(acc_ref)

    a = jnp.dot(rec_ref[...], plru_ref[...], preferred_element_type=_F32)
    b = jnp.dot(att_ref[...], pattn_ref[...], preferred_element_type=_F32)
    merged = _sigmoid(gm1_ref[...]) * a + _sigmoid(gm2_ref[...]) * b
    acc_ref[...] += jnp.dot(merged.astype(_BF16), wout_ref[...], preferred_element_type=_F32)

    @pl.when(j == pl.num_programs(1) - 1)
    def _():
        _gated_residual_rows(x_ref, acc_ref, o_ref, rstd_ref, vec_ref, gpost_ref, mod_ref, mod_base + 2, 1.0)


def _merge(x, mod, group_of_tile, gpost, rec, att, proj, col_gm, p_lru, p_attn, w_out, l, mod_base, tm, tn):
    n, d = x.shape
    d_rnn, v_cols = rec.shape[1], att.shape[1]
    g1, g2 = col_gm // tn, (col_gm + d) // tn
    return pl.pallas_call(
        functools.partial(_merge_kernel, mod_base=mod_base),
        out_shape=jax.ShapeDtypeStruct((n, d), _F32),
        grid=(n // tm, d // tn),
        in_specs=[
            pl.BlockSpec((tm, d), lambda r, j: (r, 0)),
            pl.BlockSpec((None, N_MOD, d), lambda r, j: (group_of_tile(r, tm), 0, 0)),
            pl.BlockSpec((1, d), lambda r, j: (0, 0)),
            pl.BlockSpec((tm, d_rnn), lambda r, j: (r, 0)),
            pl.BlockSpec((tm, v_cols), lambda r, j: (r, 0)),
            pl.BlockSpec((tm, tn), lambda r, j: (r, g1 + j)),
            pl.BlockSpec((tm, tn), lambda r, j: (r, g2 + j)),
            pl.BlockSpec((None, d_rnn, tn), lambda r, j: (l, 0, j)),
            pl.BlockSpec((None, v_cols, tn), lambda r, j: (l, 0, j)),
            pl.BlockSpec((None, tn, d), lambda r, j: (l, j, 0)),
        ],
        out_specs=pl.BlockSpec((tm, d), lambda r, j: (r, 0)),
        scratch_shapes=[pltpu.VMEM((tm, d), _F32), pltpu.VMEM((tm, LANES), _F32),
                        pltpu.VMEM((2, NORM_ROWS, d), _F32)],
        compiler_params=_params("parallel", "arbitrary"),
        name=f"merge_l{l}",
    )(x, mod, gpost, rec, att, proj, proj, p_lru, p_attn, w_out)


def _rope_tables(n_tokens):
    t = jnp.arange(n_tokens)
    row = (t // GRID_W).astype(_F32)
    col = (t % GRID_W).astype(_F32)
    n_freq = QK_DIM // 4
    freqs = 1.0 / (ROPE_BASE ** (jnp.arange(0, 2 * n_freq, 2, dtype=_F32) / (2 * n_freq)))
    ar, ac = row[:, None] * freqs, col[:, None] * freqs
    cos = jnp.concatenate([jnp.cos(ar), jnp.cos(ar), jnp.cos(ac), jnp.cos(ac)], axis=-1)
    sin = jnp.concatenate([-jnp.sin(ar), jnp.sin(ar), -jnp.sin(ac), jnp.sin(ac)], axis=-1)
    reps = LANES // QK_DIM
    return jnp.tile(cos, (1, reps)), jnp.tile(sin, (1, reps))


def _largest_tile(candidates, *extents):
    return next(t for t in candidates if all(e % t == 0 for e in extents))


def kernel(x_prompt, x_sample, c, cache_k, cache_v, state_lru, c_ctx, w_mod, b_mod, g_pre, g_post, ffn_w1, ffn_w2, w_in, conv_w, conv_b, lru_wa, lru_ba, lru_wx, lru_bx, lru_lambda, lam_q1, lam_k1, lam_q2, lam_k2, attn_subln, p_lru, p_attn, w_out):
    batch, seq, d = x_prompt.shape
    dec_batch, dec_seq, _ = x_sample.shape
    depth = w_mod.shape[0]
    d_rnn = conv_w.shape[2]
    heads = cache_k.shape[3]
    past = cache_k.shape[2]
    v_cols = heads * cache_v.shape[4]
    qk_cols = heads * cache_k.shape[4]
    d_ff = ffn_w2.shape[2]
    n_prompt, n_sample = batch * seq, dec_batch * dec_seq
    assert cache_k.shape[4] == LANES and cache_v.shape[4] == LANES and 1 + dec_batch <= MOD_ROWS
    col_q = 2 * d_rnn
    col_gm = col_q + 2 * qk_cols + v_cols

    tm = _largest_tile((512, 256), n_prompt, dec_seq)
    tm_in = _largest_tile((1024, 512, 256), n_prompt, dec_seq)
    tq = _largest_tile((1024, 512, 256, 128), dec_seq)
    tf, tn_in, tn_merge = 512, 1024, 512

    def group_of_tile(r, rows):
        first = n_prompt // rows
        return jnp.where(r < first, 0, 1 + (r - first) // (dec_seq // rows))

    ffp = -(-d_ff // tf) * tf
    padc = ((0, 0), (0, 0), (0, 0), (0, ffp - d_ff))
    w1 = jnp.concatenate([jnp.pad(ffn_w1[..., :d_ff], padc), jnp.pad(ffn_w1[..., d_ff:], padc)],
                         axis=-1).astype(_BF16)
    w2 = jnp.pad(ffn_w2, ((0, 0), (0, 0), (0, ffp - d_ff), (0, 0))).astype(_BF16)
    w_in_b, p_lru_b, p_attn_b, w_out_b = (w.astype(_BF16) for w in (w_in, p_lru, p_attn, w_out))
    wg = (0.5 * jnp.concatenate([lru_wa[:, 0], lru_wx[:, 0], lru_wa[:, 1], lru_wx[:, 1]], axis=-1)).astype(_BF16)
    bg = 0.5 * jnp.stack([lru_ba[:, 0], lru_bx[:, 0], lru_ba[:, 1], lru_bx[:, 1]], axis=1)
    lamv = jnp.stack([lam_q1, lam_k1, lam_q2, lam_k2], axis=1)
    gsub = attn_subln.reshape(depth, 1, -1)
    conv_b3 = conv_b.reshape(depth, 1, d_rnn)
    cache_k2 = cache_k.reshape(dec_batch, depth, past, qk_cols)
    cache_v2 = cache_v.reshape(dec_batch, depth, past, v_cols)
    cos, sin = _rope_tables(dec_seq)
    zero_state = jnp.zeros((batch, 2, d_rnn), _F32)

    cond = jnp.concatenate([c_ctx[None], c, jnp.zeros((MOD_ROWS - 1 - dec_batch, d), _F32)], axis=0)
    mod_all = _modulation(cond, w_mod, b_mod).reshape(depth, MOD_ROWS, N_MOD, d)

    xs = (x_prompt.reshape(n_prompt, d), x_sample.reshape(n_sample, d))
    n_first = n_prompt // tm
    caches, new_s = None, []
    for l in range(depth):
        mod = mod_all[l]
        gpre = [g_pre[l, i].reshape(1, d) for i in range(3)]
        gpost = [g_post[l, i].reshape(1, d) for i in range(3)]
        lam_init = 0.8 - 0.6 * math.exp(-0.3 * l)

        x = _ffn(xs, mod, group_of_tile, gpre[0], gpost[0], w1, w2, l, 0, 0, tm, tf, n_first, False)
        proj = _inproj(x, mod, group_of_tile, gpre[1], w_in_b, l, 3, tm_in, tn_in)
        rec, st_p = _lru(proj, None, conv_w, conv_b3, wg, bg, lru_lambda, zero_state, l, 0, seq, d_rnn)
        rec, _ = _lru(proj, rec, conv_w, conv_b3, wg, bg, lru_lambda, state_lru[:, l], l, n_prompt, dec_seq, d_rnn)
        att, *caches = _attention_prompt(proj, lamv, gsub, caches, l, depth, batch, seq, heads, col_q, lam_init)
        att = _attention_ctx(proj, att, lamv, gsub, cache_k2, cache_v2, cos, sin, l, n_prompt, dec_batch, dec_seq,
                             tq, heads, col_q, lam_init)
        x = _merge(x, mod, group_of_tile, gpost[1], rec, att, proj, col_gm, p_lru_b, p_attn_b, w_out_b,
                   l, 3, tm, tn_merge)
        last = l == depth - 1
        xs = _ffn((x,), mod, group_of_tile, gpre[2], gpost[2], w1, w2, l, 1, 6, tm, tf, n_first, last)
        xs = xs if last else (xs,)
        new_s.append(st_p)

    y_prompt = xs[0].reshape(batch, seq, d)
    y_sample = xs[1].reshape(dec_batch, dec_seq, d)
    new_k, new_v = (a.reshape(batch, depth, seq, heads, -1) for a in caches)
    return (y_prompt, y_sample, new_k, new_v, jnp.stack(new_s, axis=1))
```

```python
import functools
import math

import jax
import jax.numpy as jnp
from jax import lax
from jax.experimental import pallas as pl
from jax.experimental.pallas import tpu as pltpu

NORM_EPS = 1e-6
LRU_C = 8.0
N_MOD = 9
GRID_W = 64
ROPE_BASE = 10000.0
QK_DIM = 64
CONV_PAD_LEFT = 2

LANES = 128
SUBLANES = 8
MOD_ROWS = 16
VMEM_LIMIT = 52 * 1024 * 1024
VMEM_LIMIT_LARGE = 58 * 1024 * 1024

_BF16 = jnp.bfloat16
_F32 = jnp.float32


def _rms(x, g):
    ms = jnp.mean(x * x, axis=-1, keepdims=True)
    return x * lax.rsqrt(ms + NORM_EPS) * g


def _params(*sem, vmem=VMEM_LIMIT):
    return pltpu.CompilerParams(dimension_semantics=sem, vmem_limit_bytes=vmem)


def _mod_kernel(c_ref, w_ref, b_ref, o_ref):
    c = c_ref[...]
    s = (c * jax.nn.sigmoid(c)).astype(_BF16)
    o_ref[...] = jnp.dot(s, w_ref[...].astype(_BF16), preferred_element_type=_F32) + b_ref[...]


def _modulation(cond, w_mod, b_mod, tn=1024):
    depth, d, n = w_mod.shape
    return pl.pallas_call(
        _mod_kernel,
        out_shape=jax.ShapeDtypeStruct((depth, MOD_ROWS, n), _F32),
        grid=(depth, n // tn),
        in_specs=[
            pl.BlockSpec((MOD_ROWS, d), lambda l, j: (0, 0)),
            pl.BlockSpec((None, d, tn), lambda l, j: (l, 0, j)),
            pl.BlockSpec((None, 1, tn), lambda l, j: (l, 0, j)),
        ],
        out_specs=pl.BlockSpec((None, MOD_ROWS, tn), lambda l, j: (l, 0, j)),
        compiler_params=_params("parallel", "parallel"),
        name="modulation",
    )(cond, w_mod, b_mod.reshape(depth, 1, n))


NORM_ROWS = 16


def _for_row_chunks(n_rows, fn):
    def body(i, carry):
        fn(pl.ds(pl.multiple_of(i * NORM_ROWS, NORM_ROWS), NORM_ROWS))
        return carry

    lax.fori_loop(0, n_rows // NORM_ROWS, body, 0, unroll=2)


def _row_rstd(x_ref, rstd_ref):
    n_tiles = x_ref.shape[1] // LANES

    def lane_partial_sums(rows):
        sq = jnp.square(x_ref[rows, :])
        part = sq[:, 0:LANES]
        for t in range(1, n_tiles):
            part = part + sq[:, t * LANES:(t + 1) * LANES]
        rstd_ref[rows, :] = part

    _for_row_chunks(x_ref.shape[0], lane_partial_sums)
    ms = jnp.sum(rstd_ref[...], axis=-1, keepdims=True) * (1.0 / x_ref.shape[1])
    rstd_ref[...] = jnp.broadcast_to(lax.rsqrt(ms + NORM_EPS), rstd_ref.shape)


def _modulated_norm_rows(x_ref, h_ref, rstd_ref, vec_ref, g_ref, mod_ref, mod_base):
    _row_rstd(x_ref, rstd_ref)
    n_tiles = x_ref.shape[1] // LANES
    sh = mod_ref[mod_base:mod_base + 1, :]
    sc = mod_ref[mod_base + 1:mod_base + 2, :]
    vec_ref[0] = jnp.broadcast_to(g_ref[...] * (1.0 + sc), vec_ref.shape[1:])
    vec_ref[1] = jnp.broadcast_to(sh, vec_ref.shape[1:])

    def normalise(rows):
        rstd = jnp.tile(rstd_ref[rows, :], (1, n_tiles))
        h_ref[rows, :] = (x_ref[rows, :] * rstd * vec_ref[0] + vec_ref[1]).astype(h_ref.dtype)

    _for_row_chunks(x_ref.shape[0], normalise)


def _gated_residual_rows(x_ref, acc_ref, o_ref, rstd_ref, vec_ref, g_ref, mod_ref, gate_row, gate_scale):
    _row_rstd(acc_ref, rstd_ref)
    n_tiles = x_ref.shape[1] // LANES
    gt = mod_ref[gate_row:gate_row + 1, :]
    if gate_scale != 1.0:
        gt = gate_scale * gt
    vec_ref[0] = jnp.broadcast_to(gt * g_ref[...], vec_ref.shape[1:])

    def residual(rows):
        rstd = jnp.tile(rstd_ref[rows, :], (1, n_tiles))
        o_ref[rows, :] = x_ref[rows, :] + acc_ref[rows, :] * rstd * vec_ref[0]

    _for_row_chunks(x_ref.shape[0], residual)


def _on_part(r, n_first, refs_first, refs_second, fn):
    if all(a is b for a, b in zip(refs_first, refs_second)):
        fn(*refs_first)
        return
    pl.when(r < n_first)(lambda: fn(*refs_first))
    pl.when(r >= n_first)(lambda: fn(*refs_second))


def _ffn_kernel(*refs, mod_base, n_first, split_in, split_out):
    refs = list(refs)
    xa_ref = refs.pop(0)
    xb_ref = refs.pop(0) if split_in else xa_ref
    mod_ref, gpre_ref, gpost_ref, w1g_ref, w1u_ref, w2_ref, oa_ref = refs[:7]
    ob_ref = refs[7] if split_out else oa_ref
    h_ref, acc_ref, rstd_ref, vec_ref = refs[-4:]
    r, j = pl.program_id(0), pl.program_id(1)

    @pl.when(j == 0)
    def _():
        _on_part(r, n_first, (xa_ref,), (xb_ref,),
                 lambda x_ref: _modulated_norm_rows(x_ref, h_ref, rstd_ref, vec_ref, gpre_ref, mod_ref,
                                                    mod_base))
        acc_ref[...] = jnp.zeros_like(acc_ref)

    h = h_ref[...]
    g = jnp.dot(h, w1g_ref[...], preferred_element_type=_F32)
    u = jnp.dot(h, w1u_ref[...], preferred_element_type=_F32)
    act = (g * jax.nn.sigmoid(g) * u).astype(_BF16)
    acc_ref[...] += jnp.dot(act, w2_ref[...], preferred_element_type=_F32)

    @pl.when(j == pl.num_programs(1) - 1)
    def _():
        _on_part(r, n_first, (xa_ref, oa_ref), (xb_ref, ob_ref),
                 lambda x_ref, o_ref: _gated_residual_rows(x_ref, acc_ref, o_ref, rstd_ref, vec_ref, gpost_ref,
                                                           mod_ref, mod_base + 2, 0.5))


def _ffn(xs, mod, group_of_tile, gpre, gpost, w1, w2, l, i, mod_base, tiles, tf, n_prompt, split_out):
    d = xs[0].shape[1]
    n = sum(x.shape[0] for x in xs)
    split_in = len(xs) == 2
    tm = tiles[1] if (split_in or split_out) else tiles[0]
    n_first = n_prompt // tm
    nff = w2.shape[2] // tf

    def first(r, j):
        return jnp.minimum(r, n_first - 1), 0

    def second(r, j):
        return jnp.maximum(r - n_first, 0), 0

    def whole(r, j):
        return r, 0

    x_mode = dict(pipeline_mode=pl.Buffered(1)) if tm * d * 4 >= 8 * 1024 * 1024 else {}
    x_specs = ([pl.BlockSpec((tm, d), first, **x_mode), pl.BlockSpec((tm, d), second, **x_mode)] if split_in
               else [pl.BlockSpec((tm, d), whole, **x_mode)])
    if split_out:
        out_shape = (jax.ShapeDtypeStruct((n_first * tm, d), _F32), jax.ShapeDtypeStruct((n - n_first * tm, d), _F32))
        out_specs = (pl.BlockSpec((tm, d), first), pl.BlockSpec((tm, d), second))
    else:
        out_shape = jax.ShapeDtypeStruct((n, d), _F32)
        out_specs = pl.BlockSpec((tm, d), whole)
    return pl.pallas_call(
        functools.partial(_ffn_kernel, mod_base=mod_base, n_first=n_first, split_in=split_in, split_out=split_out),
        out_shape=out_shape,
        grid=(n // tm, nff),
        in_specs=x_specs + [
            pl.BlockSpec((None, N_MOD, d), lambda r, j: (group_of_tile(r, tm), 0, 0)),
            pl.BlockSpec((1, d), lambda r, j: (0, 0)),
            pl.BlockSpec((1, d), lambda r, j: (0, 0)),
            pl.BlockSpec((None, None, d, tf), lambda r, j: (l, i, 0, j)),
            pl.BlockSpec((None, None, d, tf), lambda r, j: (l, i, 0, j)),
            pl.BlockSpec((None, None, tf, d), lambda r, j: (l, i, j, 0)),
        ],
        out_specs=out_specs,
        scratch_shapes=[pltpu.VMEM((tm, d), _BF16), pltpu.VMEM((tm, d), _F32), pltpu.VMEM((tm, LANES), _F32),
                        pltpu.VMEM((2, NORM_ROWS, d), _F32)],
        compiler_params=_params("parallel", "arbitrary", vmem=VMEM_LIMIT_LARGE if x_mode else VMEM_LIMIT),
        name=f"ffn_l{l}_{i}",
    )(*xs, mod, gpre, gpost, *w1, w2)


def _inproj_kernel(x_ref, mod_ref, gpre_ref, w_ref, o_ref, h_ref, rstd_ref, vec_ref, *, mod_base):
    @pl.when(pl.program_id(1) == 0)
    def _():
        _modulated_norm_rows(x_ref, h_ref, rstd_ref, vec_ref, gpre_ref, mod_ref, mod_base)

    o_ref[...] = jnp.dot(h_ref[...], w_ref[...], preferred_element_type=_F32)


def _inproj(x, mod, group_of_tile, gpre, w_in, l, mod_base, tm, tn):
    n, d = x.shape
    cols = w_in.shape[2]
    return pl.pallas_call(
        functools.partial(_inproj_kernel, mod_base=mod_base),
        out_shape=jax.ShapeDtypeStruct((n, cols), _F32),
        grid=(n // tm, cols // tn),
        in_specs=[
            pl.BlockSpec((tm, d), lambda r, j: (r, 0)),
            pl.BlockSpec((None, N_MOD, d), lambda r, j: (group_of_tile(r, tm), 0, 0)),
            pl.BlockSpec((1, d), lambda r, j: (0, 0)),
            pl.BlockSpec((None, d, tn), lambda r, j: (l, 0, j)),
        ],
        out_specs=pl.BlockSpec((tm, tn), lambda r, j: (r, j)),
        scratch_shapes=[pltpu.VMEM((tm, d), _BF16), pltpu.VMEM((tm, LANES), _F32),
                        pltpu.VMEM((2, NORM_ROWS, d), _F32)],
        compiler_params=_params("parallel", "arbitrary"),
        name=f"inproj_l{l}",
    )(x, mod, gpre, w_in)


LRU_TILES = SUBLANES // 2
CONV_HALO = SUBLANES


def _sigmoid(x):
    return 0.5 * jnp.tanh(0.5 * x) + 0.5


def _lru_kernel(*refs, seq, row_chunk, unroll, aliased):
    if aliased:
        refs = refs[:8] + refs[9:]
    (xr_ref, gr_ref, cw_ref, cb_ref, wg_ref, bg_ref, lam_ref, h0_ref, rec_ref, st_ref,
     xp_ref, a_ref, b_ref, h_ref) = refs
    nt = LRU_TILES
    cg = nt * LANES
    zeros = jnp.zeros((CONV_HALO, cg), _F32)
    xp_ref[0:CONV_HALO, :] = zeros
    xp_ref[CONV_HALO + seq:2 * CONV_HALO + seq, :] = zeros
    xp_ref[CONV_HALO:CONV_HALO + seq, :] = xr_ref[...]

    neg_log_a_scale = (0.5 * LRU_C) * jax.nn.softplus(-lam_ref[...])
    exp2_scale = -math.log2(math.e) * neg_log_a_scale

    def coeffs(c, carry):
        r0 = c * row_chunk
        for n in range(nt):
            cs = slice(n * LANES, (n + 1) * LANES)
            y = cb_ref[:, cs]
            for tap in range(cw_ref.shape[0]):
                y = y + cw_ref[tap:tap + 1, cs] * xp_ref[pl.ds(r0 + CONV_HALO + tap - CONV_PAD_LEFT, row_chunk), cs]
            g4 = jnp.dot(y.astype(_BF16), wg_ref[n], preferred_element_type=_F32)
            half_y = 0.5 * y
            for direction in range(2):
                ua = 1.0 + jnp.tanh(g4[:, (2 * direction) * LANES:(2 * direction + 1) * LANES]
                                    + bg_ref[2 * direction:2 * direction + 1, cs])
                ux = 1.0 + jnp.tanh(g4[:, (2 * direction + 1) * LANES:(2 * direction + 2) * LANES]
                                    + bg_ref[2 * direction + 1:2 * direction + 2, cs])
                a = jnp.exp2(ua * exp2_scale[direction:direction + 1, cs])
                one_minus_a2 = jnp.tanh(ua * neg_log_a_scale[direction:direction + 1, cs]) * (1.0 + a * a)
                mult = jnp.where(one_minus_a2 > 0.0, one_minus_a2 * lax.rsqrt(one_minus_a2), 0.0)
                dense_rows = pl.ds(r0 * SUBLANES + nt * direction + n, row_chunk, stride=SUBLANES)
                a_ref[dense_rows, :] = a
                b_ref[dense_rows, :] = mult * ux * half_y
        return carry

    for c in range(seq // row_chunk):
        coeffs(c, 0)

    is_fwd = lax.broadcasted_iota(jnp.int32, (SUBLANES, LANES), 0) < nt

    def scan(c, h):
        for r in range(unroll):
            j = c * unroll + r
            jf = pl.multiple_of(j * SUBLANES, SUBLANES)
            jb = pl.multiple_of((seq - 1 - j) * SUBLANES, SUBLANES)
            a = jnp.where(is_fwd, a_ref[pl.ds(jf, SUBLANES), :], a_ref[pl.ds(jb, SUBLANES), :])
            b = jnp.where(is_fwd, b_ref[pl.ds(jf, SUBLANES), :], b_ref[pl.ds(jb, SUBLANES), :])
            h = a * h + b
            h_ref[pl.ds(jf, nt), :] = h[0:nt]
            h_ref[pl.ds(jb + nt, nt), :] = h[nt:2 * nt]
        return h

    h0 = jnp.concatenate([h0_ref[d:d + 1, n * LANES:(n + 1) * LANES] for d in range(2) for n in range(nt)], axis=0)
    h = lax.fori_loop(0, seq // unroll, scan, h0)
    for n in range(nt):
        st_ref[0:1, n * LANES:(n + 1) * LANES] = h[n:n + 1]
        st_ref[1:2, n * LANES:(n + 1) * LANES] = h[nt + n:nt + n + 1]

    def gate(c, carry):
        r0 = pl.multiple_of(c * row_chunk, row_chunk)
        for n in range(nt):
            cs = slice(n * LANES, (n + 1) * LANES)
            hf = h_ref[pl.ds(r0 * SUBLANES + n, row_chunk, stride=SUBLANES), :]
            hb = h_ref[pl.ds(r0 * SUBLANES + nt + n, row_chunk, stride=SUBLANES), :]
            rec_ref[pl.ds(r0, row_chunk), cs] = (
                (hf + hb) * jax.nn.gelu(gr_ref[pl.ds(r0, row_chunk), cs])).astype(rec_ref.dtype)
        return carry

    lax.fori_loop(0, seq // row_chunk, gate, 0)


def _lru(proj, rec, conv_w, conv_b, wg, bg, lam, h0, l, first_row, seq, d_rnn):
    n_seq = h0.shape[0]
    rb0 = first_row // seq
    cg = LRU_TILES * LANES
    ncg = d_rnn // cg
    row_chunk = min(seq, 256)
    aliased = rec is not None
    in_specs = [
        pl.BlockSpec((seq, cg), lambda s, c: (rb0 + s, c)),
        pl.BlockSpec((seq, cg), lambda s, c: (rb0 + s, ncg + c)),
        pl.BlockSpec((None, conv_w.shape[1], cg), lambda s, c: (l, 0, c)),
        pl.BlockSpec((None, 1, cg), lambda s, c: (l, 0, c)),
        pl.BlockSpec((None, LRU_TILES, LANES, 4 * LANES), lambda s, c: (l, c, 0, 0)),
        pl.BlockSpec((None, 4, cg), lambda s, c: (l, 0, c)),
        pl.BlockSpec((None, 2, cg), lambda s, c: (l, 0, c)),
        pl.BlockSpec((None, 2, cg), lambda s, c: (s, 0, c)),
    ]
    args = [proj, proj, conv_w, conv_b, wg, bg, lam, h0]
    if aliased:
        in_specs.append(pl.BlockSpec(memory_space=pl.ANY))
        args.append(rec)
    return pl.pallas_call(
        functools.partial(_lru_kernel, seq=seq, row_chunk=row_chunk, unroll=8, aliased=aliased),
        out_shape=(jax.ShapeDtypeStruct((proj.shape[0], d_rnn), _BF16),
                   jax.ShapeDtypeStruct((n_seq, 2, d_rnn), _F32)),
        grid=(n_seq, ncg),
        in_specs=in_specs,
        out_specs=(pl.BlockSpec((seq, cg), lambda s, c: (rb0 + s, c)),
                   pl.BlockSpec((None, 2, cg), lambda s, c: (s, 0, c))),
        scratch_shapes=[pltpu.VMEM((seq + 2 * CONV_HALO, cg), _F32)] + [pltpu.VMEM((seq * SUBLANES, LANES), _F32)] * 3,
        input_output_aliases={8: 0} if aliased else {},
        compiler_params=_params("parallel", "parallel", vmem=VMEM_LIMIT_LARGE),
        name=f"rglru_l{l}_t{seq}",
    )(*args)


Q_SCALE = QK_DIM ** -0.5 * math.log2(math.e)


def _rope(x, cos, sin_signed):
    lane = lax.broadcasted_iota(jnp.int32, x.shape, 1)
    half = QK_DIM // 4
    partner = jnp.where((lane & half) == 0, pltpu.roll(x, LANES - half, 1), pltpu.roll(x, half, 1))
    return x * cos + partner * sin_signed


def _lambda(lam_ref, lam_init):
    lv = lam_ref[...]
    return (jnp.exp(jnp.sum(lv[0:1] * lv[1:2], axis=-1, keepdims=True))
            - jnp.exp(jnp.sum(lv[2:3] * lv[3:4], axis=-1, keepdims=True)) + lam_init)


def _scores(q, kb):
    lane = lax.broadcasted_iota(jnp.int32, q.shape, 1)
    qm = jnp.concatenate([jnp.where(lane < QK_DIM, q, 0.0), jnp.where(lane >= QK_DIM, q, 0.0)], axis=0)
    return lax.dot_general(qm.astype(_BF16), kb, (((1,), (1,)), ((), ())), preferred_element_type=_F32)


def _softmax_diff_pv(s, vb1, lam, g, lam_init):
    tq = s.shape[0] // 2
    e = jnp.exp2(s - jnp.max(s, axis=-1, keepdims=True)).astype(_BF16)
    ov = jnp.dot(e, vb1, preferred_element_type=_F32)
    o = (ov[0:tq, 0:LANES] / ov[0:tq, LANES:2 * LANES]
         - lam * (ov[tq:2 * tq, 0:LANES] / ov[tq:2 * tq, LANES:2 * LANES]))
    return _rms(o, g) * (1.0 - lam_init)


def _attn_ctx_kernel(q_ref, k_ref, v_ref, ck_ref, cv_ref, cq_ref, sq_ref, ckk_ref, skk_ref, lam_ref, g_ref,
                     att_in_ref, o_ref, kbuf, vbuf, *, past, seq, sub, lam_init):
    del att_in_ref

    @pl.when(pl.program_id(2) == 0)
    def _():
        kbuf[0:past, :] = ck_ref[...].astype(kbuf.dtype)
        kbuf[past:past + seq, :] = _rope(k_ref[...], ckk_ref[...], skk_ref[...]).astype(kbuf.dtype)
        vbuf[0:past, 0:LANES] = cv_ref[...].astype(vbuf.dtype)
        vbuf[past:past + seq, 0:LANES] = v_ref[...].astype(vbuf.dtype)
        vbuf[:, LANES:2 * LANES] = jnp.ones((past + seq, LANES), vbuf.dtype)

    lam = _lambda(lam_ref, lam_init)

    def scores(i):
        rows = slice(i * sub, (i + 1) * sub)
        return _scores(_rope(q_ref[rows, :], cq_ref[rows, :], sq_ref[rows, :]) * Q_SCALE, kbuf[...])

    s = scores(0)
    for i in range(q_ref.shape[0] // sub):
        s_next = scores(i + 1) if (i + 1) * sub < q_ref.shape[0] else None
        o = _softmax_diff_pv(s, vbuf[...], lam, g_ref[...], lam_init)
        o_ref[i * sub:(i + 1) * sub, :] = o.astype(o_ref.dtype)
        s = s_next


def _attn_prompt_kernel(*refs, heads, lam_init, aliased):
    if aliased:
        refs = refs[:5] + refs[7:]
    q_ref, k_ref, v_ref, lam_ref, g_ref, o_ref, nk_ref, nv_ref = refs
    nk_ref[...] = k_ref[...]
    nv_ref[...] = v_ref[...]
    lam = _lambda(lam_ref, lam_init)
    ones = jnp.ones((k_ref.shape[0], LANES), _BF16)
    for h in range(heads):
        hs = slice(h * LANES, (h + 1) * LANES)
        vb1 = jnp.concatenate([v_ref[:, hs].astype(_BF16), ones], axis=1)
        s = _scores(q_ref[:, hs] * Q_SCALE, k_ref[:, hs].astype(_BF16))
        o = _softmax_diff_pv(s, vb1, lam, g_ref[...], lam_init)
        o_ref[:, hs] = o.astype(o_ref.dtype)


def _attention_prompt(proj, lamv, gsub, caches, l, depth, n_seq, seq, heads, col_q, lam_init):
    width = heads * LANES
    cq = col_q // width
    aliased = caches is not None
    in_specs = [
        pl.BlockSpec((seq, width), lambda b: (b, cq)),
        pl.BlockSpec((seq, width), lambda b: (b, cq + 1)),
        pl.BlockSpec((seq, width), lambda b: (b, cq + 2)),
        pl.BlockSpec((None, 4, QK_DIM), lambda b: (l, 0, 0)),
        pl.BlockSpec((None, 1, LANES), lambda b: (l, 0, 0)),
    ]
    args = [proj, proj, proj, lamv, gsub]
    if aliased:
        in_specs += [pl.BlockSpec(memory_space=pl.ANY)] * 2
        args += list(caches)
    cache_sds = jax.ShapeDtypeStruct((n_seq, depth, seq, width), _F32)
    return pl.pallas_call(
        functools.partial(_attn_prompt_kernel, heads=heads, lam_init=lam_init, aliased=aliased),
        out_shape=(jax.ShapeDtypeStruct((proj.shape[0], width), _BF16), cache_sds, cache_sds),
        grid=(n_seq,),
        in_specs=in_specs,
        out_specs=(pl.BlockSpec((seq, width), lambda b: (b, 0)),
                   pl.BlockSpec((None, None, seq, width), lambda b: (b, l, 0, 0)),
                   pl.BlockSpec((None, None, seq, width), lambda b: (b, l, 0, 0))),
        input_output_aliases={5: 1, 6: 2} if aliased else {},
        compiler_params=_params("parallel"),
        name=f"diffattn_prompt_l{l}",
    )(*args)


def _attention_ctx(proj, att, lamv, gsub, cache_k, cache_v, cos, sin, l, first_row, n_seq, seq, tq, heads,
                   col_q, lam_init):
    past = cache_k.shape[2]
    rq0, rk0 = first_row // tq, first_row // seq
    nq = seq // tq
    hq, hk, hv = col_q // LANES, col_q // LANES + heads, col_q // LANES + 2 * heads
    in_specs = [
        pl.BlockSpec((tq, LANES), lambda b, h, t: (rq0 + b * nq + t, hq + h)),
        pl.BlockSpec((seq, LANES), lambda b, h, t: (rk0 + b, hk + h)),
        pl.BlockSpec((seq, LANES), lambda b, h, t: (rk0 + b, hv + h)),
        pl.BlockSpec((None, None, past, LANES), lambda b, h, t: (b, l, 0, h)),
        pl.BlockSpec((None, None, past, LANES), lambda b, h, t: (b, l, 0, h)),
        pl.BlockSpec((tq, LANES), lambda b, h, t: (t, 0)),
        pl.BlockSpec((tq, LANES), lambda b, h, t: (t, 0)),
        pl.BlockSpec((seq, LANES), lambda b, h, t: (0, 0)),
        pl.BlockSpec((seq, LANES), lambda b, h, t: (0, 0)),
        pl.BlockSpec((None, 4, QK_DIM), lambda b, h, t: (l, 0, 0)),
        pl.BlockSpec((None, 1, LANES), lambda b, h, t: (l, 0, 0)),
        pl.BlockSpec(memory_space=pl.ANY),
    ]
    return pl.pallas_call(
        functools.partial(_attn_ctx_kernel, past=past, seq=seq, sub=min(tq, 256), lam_init=lam_init),
        out_shape=jax.ShapeDtypeStruct(att.shape, att.dtype),
        grid=(n_seq, heads, nq),
        in_specs=in_specs,
        out_specs=pl.BlockSpec((tq, LANES), lambda b, h, t: (rq0 + b * nq + t, h)),
        scratch_shapes=[pltpu.VMEM((past + seq, LANES), _BF16), pltpu.VMEM((past + seq, 2 * LANES), _BF16)],
        input_output_aliases={11: 0},
        compiler_params=_params("parallel", "parallel", "arbitrary"),
        name=f"diffattn_ctx_l{l}",
    )(proj, proj, proj, cache_k, cache_v, cos, sin, cos, sin, lamv, gsub, att)


def _merge_kernel(x_ref, mod_ref, gpost_ref, rec_ref, att_ref, gm1_ref, gm2_ref, plru_ref, pattn_ref,
                  wout_ref, o_ref, acc_ref, rstd_ref, vec_ref, *, mod_base):
    j = pl.program_id(1)

    @pl.when(j == 0)
    def _():
        acc_ref[...] = jnp.zeros_like(acc_ref)

    a = jnp.dot(rec_ref[...], plru_ref[...], preferred_element_type=_F32)
    b = jnp.dot(att_ref[...], pattn_ref[...], preferred_element_type=_F32)
    merged = _sigmoid(gm1_ref[...]) * a + _sigmoid(gm2_ref[...]) * b
    acc_ref[...] += jnp.dot(merged.astype(_BF16), wout_ref[...], preferred_element_type=_F32)

    @pl.when(j == pl.num_programs(1) - 1)
    def _():
        _gated_residual_rows(x_ref, acc_ref, o_ref, rstd_ref, vec_ref, gpost_ref, mod_ref, mod_base + 2, 1.0)


def _merge(x, mod, group_of_tile, gpost, rec, att, proj, col_gm, p_lru, p_attn, w_out, l, mod_base, tm, tn):
    n, d = x.shape
    d_rnn, v_cols = rec.shape[1], att.shape[1]
    g1, g2 = col_gm // tn, (col_gm + d) // tn
    return pl.pallas_call(
        functools.partial(_merge_kernel, mod_base=mod_base),
        out_shape=jax.ShapeDtypeStruct((n, d), _F32),
        grid=(n // tm, d // tn),
        in_specs=[
            pl.BlockSpec((tm, d), lambda r, j: (r, 0)),
            pl.BlockSpec((None, N_MOD, d), lambda r, j: (group_of_tile(r, tm), 0, 0)),
            pl.BlockSpec((1, d), lambda r, j: (0, 0)),
            pl.BlockSpec((tm, d_rnn), lambda r, j: (r, 0)),
            pl.BlockSpec((tm, v_cols), lambda r, j: (r, 0)),
            pl.BlockSpec((tm, tn), lambda r, j: (r, g1 + j)),
            pl.BlockSpec((tm, tn), lambda r, j: (r, g2 + j)),
            pl.BlockSpec((None, d_rnn, tn), lambda r, j: (l, 0, j)),
            pl.BlockSpec((None, v_cols, tn), lambda r, j: (l, 0, j)),
            pl.BlockSpec((None, tn, d), lambda r, j: (l, j, 0)),
        ],
        out_specs=pl.BlockSpec((tm, d), lambda r, j: (r, 0)),
        scratch_shapes=[pltpu.VMEM((tm, d), _F32), pltpu.VMEM((tm, LANES), _F32),
                        pltpu.VMEM((2, NORM_ROWS, d), _F32)],
        compiler_params=_params("parallel", "arbitrary"),
        name=f"merge_l{l}",
    )(x, mod, gpost, rec, att, proj, proj, p_lru, p_attn, w_out)


def _rope_tables(n_tokens):
    t = jnp.arange(n_tokens)
    row = (t // GRID_W).astype(_F32)
    col = (t % GRID_W).astype(_F32)
    n_freq = QK_DIM // 4
    freqs = 1.0 / (ROPE_BASE ** (jnp.arange(0, 2 * n_freq, 2, dtype=_F32) / (2 * n_freq)))
    ar, ac = row[:, None] * freqs, col[:, None] * freqs
    cos = jnp.concatenate([jnp.cos(ar), jnp.cos(ar), jnp.cos(ac), jnp.cos(ac)], axis=-1)
    sin = jnp.concatenate([-jnp.sin(ar), jnp.sin(ar), -jnp.sin(ac), jnp.sin(ac)], axis=-1)
    reps = LANES // QK_DIM
    return jnp.tile(cos, (1, reps)), jnp.tile(sin, (1, reps))


def _largest_tile(candidates, *extents):
    return next(t for t in candidates if all(e % t == 0 for e in extents))


def kernel(x_prompt, x_sample, c, cache_k, cache_v, state_lru, c_ctx, w_mod, b_mod, g_pre, g_post, ffn_w1, ffn_w2, w_in, conv_w, conv_b, lru_wa, lru_ba, lru_wx, lru_bx, lru_lambda, lam_q1, lam_k1, lam_q2, lam_k2, attn_subln, p_lru, p_attn, w_out):
    batch, seq, d = x_prompt.shape
    dec_batch, dec_seq, _ = x_sample.shape
    depth = w_mod.shape[0]
    d_rnn = conv_w.shape[2]
    heads = cache_k.shape[3]
    past = cache_k.shape[2]
    v_cols = heads * cache_v.shape[4]
    qk_cols = heads * cache_k.shape[4]
    d_ff = ffn_w2.shape[2]
    n_prompt, n_sample = batch * seq, dec_batch * dec_seq
    assert cache_k.shape[4] == LANES and cache_v.shape[4] == LANES and 1 + dec_batch <= MOD_ROWS
    col_q = 2 * d_rnn
    col_gm = col_q + 2 * qk_cols + v_cols

    tm = _largest_tile((512, 256), n_prompt, dec_seq)
    tm_in = _largest_tile((1024, 512, 256), n_prompt, dec_seq)
    tq = _largest_tile((1024, 512, 256, 128), dec_seq)
    tf, tn_in, tn_merge = 512, 1024, 512

    def group_of_tile(r, rows):
        first = n_prompt // rows
        return jnp.where(r < first, 0, 1 + (r - first) // (dec_seq // rows))

    ffp = -(-d_ff // tf) * tf
    padc = ((0, 0), (0, 0), (0, 0), (0, ffp - d_ff))
    w1 = (jnp.pad(ffn_w1[..., :d_ff], padc).astype(_BF16), jnp.pad(ffn_w1[..., d_ff:], padc).astype(_BF16))
    w2 = jnp.pad(ffn_w2, ((0, 0), (0, 0), (0, ffp - d_ff), (0, 0))).astype(_BF16)
    w_in_b, p_lru_b, p_attn_b, w_out_b = (w.astype(_BF16) for w in (w_in, p_lru, p_attn, w_out))
    wg = (0.5 * jnp.concatenate([lru_wa[:, 0], lru_wx[:, 0], lru_wa[:, 1], lru_wx[:, 1]], axis=-1)).astype(_BF16)
    bg = 0.5 * jnp.stack([lru_ba[:, 0], lru_bx[:, 0], lru_ba[:, 1], lru_bx[:, 1]], axis=1)
    lamv = jnp.stack([lam_q1, lam_k1, lam_q2, lam_k2], axis=1)
    gsub = attn_subln.reshape(depth, 1, -1)
    conv_b3 = conv_b.reshape(depth, 1, d_rnn)
    cache_k2 = cache_k.reshape(dec_batch, depth, past, qk_cols)
    cache_v2 = cache_v.reshape(dec_batch, depth, past, v_cols)
    cos, sin = _rope_tables(dec_seq)
    zero_state = jnp.zeros((batch, 2, d_rnn), _F32)

    cond = jnp.concatenate([c_ctx[None], c, jnp.zeros((MOD_ROWS - 1 - dec_batch, d), _F32)], axis=0)
    mod_all = _modulation(cond, w_mod, b_mod).reshape(depth, MOD_ROWS, N_MOD, d)

    xs = (x_prompt.reshape(n_prompt, d), x_sample.reshape(n_sample, d))
    caches, new_s = None, []
    for l in range(depth):
        mod = mod_all[l]
        gpre = [g_pre[l, i].reshape(1, d) for i in range(3)]
        gpost = [g_post[l, i].reshape(1, d) for i in range(3)]
        lam_init = 0.8 - 0.6 * math.exp(-0.3 * l)

        x = _ffn(xs, mod, group_of_tile, gpre[0], gpost[0], w1, w2, l, 0, 0, (tm_in, tm), tf, n_prompt, False)
        proj = _inproj(x, mod, group_of_tile, gpre[1], w_in_b, l, 3, tm_in, tn_in)
        rec, st_p = _lru(proj, None, conv_w, conv_b3, wg, bg, lru_lambda, zero_state, l, 0, seq, d_rnn)
        rec, _ = _lru(proj, rec, conv_w, conv_b3, wg, bg, lru_lambda, state_lru[:, l], l, n_prompt, dec_seq, d_rnn)
        att, *caches = _attention_prompt(proj, lamv, gsub, caches, l, depth, batch, seq, heads, col_q, lam_init)
        att = _attention_ctx(proj, att, lamv, gsub, cache_k2, cache_v2, cos, sin, l, n_prompt, dec_batch, dec_seq,
                             tq, heads, col_q, lam_init)
        x = _merge(x, mod, group_of_tile, gpost[1], rec, att, proj, col_gm, p_lru_b, p_attn_b, w_out_b,
                   l, 3, tm, tn_merge)
        last = l == depth - 1
        xs = _ffn((x,), mod, group_of_tile, gpre[2], gpost[2], w1, w2, l, 1, 6, (tm_in, tm), tf, n_prompt, last)
        xs = xs if last else (xs,)
        new_s.append(st_p)

    y_prompt = xs[0].reshape(batch, seq, d)
    y_sample = xs[1].reshape(dec_batch, dec_seq, d)
    new_k, new_v = (a.reshape(batch, depth, seq, heads, -1) for a in caches)
    return (y_prompt, y_sample, new_k, new_v, jnp.stack(new_s, axis=1))
```

```python
import functools
import math

import jax
import jax.numpy as jnp
from jax import lax
from jax.experimental import pallas as pl
from jax.experimental.pallas import tpu as pltpu

NORM_EPS = 1e-6
LRU_C = 8.0
N_MOD = 9
GRID_W = 64
ROPE_BASE = 10000.0
QK_DIM = 64
CONV_PAD_LEFT = 2

LANES = 128
SUBLANES = 8
MOD_ROWS = 16
VMEM_LIMIT = 52 * 1024 * 1024
VMEM_LIMIT_LARGE = 58 * 1024 * 1024

_BF16 = jnp.bfloat16
_F32 = jnp.float32


def _rms(x, g):
    ms = jnp.mean(x * x, axis=-1, keepdims=True)
    return x * lax.rsqrt(ms + NORM_EPS) * g


def _params(*sem, vmem=VMEM_LIMIT):
    return pltpu.CompilerParams(dimension_semantics=sem, vmem_limit_bytes=vmem)


def _mod_kernel(c_ref, w_ref, b_ref, o_ref):
    c = c_ref[...]
    s = (c * jax.nn.sigmoid(c)).astype(_BF16)
    o_ref[...] = jnp.dot(s, w_ref[...].astype(_BF16), preferred_element_type=_F32) + b_ref[...]


def _modulation(cond, w_mod, b_mod, tn=1024):
    depth, d, n = w_mod.shape
    return pl.pallas_call(
        _mod_kernel,
        out_shape=jax.ShapeDtypeStruct((depth, MOD_ROWS, n), _F32),
        grid=(depth, n // tn),
        in_specs=[
            pl.BlockSpec((MOD_ROWS, d), lambda l, j: (0, 0)),
            pl.BlockSpec((None, d, tn), lambda l, j: (l, 0, j)),
            pl.BlockSpec((None, 1, tn), lambda l, j: (l, 0, j)),
        ],
        out_specs=pl.BlockSpec((None, MOD_ROWS, tn), lambda l, j: (l, 0, j)),
        compiler_params=_params("parallel", "parallel"),
        name="modulation",
    )(cond, w_mod, b_mod.reshape(depth, 1, n))


NORM_ROWS = 16


def _for_row_chunks(n_rows, fn):
    def body(i, carry):
        fn(pl.ds(pl.multiple_of(i * NORM_ROWS, NORM_ROWS), NORM_ROWS))
        return carry

    lax.fori_loop(0, n_rows // NORM_ROWS, body, 0, unroll=2)


def _row_rstd(x_ref, rstd_ref):
    n_tiles = x_ref.shape[1] // LANES

    def lane_partial_sums(rows):
        sq = jnp.square(x_ref[rows, :])
        part = sq[:, 0:LANES]
        for t in range(1, n_tiles):
            part = part + sq[:, t * LANES:(t + 1) * LANES]
        rstd_ref[rows, :] = part

    _for_row_chunks(x_ref.shape[0], lane_partial_sums)
    ms = jnp.sum(rstd_ref[...], axis=-1, keepdims=True) * (1.0 / x_ref.shape[1])
    rstd_ref[...] = jnp.broadcast_to(lax.rsqrt(ms + NORM_EPS), rstd_ref.shape)


def _modulated_norm_rows(x_ref, h_ref, rstd_ref, vec_ref, g_ref, mod_ref, mod_base):
    _row_rstd(x_ref, rstd_ref)
    n_tiles = x_ref.shape[1] // LANES
    sh = mod_ref[mod_base:mod_base + 1, :]
    sc = mod_ref[mod_base + 1:mod_base + 2, :]
    vec_ref[0] = jnp.broadcast_to(g_ref[...] * (1.0 + sc), vec_ref.shape[1:])
    vec_ref[1] = jnp.broadcast_to(sh, vec_ref.shape[1:])

    def normalise(rows):
        rstd = jnp.tile(rstd_ref[rows, :], (1, n_tiles))
        h_ref[rows, :] = (x_ref[rows, :] * rstd * vec_ref[0] + vec_ref[1]).astype(h_ref.dtype)

    _for_row_chunks(x_ref.shape[0], normalise)


def _gated_residual_rows(x_ref, acc_ref, o_ref, rstd_ref, vec_ref, g_ref, mod_ref, gate_row, gate_scale):
    _row_rstd(acc_ref, rstd_ref)
    n_tiles = x_ref.shape[1] // LANES
    gt = mod_ref[gate_row:gate_row + 1, :]
    if gate_scale != 1.0:
        gt = gate_scale * gt
    vec_ref[0] = jnp.broadcast_to(gt * g_ref[...], vec_ref.shape[1:])

    def residual(rows):
        rstd = jnp.tile(rstd_ref[rows, :], (1, n_tiles))
        o_ref[rows, :] = x_ref[rows, :] + acc_ref[rows, :] * rstd * vec_ref[0]

    _for_row_chunks(x_ref.shape[0], residual)


def _on_part(r, n_first, refs_first, refs_second, fn):
    if all(a is b for a, b in zip(refs_first, refs_second)):
        fn(*refs_first)
        return
    pl.when(r < n_first)(lambda: fn(*refs_first))
    pl.when(r >= n_first)(lambda: fn(*refs_second))


def _ffn_kernel(*refs, mod_base, n_first, split_in, split_out):
    refs = list(refs)
    xa_ref = refs.pop(0)
    xb_ref = refs.pop(0) if split_in else xa_ref
    mod_ref, gpre_ref, gpost_ref, w1g_ref, w1u_ref, w2_ref, oa_ref = refs[:7]
    ob_ref = refs[7] if split_out else oa_ref
    h_ref, acc_ref, rstd_ref, vec_ref = refs[-4:]
    r, j = pl.program_id(0), pl.program_id(1)

    @pl.when(j == 0)
    def _():
        _on_part(r, n_first, (xa_ref,), (xb_ref,),
                 lambda x_ref: _modulated_norm_rows(x_ref, h_ref, rstd_ref, vec_ref, gpre_ref, mod_ref,
                                                    mod_base))
        acc_ref[...] = jnp.zeros_like(acc_ref)

    h = h_ref[...]
    g = jnp.dot(h, w1g_ref[...], preferred_element_type=_F32)
    u = jnp.dot(h, w1u_ref[...], preferred_element_type=_F32)
    act = (g * jax.nn.sigmoid(g) * u).astype(_BF16)
    acc_ref[...] += jnp.dot(act, w2_ref[...], preferred_element_type=_F32)

    @pl.when(j == pl.num_programs(1) - 1)
    def _():
        _on_part(r, n_first, (xa_ref, oa_ref), (xb_ref, ob_ref),
                 lambda x_ref, o_ref: _gated_residual_rows(x_ref, acc_ref, o_ref, rstd_ref, vec_ref, gpost_ref,
                                                           mod_ref, mod_base + 2, 0.5))


def _ffn(xs, mod, group_of_tile, gpre, gpost, w1, w2, l, i, mod_base, tiles, tf, n_prompt, split_out):
    d = xs[0].shape[1]
    n = sum(x.shape[0] for x in xs)
    split_in = len(xs) == 2
    tm = tiles[1] if (split_in or split_out) else tiles[0]
    n_first = n_prompt // tm
    nff = w2.shape[2] // tf

    def first(r, j):
        return jnp.minimum(r, n_first - 1), 0

    def second(r, j):
        return jnp.maximum(r - n_first, 0), 0

    def whole(r, j):
        return r, 0

    x_mode = dict(pipeline_mode=pl.Buffered(1)) if tm * d * 4 >= 8 * 1024 * 1024 else {}
    x_specs = ([pl.BlockSpec((tm, d), first, **x_mode), pl.BlockSpec((tm, d), second, **x_mode)] if split_in
               else [pl.BlockSpec((tm, d), whole, **x_mode)])
    if split_out:
        out_shape = (jax.ShapeDtypeStruct((n_first * tm, d), _F32), jax.ShapeDtypeStruct((n - n_first * tm, d), _F32))
        out_specs = (pl.BlockSpec((tm, d), first), pl.BlockSpec((tm, d), second))
    else:
        out_shape = jax.ShapeDtypeStruct((n, d), _F32)
        out_specs = pl.BlockSpec((tm, d), whole)
    return pl.pallas_call(
        functools.partial(_ffn_kernel, mod_base=mod_base, n_first=n_first, split_in=split_in, split_out=split_out),
        out_shape=out_shape,
        grid=(n // tm, nff),
        in_specs=x_specs + [
            pl.BlockSpec((None, N_MOD, d), lambda r, j: (group_of_tile(r, tm), 0, 0)),
            pl.BlockSpec((1, d), lambda r, j: (0, 0)),
            pl.BlockSpec((1, d), lambda r, j: (0, 0)),
            pl.BlockSpec((None, None, d, tf), lambda r, j: (l, i, 0, j)),
            pl.BlockSpec((None, None, d, tf), lambda r, j: (l, i, 0, j)),
            pl.BlockSpec((None, None, tf, d), lambda r, j: (l, i, j, 0)),
        ],
        out_specs=out_specs,
        scratch_shapes=[pltpu.VMEM((tm, d), _BF16), pltpu.VMEM((tm, d), _F32), pltpu.VMEM((tm, LANES), _F32),
                        pltpu.VMEM((2, NORM_ROWS, d), _F32)],
        compiler_params=_params("parallel", "arbitrary", vmem=VMEM_LIMIT_LARGE if x_mode else VMEM_LIMIT),
        name=f"ffn_l{l}_{i}",
    )(*xs, mod, gpre, gpost, *w1, w2)


def _inproj_kernel(x_ref, mod_ref, gpre_ref, w_ref, o_ref, h_ref, rstd_ref, vec_ref, *, mod_base):
    @pl.when(pl.program_id(1) == 0)
    def _():
        _modulated_norm_rows(x_ref, h_ref, rstd_ref, vec_ref, gpre_ref, mod_ref, mod_base)

    o_ref[...] = jnp.dot(h_ref[...], w_ref[...], preferred_element_type=_F32)


def _inproj(x, mod, group_of_tile, gpre, w_in, l, mod_base, tm, tn):
    n, d = x.shape
    cols = w_in.shape[2]
    return pl.pallas_call(
        functools.partial(_inproj_kernel, mod_base=mod_base),
        out_shape=jax.ShapeDtypeStruct((n, cols), _F32),
        grid=(n // tm, cols // tn),
        in_specs=[
            pl.BlockSpec((tm, d), lambda r, j: (r, 0)),
            pl.BlockSpec((None, N_MOD, d), lambda r, j: (group_of_tile(r, tm), 0, 0)),
            pl.BlockSpec((1, d), lambda r, j: (0, 0)),
            pl.BlockSpec((None, d, tn), lambda r, j: (l, 0, j)),
        ],
        out_specs=pl.BlockSpec((tm, tn), lambda r, j: (r, j)),
        scratch_shapes=[pltpu.VMEM((tm, d), _BF16), pltpu.VMEM((tm, LANES), _F32),
                        pltpu.VMEM((2, NORM_ROWS, d), _F32)],
        compiler_params=_params("parallel", "arbitrary"),
        name=f"inproj_l{l}",
    )(x, mod, gpre, w_in)


LRU_TILES = SUBLANES // 2
CONV_HALO = SUBLANES
XP_PITCH = 2


def _sigmoid(x):
    return 0.5 * jnp.tanh(0.5 * x) + 0.5


def _lru_kernel(*refs, seq, row_chunk, unroll, aliased):
    if aliased:
        refs = refs[:8] + refs[9:]
    (xr_ref, gr_ref, cw_ref, cb_ref, wg_ref, bg_ref, lam_ref, h0_ref, rec_ref, st_ref,
     xp_ref, a_ref, b_ref, h_ref) = refs
    nt = LRU_TILES
    halo = jnp.zeros((XP_PITCH * CONV_HALO, LANES), _F32)
    for n in range(nt):
        pair, member = divmod(n, XP_PITCH)
        xp_ref[pair, 0:XP_PITCH * CONV_HALO, :] = halo
        xp_ref[pair, XP_PITCH * (CONV_HALO + seq):XP_PITCH * (2 * CONV_HALO + seq), :] = halo
        xp_ref[pair, pl.ds(XP_PITCH * CONV_HALO + member, seq, stride=XP_PITCH), :] = (
            xr_ref[:, n * LANES:(n + 1) * LANES])

    neg_log_a_scale = (0.5 * LRU_C) * jax.nn.softplus(-lam_ref[...])
    exp2_scale = -math.log2(math.e) * neg_log_a_scale

    def coeffs(c, carry):
        r0 = c * row_chunk
        for n in range(nt):
            cs = slice(n * LANES, (n + 1) * LANES)
            pair, member = divmod(n, XP_PITCH)
            y = cb_ref[:, cs]
            for tap in range(cw_ref.shape[0]):
                first = XP_PITCH * (r0 + CONV_HALO + tap - CONV_PAD_LEFT) + member
                y = y + cw_ref[tap:tap + 1, cs] * xp_ref[pair, pl.ds(first, row_chunk, stride=XP_PITCH), :]
            g4 = jnp.dot(y.astype(_BF16), wg_ref[n], preferred_element_type=_F32)
            half_y = 0.5 * y
            for direction in range(2):
                ua = 1.0 + jnp.tanh(g4[:, (2 * direction) * LANES:(2 * direction + 1) * LANES]
                                    + bg_ref[2 * direction:2 * direction + 1, cs])
                ux = 1.0 + jnp.tanh(g4[:, (2 * direction + 1) * LANES:(2 * direction + 2) * LANES]
                                    + bg_ref[2 * direction + 1:2 * direction + 2, cs])
                a = jnp.exp2(ua * exp2_scale[direction:direction + 1, cs])
                one_minus_a2 = jnp.tanh(ua * neg_log_a_scale[direction:direction + 1, cs]) * (1.0 + a * a)
                mult = jnp.where(one_minus_a2 > 0.0, one_minus_a2 * lax.rsqrt(one_minus_a2), 0.0)
                dense_rows = pl.ds(r0 * SUBLANES + nt * direction + n, row_chunk, stride=SUBLANES)
                a_ref[dense_rows, :] = a
                b_ref[dense_rows, :] = mult * ux * half_y
        return carry

    for c in range(seq // row_chunk):
        coeffs(c, 0)

    is_fwd = lax.broadcasted_iota(jnp.int32, (SUBLANES, LANES), 0) < nt

    block = unroll * SUBLANES

    def scan(c, h):
        rf = pl.multiple_of(c * block, block)
        rb = pl.multiple_of((seq - unroll) * SUBLANES - c * block, block)
        a_f, a_b = a_ref[pl.ds(rf, block), :], a_ref[pl.ds(rb, block), :]
        b_f, b_b = b_ref[pl.ds(rf, block), :], b_ref[pl.ds(rb, block), :]
        def coeff(r):
            up = slice(r * SUBLANES, (r + 1) * SUBLANES)
            down = slice((unroll - 1 - r) * SUBLANES, (unroll - r) * SUBLANES)
            return jnp.where(is_fwd, a_f[up], a_b[down]), jnp.where(is_fwd, b_f[up], b_b[down])

        def emit(r, h_r):
            h_ref[pl.ds(rf + r * SUBLANES, nt), :] = h_r[0:nt]
            h_ref[pl.ds(rb + (unroll - 1 - r) * SUBLANES + nt, nt), :] = h_r[nt:2 * nt]

        for r in range(0, unroll, 2):
            (a0, b0), (a1, b1) = coeff(r), coeff(r + 1)
            emit(r, a0 * h + b0)
            h = (a1 * a0) * h + (a1 * b0 + b1)
            emit(r + 1, h)
        return h

    h0 = jnp.concatenate([h0_ref[d:d + 1, n * LANES:(n + 1) * LANES] for d in range(2) for n in range(nt)], axis=0)
    h = lax.fori_loop(0, seq // unroll, scan, h0)
    for n in range(nt):
        st_ref[0:1, n * LANES:(n + 1) * LANES] = h[n:n + 1]
        st_ref[1:2, n * LANES:(n + 1) * LANES] = h[nt + n:nt + n + 1]

    def gate(c, carry):
        r0 = pl.multiple_of(c * row_chunk, row_chunk)
        for n in range(nt):
            cs = slice(n * LANES, (n + 1) * LANES)
            hf = h_ref[pl.ds(r0 * SUBLANES + n, row_chunk, stride=SUBLANES), :]
            hb = h_ref[pl.ds(r0 * SUBLANES + nt + n, row_chunk, stride=SUBLANES), :]
            rec_ref[pl.ds(r0, row_chunk), cs] = (
                (hf + hb) * jax.nn.gelu(gr_ref[pl.ds(r0, row_chunk), cs])).astype(rec_ref.dtype)
        return carry

    lax.fori_loop(0, seq // row_chunk, gate, 0)


def _lru(proj, rec, conv_w, conv_b, wg, bg, lam, h0, l, first_row, seq, d_rnn):
    n_seq = h0.shape[0]
    rb0 = first_row // seq
    cg = LRU_TILES * LANES
    ncg = d_rnn // cg
    row_chunk = min(seq, 256)
    aliased = rec is not None
    in_specs = [
        pl.BlockSpec((seq, cg), lambda s, c: (rb0 + s, c)),
        pl.BlockSpec((seq, cg), lambda s, c: (rb0 + s, ncg + c)),
        pl.BlockSpec((None, conv_w.shape[1], cg), lambda s, c: (l, 0, c)),
        pl.BlockSpec((None, 1, cg), lambda s, c: (l, 0, c)),
        pl.BlockSpec((None, LRU_TILES, LANES, 4 * LANES), lambda s, c: (l, c, 0, 0)),
        pl.BlockSpec((None, 4, cg), lambda s, c: (l, 0, c)),
        pl.BlockSpec((None, 2, cg), lambda s, c: (l, 0, c)),
        pl.BlockSpec((None, 2, cg), lambda s, c: (s, 0, c)),
    ]
    args = [proj, proj, conv_w, conv_b, wg, bg, lam, h0]
    if aliased:
        in_specs.append(pl.BlockSpec(memory_space=pl.ANY))
        args.append(rec)
    return pl.pallas_call(
        functools.partial(_lru_kernel, seq=seq, row_chunk=row_chunk, unroll=8, aliased=aliased),
        out_shape=(jax.ShapeDtypeStruct((proj.shape[0], d_rnn), _BF16),
                   jax.ShapeDtypeStruct((n_seq, 2, d_rnn), _F32)),
        grid=(n_seq, ncg),
        in_specs=in_specs,
        out_specs=(pl.BlockSpec((seq, cg), lambda s, c: (rb0 + s, c)),
                   pl.BlockSpec((None, 2, cg), lambda s, c: (s, 0, c))),
        scratch_shapes=[pltpu.VMEM((LRU_TILES // XP_PITCH, XP_PITCH * (seq + 2 * CONV_HALO), LANES), _F32)]
        + [pltpu.VMEM((seq * SUBLANES, LANES), _F32)] * 3,
        input_output_aliases={8: 0} if aliased else {},
        compiler_params=_params("parallel", "parallel", vmem=VMEM_LIMIT_LARGE),
        name=f"rglru_l{l}_t{seq}",
    )(*args)


Q_SCALE = QK_DIM ** -0.5 * math.log2(math.e)


def _rope(x, cos, sin_signed):
    lane = lax.broadcasted_iota(jnp.int32, x.shape, 1)
    half = QK_DIM // 4
    partner = jnp.where((lane & half) == 0, pltpu.roll(x, LANES - half, 1), pltpu.roll(x, half, 1))
    return x * cos + partner * sin_signed


def _lambda(lam_ref, lam_init):
    lv = lam_ref[...]
    return (jnp.exp(jnp.sum(lv[0:1] * lv[1:2], axis=-1, keepdims=True))
            - jnp.exp(jnp.sum(lv[2:3] * lv[3:4], axis=-1, keepdims=True)) + lam_init)


def _scores(q, kb):
    lane = lax.broadcasted_iota(jnp.int32, q.shape, 1)
    qm = jnp.concatenate([jnp.where(lane < QK_DIM, q, 0.0), jnp.where(lane >= QK_DIM, q, 0.0)], axis=0)
    return lax.dot_general(qm.astype(_BF16), kb, (((1,), (1,)), ((), ())), preferred_element_type=_F32)


def _softmax_diff_pv(s, vb1, lam, g, lam_init):
    tq = s.shape[0] // 2
    e = jnp.exp2(s - jnp.max(s, axis=-1, keepdims=True)).astype(_BF16)
    ov = jnp.dot(e, vb1, preferred_element_type=_F32)
    o = (ov[0:tq, 0:LANES] / ov[0:tq, LANES:2 * LANES]
         - lam * (ov[tq:2 * tq, 0:LANES] / ov[tq:2 * tq, LANES:2 * LANES]))
    return _rms(o, g) * (1.0 - lam_init)


def _attn_ctx_kernel(q_ref, k_ref, v_ref, ck_ref, cv_ref, cq_ref, sq_ref, ckk_ref, skk_ref, lam_ref, g_ref,
                     att_in_ref, o_ref, kbuf, vbuf, *, past, seq, sub, lam_init):
    del att_in_ref

    @pl.when(pl.program_id(2) == 0)
    def _():
        kbuf[0:past, :] = ck_ref[...].astype(kbuf.dtype)
        kbuf[past:past + seq, :] = _rope(k_ref[...], ckk_ref[...], skk_ref[...]).astype(kbuf.dtype)
        vbuf[0:past, 0:LANES] = cv_ref[...].astype(vbuf.dtype)
        vbuf[past:past + seq, 0:LANES] = v_ref[...].astype(vbuf.dtype)
        vbuf[:, LANES:2 * LANES] = jnp.ones((past + seq, LANES), vbuf.dtype)

    lam = _lambda(lam_ref, lam_init)

    def scores(i):
        rows = slice(i * sub, (i + 1) * sub)
        return _scores(_rope(q_ref[rows, :], cq_ref[rows, :], sq_ref[rows, :]) * Q_SCALE, kbuf[...])

    s = scores(0)
    for i in range(q_ref.shape[0] // sub):
        s_next = scores(i + 1) if (i + 1) * sub < q_ref.shape[0] else None
        o = _softmax_diff_pv(s, vbuf[...], lam, g_ref[...], lam_init)
        o_ref[i * sub:(i + 1) * sub, :] = o.astype(o_ref.dtype)
        s = s_next


def _attn_prompt_kernel(*refs, heads, lam_init, aliased):
    if aliased:
        refs = refs[:5] + refs[7:]
    q_ref, k_ref, v_ref, lam_ref, g_ref, o_ref, nk_ref, nv_ref = refs
    nk_ref[...] = k_ref[...]
    nv_ref[...] = v_ref[...]
    lam = _lambda(lam_ref, lam_init)
    ones = jnp.ones((k_ref.shape[0], LANES), _BF16)
    for h in range(heads):
        hs = slice(h * LANES, (h + 1) * LANES)
        vb1 = jnp.concatenate([v_ref[:, hs].astype(_BF16), ones], axis=1)
        s = _scores(q_ref[:, hs] * Q_SCALE, k_ref[:, hs].astype(_BF16))
        o = _softmax_diff_pv(s, vb1, lam, g_ref[...], lam_init)
        o_ref[:, hs] = o.astype(o_ref.dtype)


def _attention_prompt(proj, lamv, gsub, caches, l, depth, n_seq, seq, heads, col_q, lam_init):
    width = heads * LANES
    cq = col_q // width
    aliased = caches is not None
    in_specs = [
        pl.BlockSpec((seq, width), lambda b: (b, cq)),
        pl.BlockSpec((seq, width), lambda b: (b, cq + 1)),
        pl.BlockSpec((seq, width), lambda b: (b, cq + 2)),
        pl.BlockSpec((None, 4, QK_DIM), lambda b: (l, 0, 0)),
        pl.BlockSpec((None, 1, LANES), lambda b: (l, 0, 0)),
    ]
    args = [proj, proj, proj, lamv, gsub]
    if aliased:
        in_specs += [pl.BlockSpec(memory_space=pl.ANY)] * 2
        args += list(caches)
    cache_sds = jax.ShapeDtypeStruct((n_seq, depth, seq, width), _F32)
    return pl.pallas_call(
        functools.partial(_attn_prompt_kernel, heads=heads, lam_init=lam_init, aliased=aliased),
        out_shape=(jax.ShapeDtypeStruct((proj.shape[0], width), _BF16), cache_sds, cache_sds),
        grid=(n_seq,),
        in_specs=in_specs,
        out_specs=(pl.BlockSpec((seq, width), lambda b: (b, 0)),
                   pl.BlockSpec((None, None, seq, width), lambda b: (b, l, 0, 0)),
                   pl.BlockSpec((None, None, seq, width), lambda b: (b, l, 0, 0))),
        input_output_aliases={5: 1, 6: 2} if aliased else {},
        compiler_params=_params("parallel"),
        name=f"diffattn_prompt_l{l}",
    )(*args)


def _attention_ctx(proj, att, lamv, gsub, cache_k, cache_v, cos, sin, l, first_row, n_seq, seq, tq, heads,
                   col_q, lam_init):
    past = cache_k.shape[2]
    rq0, rk0 = first_row // tq, first_row // seq
    nq = seq // tq
    hq, hk, hv = col_q // LANES, col_q // LANES + heads, col_q // LANES + 2 * heads
    in_specs = [
        pl.BlockSpec((tq, LANES), lambda b, h, t: (rq0 + b * nq + t, hq + h)),
        pl.BlockSpec((seq, LANES), lambda b, h, t: (rk0 + b, hk + h)),
        pl.BlockSpec((seq, LANES), lambda b, h, t: (rk0 + b, hv + h)),
        pl.BlockSpec((None, None, past, LANES), lambda b, h, t: (b, l, 0, h)),
        pl.BlockSpec((None, None, past, LANES), lambda b, h, t: (b, l, 0, h)),
        pl.BlockSpec((tq, LANES), lambda b, h, t: (t, 0)),
        pl.BlockSpec((tq, LANES), lambda b, h, t: (t, 0)),
        pl.BlockSpec((seq, LANES), lambda b, h, t: (0, 0)),
        pl.BlockSpec((seq, LANES), lambda b, h, t: (0, 0)),
        pl.BlockSpec((None, 4, QK_DIM), lambda b, h, t: (l, 0, 0)),
        pl.BlockSpec((None, 1, LANES), lambda b, h, t: (l, 0, 0)),
        pl.BlockSpec(memory_space=pl.ANY),
    ]
    return pl.pallas_call(
        functools.partial(_attn_ctx_kernel, past=past, seq=seq, sub=min(tq, 256), lam_init=lam_init),
        out_shape=jax.ShapeDtypeStruct(att.shape, att.dtype),
        grid=(n_seq, heads, nq),
        in_specs=in_specs,
        out_specs=pl.BlockSpec((tq, LANES), lambda b, h, t: (rq0 + b * nq + t, h)),
        scratch_shapes=[pltpu.VMEM((past + seq, LANES), _BF16), pltpu.VMEM((past + seq, 2 * LANES), _BF16)],
        input_output_aliases={11: 0},
        compiler_params=_params("parallel", "parallel", "arbitrary"),
        name=f"diffattn_ctx_l{l}",
    )(proj, proj, proj, cache_k, cache_v, cos, sin, cos, sin, lamv, gsub, att)


def _merge_kernel(x_ref, mod_ref, gpost_ref, rec_ref, att_ref, gm1_ref, gm2_ref, plru_ref, pattn_ref,
                  wout_ref, o_ref, acc_ref, rstd_ref, vec_ref, *, mod_base):
    j = pl.program_id(1)

    @pl.when(j == 0)
    def _():
        acc_ref[...] = jnp.zeros_like(acc_ref)

    a = jnp.dot(rec_ref[...], plru_ref[...], preferred_element_type=_F32)
    b = jnp.dot(att_ref[...], pattn_ref[...], preferred_element_type=_F32)
    merged = _sigmoid(gm1_ref[...]) * a + _sigmoid(gm2_ref[...]) * b
    acc_ref[...] += jnp.dot(merged.astype(_BF16), wout_ref[...], preferred_element_type=_F32)

    @pl.when(j == pl.num_programs(1) - 1)
    def _():
        _gated_residual_rows(x_ref, acc_ref, o_ref, rstd_ref, vec_ref, gpost_ref, mod_ref, mod_base + 2, 1.0)


def _merge(x, mod, group_of_tile, gpost, rec, att, proj, col_gm, p_lru, p_attn, w_out, l, mod_base, tm, tn):
    n, d = x.shape
    d_rnn, v_cols = rec.shape[1], att.shape[1]
    g1, g2 = col_gm // tn, (col_gm + d) // tn
    return pl.pallas_call(
        functools.partial(_merge_kernel, mod_base=mod_base),
        out_shape=jax.ShapeDtypeStruct((n, d), _F32),
        grid=(n // tm, d // tn),
        in_specs=[
            pl.BlockSpec((tm, d), lambda r, j: (r, 0)),
            pl.BlockSpec((None, N_MOD, d), lambda r, j: (group_of_tile(r, tm), 0, 0)),
            pl.BlockSpec((1, d), lambda r, j: (0, 0)),
            pl.BlockSpec((tm, d_rnn), lambda r, j: (r, 0)),
            pl.BlockSpec((tm, v_cols), lambda r, j: (r, 0)),
            pl.BlockSpec((tm, tn), lambda r, j: (r, g1 + j)),
            pl.BlockSpec((tm, tn), lambda r, j: (r, g2 + j)),
            pl.BlockSpec((None, d_rnn, tn), lambda r, j: (l, 0, j)),
            pl.BlockSpec((None, v_cols, tn), lambda r, j: (l, 0, j)),
            pl.BlockSpec((None, tn, d), lambda r, j: (l, j, 0)),
        ],
        out_specs=pl.BlockSpec((tm, d), lambda r, j: (r, 0)),
        scratch_shapes=[pltpu.VMEM((tm, d), _F32), pltpu.VMEM((tm, LANES), _F32),
                        pltpu.VMEM((2, NORM_ROWS, d), _F32)],
        compiler_params=_params("parallel", "arbitrary"),
        name=f"merge_l{l}",
    )(x, mod, gpost, rec, att, proj, proj, p_lru, p_attn, w_out)


def _rope_tables(n_tokens):
    t = jnp.arange(n_tokens)
    row = (t // GRID_W).astype(_F32)
    col = (t % GRID_W).astype(_F32)
    n_freq = QK_DIM // 4
    freqs = 1.0 / (ROPE_BASE ** (jnp.arange(0, 2 * n_freq, 2, dtype=_F32) / (2 * n_freq)))
    ar, ac = row[:, None] * freqs, col[:, None] * freqs
    cos = jnp.concatenate([jnp.cos(ar), jnp.cos(ar), jnp.cos(ac), jnp.cos(ac)], axis=-1)
    sin = jnp.concatenate([-jnp.sin(ar), jnp.sin(ar), -jnp.sin(ac), jnp.sin(ac)], axis=-1)
    reps = LANES // QK_DIM
    return jnp.tile(cos, (1, reps)), jnp.tile(sin, (1, reps))


def _largest_tile(candidates, *extents):
    return next(t for t in candidates if all(e % t == 0 for e in extents))


def kernel(x_prompt, x_sample, c, cache_k, cache_v, state_lru, c_ctx, w_mod, b_mod, g_pre, g_post, ffn_w1, ffn_w2, w_in, conv_w, conv_b, lru_wa, lru_ba, lru_wx, lru_bx, lru_lambda, lam_q1, lam_k1, lam_q2, lam_k2, attn_subln, p_lru, p_attn, w_out):
    batch, seq, d = x_prompt.shape
    dec_batch, dec_seq, _ = x_sample.shape
    depth = w_mod.shape[0]
    d_rnn = conv_w.shape[2]
    heads = cache_k.shape[3]
    past = cache_k.shape[2]
    v_cols = heads * cache_v.shape[4]
    qk_cols = heads * cache_k.shape[4]
    d_ff = ffn_w2.shape[2]
    n_prompt, n_sample = batch * seq, dec_batch * dec_seq
    assert cache_k.shape[4] == LANES and cache_v.shape[4] == LANES and 1 + dec_batch <= MOD_ROWS
    col_q = 2 * d_rnn
    col_gm = col_q + 2 * qk_cols + v_cols

    tm = _largest_tile((512, 256), n_prompt, dec_seq)
    tm_in = _largest_tile((1024, 512, 256), n_prompt, dec_seq)
    tq = _largest_tile((1024, 512, 256, 128), dec_seq)
    tf, tn_in, tn_merge = 512, 1024, 512

    def group_of_tile(r, rows):
        first = n_prompt // rows
        return jnp.where(r < first, 0, 1 + (r - first) // (dec_seq // rows))

    ffp = -(-d_ff // tf) * tf
    padc = ((0, 0), (0, 0), (0, 0), (0, ffp - d_ff))
    w1 = (jnp.pad(ffn_w1[..., :d_ff], padc).astype(_BF16), jnp.pad(ffn_w1[..., d_ff:], padc).astype(_BF16))
    w2 = jnp.pad(ffn_w2, ((0, 0), (0, 0), (0, ffp - d_ff), (0, 0))).astype(_BF16)
    w_in_b, p_lru_b, p_attn_b, w_out_b = (w.astype(_BF16) for w in (w_in, p_lru, p_attn, w_out))
    wg = (0.5 * jnp.concatenate([lru_wa[:, 0], lru_wx[:, 0], lru_wa[:, 1], lru_wx[:, 1]], axis=-1)).astype(_BF16)
    bg = 0.5 * jnp.stack([lru_ba[:, 0], lru_bx[:, 0], lru_ba[:, 1], lru_bx[:, 1]], axis=1)
    lamv = jnp.stack([lam_q1, lam_k1, lam_q2, lam_k2], axis=1)
    gsub = attn_subln.reshape(depth, 1, -1)
    conv_b3 = conv_b.reshape(depth, 1, d_rnn)
    cache_k2 = cache_k.reshape(dec_batch, depth, past, qk_cols)
    cache_v2 = cache_v.reshape(dec_batch, depth, past, v_cols)
    cos, sin = _rope_tables(dec_seq)
    zero_state = jnp.zeros((batch, 2, d_rnn), _F32)

    cond = jnp.concatenate([c_ctx[None], c, jnp.zeros((MOD_ROWS - 1 - dec_batch, d), _F32)], axis=0)
    mod_all = _modulation(cond, w_mod, b_mod).reshape(depth, MOD_ROWS, N_MOD, d)

    xs = (x_prompt.reshape(n_prompt, d), x_sample.reshape(n_sample, d))
    caches, new_s = None, []
    for l in range(depth):
        mod = mod_all[l]
        gpre = [g_pre[l, i].reshape(1, d) for i in range(3)]
        gpost = [g_post[l, i].reshape(1, d) for i in range(3)]
        lam_init = 0.8 - 0.6 * math.exp(-0.3 * l)

        x = _ffn(xs, mod, group_of_tile, gpre[0], gpost[0], w1, w2, l, 0, 0, (tm, tm), tf, n_prompt, False)
        proj = _inproj(x, mod, group_of_tile, gpre[1], w_in_b, l, 3, tm_in, tn_in)
        rec, st_p = _lru(proj, None, conv_w, conv_b3, wg, bg, lru_lambda, zero_state, l, 0, seq, d_rnn)
        rec, _ = _lru(proj, rec, conv_w, conv_b3, wg, bg, lru_lambda, state_lru[:, l], l, n_prompt, dec_seq, d_rnn)
        att, *caches = _attention_prompt(proj, lamv, gsub, caches, l, depth, batch, seq, heads, col_q, lam_init)
        att = _attention_ctx(proj, att, lamv, gsub, cache_k2, cache_v2, cos, sin, l, n_prompt, dec_batch, dec_seq,
                             tq, heads, col_q, lam_init)
        x = _merge(x, mod, group_of_tile, gpost[1], rec, att, proj, col_gm, p_lru_b, p_attn_b, w_out_b,
                   l, 3, tm, tn_merge)
        last = l == depth - 1
        xs = _ffn((x,), mod, group_of_tile, gpre[2], gpost[2], w1, w2, l, 1, 6, (tm, tm), tf, n_prompt, last)
        xs = xs if last else (xs,)
        new_s.append(st_p)

    y_prompt = xs[0].reshape(batch, seq, d)
    y_sample = xs[1].reshape(dec_batch, dec_seq, d)
    new_k, new_v = (a.reshape(batch, depth, seq, heads, -1) for a in caches)
    return (y_prompt, y_sample, new_k, new_v, jnp.stack(new_s, axis=1))
```

```python
import functools
import math

import jax
import jax.numpy as jnp
from jax import lax
from jax.experimental import pallas as pl
from jax.experimental.pallas import tpu as pltpu

NORM_EPS = 1e-6
LRU_C = 8.0
N_MOD = 9
GRID_W = 64
ROPE_BASE = 10000.0
QK_DIM = 64
CONV_PAD_LEFT = 2

LANES = 128
SUBLANES = 8
MOD_ROWS = 16
VMEM_LIMIT = 52 * 1024 * 1024
VMEM_LIMIT_LARGE = 58 * 1024 * 1024

_BF16 = jnp.bfloat16
_F32 = jnp.float32


def _rms(x, g):
    ms = jnp.mean(x * x, axis=-1, keepdims=True)
    return x * lax.rsqrt(ms + NORM_EPS) * g


def _params(*sem, vmem=VMEM_LIMIT):
    return pltpu.CompilerParams(dimension_semantics=sem, vmem_limit_bytes=vmem)


def _mod_kernel(c_ref, w_ref, b_ref, o_ref):
    c = c_ref[...]
    s = (c * jax.nn.sigmoid(c)).astype(_BF16)
    o_ref[...] = jnp.dot(s, w_ref[...].astype(_BF16), preferred_element_type=_F32) + b_ref[...]


def _modulation(cond, w_mod, b_mod, tn=1024):
    depth, d, n = w_mod.shape
    return pl.pallas_call(
        _mod_kernel,
        out_shape=jax.ShapeDtypeStruct((depth, MOD_ROWS, n), _F32),
        grid=(depth, n // tn),
        in_specs=[
            pl.BlockSpec((MOD_ROWS, d), lambda l, j: (0, 0)),
            pl.BlockSpec((None, d, tn), lambda l, j: (l, 0, j)),
            pl.BlockSpec((None, 1, tn), lambda l, j: (l, 0, j)),
        ],
        out_specs=pl.BlockSpec((None, MOD_ROWS, tn), lambda l, j: (l, 0, j)),
        compiler_params=_params("parallel", "parallel"),
        name="modulation",
    )(cond, w_mod, b_mod.reshape(depth, 1, n))


NORM_ROWS = 16


def _for_row_chunks(n_rows, fn, pieces=None):
    if pieces is not None:
        pieces.extend(functools.partial(fn, pl.ds(i * NORM_ROWS, NORM_ROWS)) for i in range(n_rows // NORM_ROWS))
        return

    def body(i, carry):
        fn(pl.ds(pl.multiple_of(i * NORM_ROWS, NORM_ROWS), NORM_ROWS))
        return carry

    lax.fori_loop(0, n_rows // NORM_ROWS, body, 0, unroll=2)


def _once(fn, pieces=None):
    fn() if pieces is None else pieces.append(fn)


def _row_rstd(x_ref, rstd_ref, pieces=None):
    n_tiles = x_ref.shape[1] // LANES

    def lane_partial_sums(rows):
        sq = jnp.square(x_ref[rows, :])
        part = sq[:, 0:LANES]
        for t in range(1, n_tiles):
            part = part + sq[:, t * LANES:(t + 1) * LANES]
        rstd_ref[rows, :] = part

    def reduce():
        ms = jnp.sum(rstd_ref[...], axis=-1, keepdims=True) * (1.0 / x_ref.shape[1])
        rstd_ref[...] = jnp.broadcast_to(lax.rsqrt(ms + NORM_EPS), rstd_ref.shape)

    _for_row_chunks(x_ref.shape[0], lane_partial_sums, pieces)
    _once(reduce, pieces)


def _modulated_norm_rows(x_ref, h_ref, rstd_ref, vec_ref, g_ref, mod_ref, mod_base, pieces=None):
    n_tiles = x_ref.shape[1] // LANES

    def vectors():
        sh = mod_ref[mod_base:mod_base + 1, :]
        sc = mod_ref[mod_base + 1:mod_base + 2, :]
        vec_ref[0] = jnp.broadcast_to(g_ref[...] * (1.0 + sc), vec_ref.shape[1:])
        vec_ref[1] = jnp.broadcast_to(sh, vec_ref.shape[1:])

    def normalise(rows):
        rstd = jnp.tile(rstd_ref[rows, :], (1, n_tiles))
        h_ref[rows, :] = (x_ref[rows, :] * rstd * vec_ref[0] + vec_ref[1]).astype(h_ref.dtype)

    _row_rstd(x_ref, rstd_ref, pieces)
    _once(vectors, pieces)
    _for_row_chunks(x_ref.shape[0], normalise, pieces)


def _gated_residual_rows(x_ref, acc_ref, o_ref, rstd_ref, vec_ref, g_ref, mod_ref, gate_row, gate_scale,
                         pieces=None):
    n_tiles = x_ref.shape[1] // LANES

    def vectors():
        gt = mod_ref[gate_row:gate_row + 1, :]
        if gate_scale != 1.0:
            gt = gate_scale * gt
        vec_ref[0] = jnp.broadcast_to(gt * g_ref[...], vec_ref.shape[1:])

    def residual(rows):
        rstd = jnp.tile(rstd_ref[rows, :], (1, n_tiles))
        o_ref[rows, :] = x_ref[rows, :] + acc_ref[rows, :] * rstd * vec_ref[0]

    _row_rstd(acc_ref, rstd_ref, pieces)
    _once(vectors, pieces)
    _for_row_chunks(x_ref.shape[0], residual, pieces)


def _on_part(r, n_first, refs_first, refs_second, fn):
    if all(a is b for a, b in zip(refs_first, refs_second)):
        fn(*refs_first)
        return
    pl.when(r < n_first)(lambda: fn(*refs_first))
    pl.when(r >= n_first)(lambda: fn(*refs_second))


def _ffn_kernel(*refs, mod_base, n_first, split_in, split_out):
    refs = list(refs)
    xa_ref = refs.pop(0)
    xb_ref = refs.pop(0) if split_in else xa_ref
    mod_ref, gpre_ref, gpost_ref, w1g_ref, w1u_ref, w2_ref, oa_ref = refs[:7]
    ob_ref = refs[7] if split_out else oa_ref
    h_ref, acc_ref, rstd_ref, vec_ref = refs[-4:]
    r, j = pl.program_id(0), pl.program_id(1)

    @pl.when(j == 0)
    def _():
        _on_part(r, n_first, (xa_ref,), (xb_ref,),
                 lambda x_ref: _modulated_norm_rows(x_ref, h_ref, rstd_ref, vec_ref, gpre_ref, mod_ref,
                                                    mod_base))
        acc_ref[...] = jnp.zeros_like(acc_ref)

    h = h_ref[...]
    g = jnp.dot(h, w1g_ref[...], preferred_element_type=_F32)
    u = jnp.dot(h, w1u_ref[...], preferred_element_type=_F32)
    act = (g * jax.nn.sigmoid(g) * u).astype(_BF16)
    acc_ref[...] += jnp.dot(act, w2_ref[...], preferred_element_type=_F32)

    @pl.when(j == pl.num_programs(1) - 1)
    def _():
        _on_part(r, n_first, (xa_ref, oa_ref), (xb_ref, ob_ref),
                 lambda x_ref, o_ref: _gated_residual_rows(x_ref, acc_ref, o_ref, rstd_ref, vec_ref, gpost_ref,
                                                           mod_ref, mod_base + 2, 0.5))


def _ffn(xs, mod, group_of_tile, gpre, gpost, w1, w2, l, i, mod_base, tiles, tf, n_prompt, split_out):
    d = xs[0].shape[1]
    n = sum(x.shape[0] for x in xs)
    split_in = len(xs) == 2
    tm = tiles[1] if (split_in or split_out) else tiles[0]
    n_first = n_prompt // tm
    nff = w2.shape[2] // tf

    def first(r, j):
        return jnp.minimum(r, n_first - 1), 0

    def second(r, j):
        return jnp.maximum(r - n_first, 0), 0

    def whole(r, j):
        return r, 0

    x_mode = dict(pipeline_mode=pl.Buffered(1)) if tm * d * 4 >= 8 * 1024 * 1024 else {}
    x_specs = ([pl.BlockSpec((tm, d), first, **x_mode), pl.BlockSpec((tm, d), second, **x_mode)] if split_in
               else [pl.BlockSpec((tm, d), whole, **x_mode)])
    if split_out:
        out_shape = (jax.ShapeDtypeStruct((n_first * tm, d), _F32), jax.ShapeDtypeStruct((n - n_first * tm, d), _F32))
        out_specs = (pl.BlockSpec((tm, d), first), pl.BlockSpec((tm, d), second))
    else:
        out_shape = jax.ShapeDtypeStruct((n, d), _F32)
        out_specs = pl.BlockSpec((tm, d), whole)
    return pl.pallas_call(
        functools.partial(_ffn_kernel, mod_base=mod_base, n_first=n_first, split_in=split_in, split_out=split_out),
        out_shape=out_shape,
        grid=(n // tm, nff),
        in_specs=x_specs + [
            pl.BlockSpec((None, N_MOD, d), lambda r, j: (group_of_tile(r, tm), 0, 0)),
            pl.BlockSpec((1, d), lambda r, j: (0, 0)),
            pl.BlockSpec((1, d), lambda r, j: (0, 0)),
            pl.BlockSpec((None, None, d, tf), lambda r, j: (l, i, 0, j)),
            pl.BlockSpec((None, None, d, tf), lambda r, j: (l, i, 0, j)),
            pl.BlockSpec((None, None, tf, d), lambda r, j: (l, i, j, 0)),
        ],
        out_specs=out_specs,
        scratch_shapes=[pltpu.VMEM((tm, d), _BF16), pltpu.VMEM((tm, d), _F32), pltpu.VMEM((tm, LANES), _F32),
                        pltpu.VMEM((2, NORM_ROWS, d), _F32)],
        compiler_params=_params("parallel", "arbitrary", vmem=VMEM_LIMIT_LARGE if x_mode else VMEM_LIMIT),
        name=f"ffn_l{l}_{i}",
    )(*xs, mod, gpre, gpost, *w1, w2)


def _ffn_overlap_kernel(x_ref, mod_ref, gpre_ref, gpost_ref, w1g_ref, w1u_ref, w2_ref, o_ref,
                        h_ref, h_next_ref, acc_ref, acc_done_ref, rstd_ref, vec_ref, *, mod_base, n_tiles):
    r, j = pl.program_id(0), pl.program_id(1)
    last = pl.num_programs(1) - 1

    def ffn_update(h):
        g = jnp.dot(h, w1g_ref[...], preferred_element_type=_F32)
        u = jnp.dot(h, w1u_ref[...], preferred_element_type=_F32)
        act = (g * jax.nn.sigmoid(g) * u).astype(_BF16)
        return jnp.dot(act, w2_ref[...], preferred_element_type=_F32)

    def run(pieces):
        for piece in pieces:
            piece()

    def finish_previous(pieces=None):
        _gated_residual_rows(x_ref, acc_done_ref, o_ref, rstd_ref, vec_ref, gpost_ref, mod_ref, mod_base + 2, 0.5,
                             pieces)

    def prepare_next(pieces=None):
        _modulated_norm_rows(x_ref, h_next_ref, rstd_ref, vec_ref, gpre_ref, mod_ref, mod_base, pieces)

    @pl.when((r == 0) & (j == 0))
    def _():
        prepare_next()
        acc_done_ref[...] = jnp.zeros_like(acc_done_ref)

    @pl.when(r < n_tiles)
    def _():
        @pl.when(j == 0)
        def _():
            h = h_next_ref[...]
            acc_ref[...] = ffn_update(h)
            h_ref[...] = h
            under_matmul = []
            finish_previous(under_matmul)
            run(under_matmul)

        @pl.when((j > 0) & (j < last))
        def _():
            acc_ref[...] += ffn_update(h_ref[...])

        @pl.when(j == last)
        def _():
            acc_done_ref[...] = acc_ref[...] + ffn_update(h_ref[...])
            under_matmul = []
            prepare_next(under_matmul)
            run(under_matmul)

    @pl.when((r == n_tiles) & (j == 0))
    def _():
        finish_previous()


def _ffn_overlapped(x, mod, group_of_tile, gpre, gpost, w1, w2, l, i, mod_base, tm, tf):
    n, d = x.shape
    n_tiles = n // tm
    nff = w2.shape[2] // tf

    def neighbour(r, j):
        return jnp.where(j == nff - 1, jnp.minimum(r + 1, n_tiles - 1), jnp.maximum(r - 1, 0))

    def weight_tile(r, j):
        return jnp.where(r < n_tiles, j, nff - 1)

    return pl.pallas_call(
        functools.partial(_ffn_overlap_kernel, mod_base=mod_base, n_tiles=n_tiles),
        out_shape=jax.ShapeDtypeStruct((n, d), _F32),
        grid=(n_tiles + 1, nff),
        in_specs=[
            pl.BlockSpec((tm, d), lambda r, j: (neighbour(r, j), 0)),
            pl.BlockSpec((None, N_MOD, d), lambda r, j: (group_of_tile(neighbour(r, j), tm), 0, 0)),
            pl.BlockSpec((1, d), lambda r, j: (0, 0)),
            pl.BlockSpec((1, d), lambda r, j: (0, 0)),
            pl.BlockSpec((None, None, d, tf), lambda r, j: (l, i, 0, weight_tile(r, j))),
            pl.BlockSpec((None, None, d, tf), lambda r, j: (l, i, 0, weight_tile(r, j))),
            pl.BlockSpec((None, None, tf, d), lambda r, j: (l, i, weight_tile(r, j), 0)),
        ],
        out_specs=pl.BlockSpec((tm, d), lambda r, j: (jnp.maximum(r - 1, 0), 0)),
        scratch_shapes=[pltpu.VMEM((tm, d), _BF16)] * 2 + [pltpu.VMEM((tm, d), _F32)] * 2
        + [pltpu.VMEM((tm, LANES), _F32), pltpu.VMEM((2, NORM_ROWS, d), _F32)],
        compiler_params=_params("arbitrary", "arbitrary"),
        name=f"ffn_l{l}_{i}",
    )(x, mod, gpre, gpost, *w1, w2)


def _inproj_kernel(x_ref, mod_ref, gpre_ref, w_ref, o_ref, h_ref, rstd_ref, vec_ref, *, mod_base):
    @pl.when(pl.program_id(1) == 0)
    def _():
        _modulated_norm_rows(x_ref, h_ref, rstd_ref, vec_ref, gpre_ref, mod_ref, mod_base)

    o_ref[...] = jnp.dot(h_ref[...], w_ref[...], preferred_element_type=_F32)


def _inproj(x, mod, group_of_tile, gpre, w_in, l, mod_base, tm, tn):
    n, d = x.shape
    cols = w_in.shape[2]
    return pl.pallas_call(
        functools.partial(_inproj_kernel, mod_base=mod_base),
        out_shape=jax.ShapeDtypeStruct((n, cols), _F32),
        grid=(n // tm, cols // tn),
        in_specs=[
            pl.BlockSpec((tm, d), lambda r, j: (r, 0)),
            pl.BlockSpec((None, N_MOD, d), lambda r, j: (group_of_tile(r, tm), 0, 0)),
            pl.BlockSpec((1, d), lambda r, j: (0, 0)),
            pl.BlockSpec((None, d, tn), lambda r, j: (l, 0, j)),
        ],
        out_specs=pl.BlockSpec((tm, tn), lambda r, j: (r, j)),
        scratch_shapes=[pltpu.VMEM((tm, d), _BF16), pltpu.VMEM((tm, LANES), _F32),
                        pltpu.VMEM((2, NORM_ROWS, d), _F32)],
        compiler_params=_params("parallel", "arbitrary"),
        name=f"inproj_l{l}",
    )(x, mod, gpre, w_in)


LRU_TILES = SUBLANES // 2
CONV_HALO = SUBLANES
XP_PITCH = 2


def _sigmoid(x):
    return 0.5 * jnp.tanh(0.5 * x) + 0.5


def _lru_kernel(*refs, seq, row_chunk, unroll, aliased):
    if aliased:
        refs = refs[:8] + refs[9:]
    (xr_ref, gr_ref, cw_ref, cb_ref, wg_ref, bg_ref, lam_ref, h0_ref, rec_ref, st_ref,
     xp_ref, a_ref, b_ref, h_ref) = refs
    nt = LRU_TILES
    halo = jnp.zeros((XP_PITCH * CONV_HALO, LANES), _F32)
    for n in range(nt):
        pair, member = divmod(n, XP_PITCH)
        xp_ref[pair, 0:XP_PITCH * CONV_HALO, :] = halo
        xp_ref[pair, XP_PITCH * (CONV_HALO + seq):XP_PITCH * (2 * CONV_HALO + seq), :] = halo
        xp_ref[pair, pl.ds(XP_PITCH * CONV_HALO + member, seq, stride=XP_PITCH), :] = (
            xr_ref[:, n * LANES:(n + 1) * LANES])

    neg_log_a_scale = (0.5 * LRU_C) * jax.nn.softplus(-lam_ref[...])
    exp2_scale = -math.log2(math.e) * neg_log_a_scale

    def coeffs(c, carry):
        r0 = c * row_chunk
        for n in range(nt):
            cs = slice(n * LANES, (n + 1) * LANES)
            pair, member = divmod(n, XP_PITCH)
            y = cb_ref[:, cs]
            for tap in range(cw_ref.shape[0]):
                first = XP_PITCH * (r0 + CONV_HALO + tap - CONV_PAD_LEFT) + member
                y = y + cw_ref[tap:tap + 1, cs] * xp_ref[pair, pl.ds(first, row_chunk, stride=XP_PITCH), :]
            g4 = jnp.dot(y.astype(_BF16), wg_ref[n], preferred_element_type=_F32)
            half_y = 0.5 * y
            for direction in range(2):
                ua = 1.0 + jnp.tanh(g4[:, (2 * direction) * LANES:(2 * direction + 1) * LANES]
                                    + bg_ref[2 * direction:2 * direction + 1, cs])
                ux = 1.0 + jnp.tanh(g4[:, (2 * direction + 1) * LANES:(2 * direction + 2) * LANES]
                                    + bg_ref[2 * direction + 1:2 * direction + 2, cs])
                a = jnp.exp2(ua * exp2_scale[direction:direction + 1, cs])
                one_minus_a2 = jnp.tanh(ua * neg_log_a_scale[direction:direction + 1, cs]) * (1.0 + a * a)
                mult = jnp.where(one_minus_a2 > 0.0, one_minus_a2 * lax.rsqrt(one_minus_a2), 0.0)
                dense_rows = pl.ds(r0 * SUBLANES + nt * direction + n, row_chunk, stride=SUBLANES)
                a_ref[dense_rows, :] = a
                b_ref[dense_rows, :] = mult * ux * half_y
        return carry

    for c in range(seq // row_chunk):
        coeffs(c, 0)

    is_fwd = lax.broadcasted_iota(jnp.int32, (SUBLANES, LANES), 0) < nt

    block = unroll * SUBLANES

    def scan(c, h):
        rf = pl.multiple_of(c * block, block)
        rb = pl.multiple_of((seq - unroll) * SUBLANES - c * block, block)
        a_f, a_b = a_ref[pl.ds(rf, block), :], a_ref[pl.ds(rb, block), :]
        b_f, b_b = b_ref[pl.ds(rf, block), :], b_ref[pl.ds(rb, block), :]
        def coeff(r):
            up = slice(r * SUBLANES, (r + 1) * SUBLANES)
            down = slice((unroll - 1 - r) * SUBLANES, (unroll - r) * SUBLANES)
            return jnp.where(is_fwd, a_f[up], a_b[down]), jnp.where(is_fwd, b_f[up], b_b[down])

        def emit(r, h_r):
            h_ref[pl.ds(rf + r * SUBLANES, nt), :] = h_r[0:nt]
            h_ref[pl.ds(rb + (unroll - 1 - r) * SUBLANES + nt, nt), :] = h_r[nt:2 * nt]

        for r in range(0, unroll, 2):
            (a0, b0), (a1, b1) = coeff(r), coeff(r + 1)
            emit(r, a0 * h + b0)
            h = (a1 * a0) * h + (a1 * b0 + b1)
            emit(r + 1, h)
        return h

    h0 = jnp.concatenate([h0_ref[d:d + 1, n * LANES:(n + 1) * LANES] for d in range(2) for n in range(nt)], axis=0)
    h = lax.fori_loop(0, seq // unroll, scan, h0)
    for n in range(nt):
        st_ref[0:1, n * LANES:(n + 1) * LANES] = h[n:n + 1]
        st_ref[1:2, n * LANES:(n + 1) * LANES] = h[nt + n:nt + n + 1]

    def gate(c, carry):
        r0 = pl.multiple_of(c * row_chunk, row_chunk)
        for n in range(nt):
            cs = slice(n * LANES, (n + 1) * LANES)
            hf = h_ref[pl.ds(r0 * SUBLANES + n, row_chunk, stride=SUBLANES), :]
            hb = h_ref[pl.ds(r0 * SUBLANES + nt + n, row_chunk, stride=SUBLANES), :]
            rec_ref[pl.ds(r0, row_chunk), cs] = (
                (hf + hb) * jax.nn.gelu(gr_ref[pl.ds(r0, row_chunk), cs])).astype(rec_ref.dtype)
        return carry

    lax.fori_loop(0, seq // row_chunk, gate, 0)


def _lru(proj, rec, conv_w, conv_b, wg, bg, lam, h0, l, first_row, seq, d_rnn):
    n_seq = h0.shape[0]
    rb0 = first_row // seq
    cg = LRU_TILES * LANES
    ncg = d_rnn // cg
    row_chunk = min(seq, 256)
    aliased = rec is not None
    in_specs = [
        pl.BlockSpec((seq, cg), lambda s, c: (rb0 + s, c)),
        pl.BlockSpec((seq, cg), lambda s, c: (rb0 + s, ncg + c)),
        pl.BlockSpec((None, conv_w.shape[1], cg), lambda s, c: (l, 0, c)),
        pl.BlockSpec((None, 1, cg), lambda s, c: (l, 0, c)),
        pl.BlockSpec((None, LRU_TILES, LANES, 4 * LANES), lambda s, c: (l, c, 0, 0)),
        pl.BlockSpec((None, 4, cg), lambda s, c: (l, 0, c)),
        pl.BlockSpec((None, 2, cg), lambda s, c: (l, 0, c)),
        pl.BlockSpec((None, 2, cg), lambda s, c: (s, 0, c)),
    ]
    args = [proj, proj, conv_w, conv_b, wg, bg, lam, h0]
    if aliased:
        in_specs.append(pl.BlockSpec(memory_space=pl.ANY))
        args.append(rec)
    return pl.pallas_call(
        functools.partial(_lru_kernel, seq=seq, row_chunk=row_chunk, unroll=8, aliased=aliased),
        out_shape=(jax.ShapeDtypeStruct((proj.shape[0], d_rnn), _BF16),
                   jax.ShapeDtypeStruct((n_seq, 2, d_rnn), _F32)),
        grid=(n_seq, ncg),
        in_specs=in_specs,
        out_specs=(pl.BlockSpec((seq, cg), lambda s, c: (rb0 + s, c)),
                   pl.BlockSpec((None, 2, cg), lambda s, c: (s, 0, c))),
        scratch_shapes=[pltpu.VMEM((LRU_TILES // XP_PITCH, XP_PITCH * (seq + 2 * CONV_HALO), LANES), _F32)]
        + [pltpu.VMEM((seq * SUBLANES, LANES), _F32)] * 3,
        input_output_aliases={8: 0} if aliased else {},
        compiler_params=_params("parallel", "parallel", vmem=VMEM_LIMIT_LARGE),
        name=f"rglru_l{l}_t{seq}",
    )(*args)


Q_SCALE = QK_DIM ** -0.5 * math.log2(math.e)


def _rope(x, cos, sin_signed):
    lane = lax.broadcasted_iota(jnp.int32, x.shape, 1)
    half = QK_DIM // 4
    partner = jnp.where((lane & half) == 0, pltpu.roll(x, LANES - half, 1), pltpu.roll(x, half, 1))
    return x * cos + partner * sin_signed


def _lambda(lam_ref, lam_init):
    lv = lam_ref[...]
    return (jnp.exp(jnp.sum(lv[0:1] * lv[1:2], axis=-1, keepdims=True))
            - jnp.exp(jnp.sum(lv[2:3] * lv[3:4], axis=-1, keepdims=True)) + lam_init)


def _scores(q, kb):
    lane = lax.broadcasted_iota(jnp.int32, q.shape, 1)
    qm = jnp.concatenate([jnp.where(lane < QK_DIM, q, 0.0), jnp.where(lane >= QK_DIM, q, 0.0)], axis=0)
    return lax.dot_general(qm.astype(_BF16), kb, (((1,), (1,)), ((), ())), preferred_element_type=_F32)


def _softmax_diff_pv(s, vb1, lam, g, lam_init):
    tq = s.shape[0] // 2
    e = jnp.exp2(s - jnp.max(s, axis=-1, keepdims=True)).astype(_BF16)
    ov = jnp.dot(e, vb1, preferred_element_type=_F32)
    o = (ov[0:tq, 0:LANES] / ov[0:tq, LANES:2 * LANES]
         - lam * (ov[tq:2 * tq, 0:LANES] / ov[tq:2 * tq, LANES:2 * LANES]))
    return _rms(o, g) * (1.0 - lam_init)


def _attn_ctx_kernel(q_ref, k_ref, v_ref, ck_ref, cv_ref, cq_ref, sq_ref, ckk_ref, skk_ref, lam_ref, g_ref,
                     att_in_ref, o_ref, kbuf, vbuf, *, past, seq, sub, lam_init):
    del att_in_ref

    @pl.when(pl.program_id(2) == 0)
    def _():
        kbuf[0:past, :] = ck_ref[...].astype(kbuf.dtype)
        kbuf[past:past + seq, :] = _rope(k_ref[...], ckk_ref[...], skk_ref[...]).astype(kbuf.dtype)
        vbuf[0:past, 0:LANES] = cv_ref[...].astype(vbuf.dtype)
        vbuf[past:past + seq, 0:LANES] = v_ref[...].astype(vbuf.dtype)
        vbuf[:, LANES:2 * LANES] = jnp.ones((past + seq, LANES), vbuf.dtype)

    lam = _lambda(lam_ref, lam_init)

    def scores(i):
        rows = slice(i * sub, (i + 1) * sub)
        return _scores(_rope(q_ref[rows, :], cq_ref[rows, :], sq_ref[rows, :]) * Q_SCALE, kbuf[...])

    s = scores(0)
    for i in range(q_ref.shape[0] // sub):
        s_next = scores(i + 1) if (i + 1) * sub < q_ref.shape[0] else None
        o = _softmax_diff_pv(s, vbuf[...], lam, g_ref[...], lam_init)
        o_ref[i * sub:(i + 1) * sub, :] = o.astype(o_ref.dtype)
        s = s_next


def _attn_prompt_kernel(*refs, heads, lam_init, aliased):
    if aliased:
        refs = refs[:5] + refs[7:]
    q_ref, k_ref, v_ref, lam_ref, g_ref, o_ref, nk_ref, nv_ref = refs
    nk_ref[...] = k_ref[...]
    nv_ref[...] = v_ref[...]
    lam = _lambda(lam_ref, lam_init)
    ones = jnp.ones((k_ref.shape[0], LANES), _BF16)
    for h in range(heads):
        hs = slice(h * LANES, (h + 1) * LANES)
        vb1 = jnp.concatenate([v_ref[:, hs].astype(_BF16), ones], axis=1)
        s = _scores(q_ref[:, hs] * Q_SCALE, k_ref[:, hs].astype(_BF16))
        o = _softmax_diff_pv(s, vb1, lam, g_ref[...], lam_init)
        o_ref[:, hs] = o.astype(o_ref.dtype)


def _attention_prompt(proj, lamv, gsub, caches, l, depth, n_seq, seq, heads, col_q, lam_init):
    width = heads * LANES
    cq = col_q // width
    aliased = caches is not None
    in_specs = [
        pl.BlockSpec((seq, width), lambda b: (b, cq)),
        pl.BlockSpec((seq, width), lambda b: (b, cq + 1)),
        pl.BlockSpec((seq, width), lambda b: (b, cq + 2)),
        pl.BlockSpec((None, 4, QK_DIM), lambda b: (l, 0, 0)),
        pl.BlockSpec((None, 1, LANES), lambda b: (l, 0, 0)),
    ]
    args = [proj, proj, proj, lamv, gsub]
    if aliased:
        in_specs += [pl.BlockSpec(memory_space=pl.ANY)] * 2
        args += list(caches)
    cache_sds = jax.ShapeDtypeStruct((n_seq, depth, seq, width), _F32)
    return pl.pallas_call(
        functools.partial(_attn_prompt_kernel, heads=heads, lam_init=lam_init, aliased=aliased),
        out_shape=(jax.ShapeDtypeStruct((proj.shape[0], width), _BF16), cache_sds, cache_sds),
        grid=(n_seq,),
        in_specs=in_specs,
        out_specs=(pl.BlockSpec((seq, width), lambda b: (b, 0)),
                   pl.BlockSpec((None, None, seq, width), lambda b: (b, l, 0, 0)),
                   pl.BlockSpec((None, None, seq, width), lambda b: (b, l, 0, 0))),
        input_output_aliases={5: 1, 6: 2} if aliased else {},
        compiler_params=_params("parallel"),
        name=f"diffattn_prompt_l{l}",
    )(*args)


def _attention_ctx(proj, att, lamv, gsub, cache_k, cache_v, cos, sin, l, first_row, n_seq, seq, tq, heads,
                   col_q, lam_init):
    past = cache_k.shape[2]
    rq0, rk0 = first_row // tq, first_row // seq
    nq = seq // tq
    hq, hk, hv = col_q // LANES, col_q // LANES + heads, col_q // LANES + 2 * heads
    in_specs = [
        pl.BlockSpec((tq, LANES), lambda b, h, t: (rq0 + b * nq + t, hq + h)),
        pl.BlockSpec((seq, LANES), lambda b, h, t: (rk0 + b, hk + h)),
        pl.BlockSpec((seq, LANES), lambda b, h, t: (rk0 + b, hv + h)),
        pl.BlockSpec((None, None, past, LANES), lambda b, h, t: (b, l, 0, h)),
        pl.BlockSpec((None, None, past, LANES), lambda b, h, t: (b, l, 0, h)),
        pl.BlockSpec((tq, LANES), lambda b, h, t: (t, 0)),
        pl.BlockSpec((tq, LANES), lambda b, h, t: (t, 0)),
        pl.BlockSpec((seq, LANES), lambda b, h, t: (0, 0)),
        pl.BlockSpec((seq, LANES), lambda b, h, t: (0, 0)),
        pl.BlockSpec((None, 4, QK_DIM), lambda b, h, t: (l, 0, 0)),
        pl.BlockSpec((None, 1, LANES), lambda b, h, t: (l, 0, 0)),
        pl.BlockSpec(memory_space=pl.ANY),
    ]
    return pl.pallas_call(
        functools.partial(_attn_ctx_kernel, past=past, seq=seq, sub=min(tq, 256), lam_init=lam_init),
        out_shape=jax.ShapeDtypeStruct(att.shape, att.dtype),
        grid=(n_seq, heads, nq),
        in_specs=in_specs,
        out_specs=pl.BlockSpec((tq, LANES), lambda b, h, t: (rq0 + b * nq + t, h)),
        scratch_shapes=[pltpu.VMEM((past + seq, LANES), _BF16), pltpu.VMEM((past + seq, 2 * LANES), _BF16)],
        input_output_aliases={11: 0},
        compiler_params=_params("parallel", "parallel", "arbitrary"),
        name=f"diffattn_ctx_l{l}",
    )(proj, proj, proj, cache_k, cache_v, cos, sin, cos, sin, lamv, gsub, att)


def _merge_kernel(x_ref, mod_ref, gpost_ref, rec_ref, att_ref, gm1_ref, gm2_ref, plru_ref, pattn_ref,
                  wout_ref, o_ref, acc_ref, rstd_ref, vec_ref, *, mod_base):
    j = pl.program_id(1)

    @pl.when(j == 0)
    def _():
        acc_ref[...] = jnp.zeros_like(acc_ref)

    a = jnp.dot(rec_ref[...], plru_ref[...], preferred_element_type=_F32)
    b = jnp.dot(att_ref[...], pattn_ref[...], preferred_element_type=_F32)
    merged = _sigmoid(gm1_ref[...]) * a + _sigmoid(gm2_ref[...]) * b
    acc_ref[...] += jnp.dot(merged.astype(_BF16), wout_ref[...], preferred_element_type=_F32)

    @pl.when(j == pl.num_programs(1) - 1)
    def _():
        _gated_residual_rows(x_ref, acc_ref, o_ref, rstd_ref, vec_ref, gpost_ref, mod_ref, mod_base + 2, 1.0)


def _merge(x, mod, group_of_tile, gpost, rec, att, proj, col_gm, p_lru, p_attn, w_out, l, mod_base, tm, tn):
    n, d = x.shape
    d_rnn, v_cols = rec.shape[1], att.shape[1]
    g1, g2 = col_gm // tn, (col_gm + d) // tn
    return pl.pallas_call(
        functools.partial(_merge_kernel, mod_base=mod_base),
        out_shape=jax.ShapeDtypeStruct((n, d), _F32),
        grid=(n // tm, d // tn),
        in_specs=[
            pl.BlockSpec((tm, d), lambda r, j: (r, 0)),
            pl.BlockSpec((None, N_MOD, d), lambda r, j: (group_of_tile(r, tm), 0, 0)),
            pl.BlockSpec((1, d), lambda r, j: (0, 0)),
            pl.BlockSpec((tm, d_rnn), lambda r, j: (r, 0)),
            pl.BlockSpec((tm, v_cols), lambda r, j: (r, 0)),
            pl.BlockSpec((tm, tn), lambda r, j: (r, g1 + j)),
            pl.BlockSpec((tm, tn), lambda r, j: (r, g2 + j)),
            pl.BlockSpec((None, d_rnn, tn), lambda r, j: (l, 0, j)),
            pl.BlockSpec((None, v_cols, tn), lambda r, j: (l, 0, j)),
            pl.BlockSpec((None, tn, d), lambda r, j: (l, j, 0)),
        ],
        out_specs=pl.BlockSpec((tm, d), lambda r, j: (r, 0)),
        scratch_shapes=[pltpu.VMEM((tm, d), _F32), pltpu.VMEM((tm, LANES), _F32),
                        pltpu.VMEM((2, NORM_ROWS, d), _F32)],
        compiler_params=_params("parallel", "arbitrary"),
        name=f"merge_l{l}",
    )(x, mod, gpost, rec, att, proj, proj, p_lru, p_attn, w_out)


def _rope_tables(n_tokens):
    t = jnp.arange(n_tokens)
    row = (t // GRID_W).astype(_F32)
    col = (t % GRID_W).astype(_F32)
    n_freq = QK_DIM // 4
    freqs = 1.0 / (ROPE_BASE ** (jnp.arange(0, 2 * n_freq, 2, dtype=_F32) / (2 * n_freq)))
    ar, ac = row[:, None] * freqs, col[:, None] * freqs
    cos = jnp.concatenate([jnp.cos(ar), jnp.cos(ar), jnp.cos(ac), jnp.cos(ac)], axis=-1)
    sin = jnp.concatenate([-jnp.sin(ar), jnp.sin(ar), -jnp.sin(ac), jnp.sin(ac)], axis=-1)
    reps = LANES // QK_DIM
    return jnp.tile(cos, (1, reps)), jnp.tile(sin, (1, reps))


def _largest_tile(candidates, *extents):
    return next(t for t in candidates if all(e % t == 0 for e in extents))


def kernel(x_prompt, x_sample, c, cache_k, cache_v, state_lru, c_ctx, w_mod, b_mod, g_pre, g_post, ffn_w1, ffn_w2, w_in, conv_w, conv_b, lru_wa, lru_ba, lru_wx, lru_bx, lru_lambda, lam_q1, lam_k1, lam_q2, lam_k2, attn_subln, p_lru, p_attn, w_out):
    batch, seq, d = x_prompt.shape
    dec_batch, dec_seq, _ = x_sample.shape
    depth = w_mod.shape[0]
    d_rnn = conv_w.shape[2]
    heads = cache_k.shape[3]
    past = cache_k.shape[2]
    v_cols = heads * cache_v.shape[4]
    qk_cols = heads * cache_k.shape[4]
    d_ff = ffn_w2.shape[2]
    n_prompt, n_sample = batch * seq, dec_batch * dec_seq
    assert cache_k.shape[4] == LANES and cache_v.shape[4] == LANES and 1 + dec_batch <= MOD_ROWS
    col_q = 2 * d_rnn
    col_gm = col_q + 2 * qk_cols + v_cols

    tm = _largest_tile((512, 256), n_prompt, dec_seq)
    tm_in = _largest_tile((1024, 512, 256), n_prompt, dec_seq)
    tq = _largest_tile((1024, 512, 256, 128), dec_seq)
    tf, tn_in, tn_merge = 512, 1024, 512

    def group_of_tile(r, rows):
        first = n_prompt // rows
        return jnp.where(r < first, 0, 1 + (r - first) // (dec_seq // rows))

    ffp = -(-d_ff // tf) * tf
    padc = ((0, 0), (0, 0), (0, 0), (0, ffp - d_ff))
    w1 = (jnp.pad(ffn_w1[..., :d_ff], padc).astype(_BF16), jnp.pad(ffn_w1[..., d_ff:], padc).astype(_BF16))
    w2 = jnp.pad(ffn_w2, ((0, 0), (0, 0), (0, ffp - d_ff), (0, 0))).astype(_BF16)
    w_in_b, p_lru_b, p_attn_b, w_out_b = (w.astype(_BF16) for w in (w_in, p_lru, p_attn, w_out))
    wg = (0.5 * jnp.concatenate([lru_wa[:, 0], lru_wx[:, 0], lru_wa[:, 1], lru_wx[:, 1]], axis=-1)).astype(_BF16)
    bg = 0.5 * jnp.stack([lru_ba[:, 0], lru_bx[:, 0], lru_ba[:, 1], lru_bx[:, 1]], axis=1)
    lamv = jnp.stack([lam_q1, lam_k1, lam_q2, lam_k2], axis=1)
    gsub = attn_subln.reshape(depth, 1, -1)
    conv_b3 = conv_b.reshape(depth, 1, d_rnn)
    cache_k2 = cache_k.reshape(dec_batch, depth, past, qk_cols)
    cache_v2 = cache_v.reshape(dec_batch, depth, past, v_cols)
    cos, sin = _rope_tables(dec_seq)
    zero_state = jnp.zeros((batch, 2, d_rnn), _F32)

    cond = jnp.concatenate([c_ctx[None], c, jnp.zeros((MOD_ROWS - 1 - dec_batch, d), _F32)], axis=0)
    mod_all = _modulation(cond, w_mod, b_mod).reshape(depth, MOD_ROWS, N_MOD, d)

    xs = (x_prompt.reshape(n_prompt, d), x_sample.reshape(n_sample, d))
    caches, new_s = None, []
    for l in range(depth):
        mod = mod_all[l]
        gpre = [g_pre[l, i].reshape(1, d) for i in range(3)]
        gpost = [g_post[l, i].reshape(1, d) for i in range(3)]
        lam_init = 0.8 - 0.6 * math.exp(-0.3 * l)

        if len(xs) == 2:
            x = _ffn(xs, mod, group_of_tile, gpre[0], gpost[0], w1, w2, l, 0, 0, (tm, tm), tf, n_prompt, False)
        else:
            x = _ffn_overlapped(xs[0], mod, group_of_tile, gpre[0], gpost[0], w1, w2, l, 0, 0, tm, tf)
        proj = _inproj(x, mod, group_of_tile, gpre[1], w_in_b, l, 3, tm_in, tn_in)
        rec, st_p = _lru(proj, None, conv_w, conv_b3, wg, bg, lru_lambda, zero_state, l, 0, seq, d_rnn)
        rec, _ = _lru(proj, rec, conv_w, conv_b3, wg, bg, lru_lambda, state_lru[:, l], l, n_prompt, dec_seq, d_rnn)
        att, *caches = _attention_prompt(proj, lamv, gsub, caches, l, depth, batch, seq, heads, col_q, lam_init)
        att = _attention_ctx(proj, att, lamv, gsub, cache_k2, cache_v2, cos, sin, l, n_prompt, dec_batch, dec_seq,
                             tq, heads, col_q, lam_init)
        x = _merge(x, mod, group_of_tile, gpost[1], rec, att, proj, col_gm, p_lru_b, p_attn_b, w_out_b,
                   l, 3, tm, tn_merge)
        last = l == depth - 1
        if last:
            xs = _ffn((x,), mod, group_of_tile, gpre[2], gpost[2], w1, w2, l, 1, 6, (tm, tm), tf, n_prompt, True)
        else:
            xs = (_ffn_overlapped(x, mod, group_of_tile, gpre[2], gpost[2], w1, w2, l, 1, 6, tm, tf),)
        new_s.append(st_p)

    y_prompt = xs[0].reshape(batch, seq, d)
    y_sample = xs[1].reshape(dec_batch, dec_seq, d)
    new_k, new_v = (a.reshape(batch, depth, seq, heads, -1) for a in caches)
    return (y_prompt, y_sample, new_k, new_v, jnp.stack(new_s, axis=1))
```

```python
import functools
import math

import jax
import jax.numpy as jnp
from jax import lax
from jax.experimental import pallas as pl
from jax.experimental.pallas import tpu as pltpu

NORM_EPS = 1e-6
LRU_C = 8.0
N_MOD = 9
GRID_W = 64
ROPE_BASE = 10000.0
QK_DIM = 64
CONV_PAD_LEFT = 2

LANES = 128
SUBLANES = 8
MOD_ROWS = 16
VMEM_LIMIT = 52 * 1024 * 1024
VMEM_LIMIT_LARGE = 58 * 1024 * 1024

_BF16 = jnp.bfloat16
_F32 = jnp.float32


def _rms(x, g):
    ms = jnp.mean(x * x, axis=-1, keepdims=True)
    return x * lax.rsqrt(ms + NORM_EPS) * g


def _params(*sem, vmem=VMEM_LIMIT):
    return pltpu.CompilerParams(dimension_semantics=sem, vmem_limit_bytes=vmem)


def _mod_kernel(c_ref, w_ref, b_ref, o_ref):
    c = c_ref[...]
    s = (c * jax.nn.sigmoid(c)).astype(_BF16)
    o_ref[...] = jnp.dot(s, w_ref[...].astype(_BF16), preferred_element_type=_F32) + b_ref[...]


def _modulation(cond, w_mod, b_mod, tn=1024):
    depth, d, n = w_mod.shape
    return pl.pallas_call(
        _mod_kernel,
        out_shape=jax.ShapeDtypeStruct((depth, MOD_ROWS, n), _F32),
        grid=(depth, n // tn),
        in_specs=[
            pl.BlockSpec((MOD_ROWS, d), lambda l, j: (0, 0)),
            pl.BlockSpec((None, d, tn), lambda l, j: (l, 0, j)),
            pl.BlockSpec((None, 1, tn), lambda l, j: (l, 0, j)),
        ],
        out_specs=pl.BlockSpec((None, MOD_ROWS, tn), lambda l, j: (l, 0, j)),
        compiler_params=_params("parallel", "parallel"),
        name="modulation",
    )(cond, w_mod, b_mod.reshape(depth, 1, n))


NORM_ROWS = 16


def _for_row_chunks(n_rows, fn, pieces=None):
    if pieces is not None:
        pieces.extend(functools.partial(fn, pl.ds(i * NORM_ROWS, NORM_ROWS)) for i in range(n_rows // NORM_ROWS))
        return

    def body(i, carry):
        fn(pl.ds(pl.multiple_of(i * NORM_ROWS, NORM_ROWS), NORM_ROWS))
        return carry

    lax.fori_loop(0, n_rows // NORM_ROWS, body, 0, unroll=2)


def _once(fn, pieces=None):
    fn() if pieces is None else pieces.append(fn)


def _row_rstd(x_ref, rstd_ref, pieces=None):
    n_tiles = x_ref.shape[1] // LANES

    def lane_partial_sums(rows):
        sq = jnp.square(x_ref[rows, :])
        part = sq[:, 0:LANES]
        for t in range(1, n_tiles):
            part = part + sq[:, t * LANES:(t + 1) * LANES]
        rstd_ref[rows, :] = part

    def reduce():
        ms = jnp.sum(rstd_ref[...], axis=-1, keepdims=True) * (1.0 / x_ref.shape[1])
        rstd_ref[...] = jnp.broadcast_to(lax.rsqrt(ms + NORM_EPS), rstd_ref.shape)

    _for_row_chunks(x_ref.shape[0], lane_partial_sums, pieces)
    _once(reduce, pieces)


def _modulated_norm_rows(x_ref, h_ref, rstd_ref, vec_ref, g_ref, mod_ref, mod_base, pieces=None):
    n_tiles = x_ref.shape[1] // LANES

    def vectors():
        sh = mod_ref[mod_base:mod_base + 1, :]
        sc = mod_ref[mod_base + 1:mod_base + 2, :]
        vec_ref[0] = jnp.broadcast_to(g_ref[...] * (1.0 + sc), vec_ref.shape[1:])
        vec_ref[1] = jnp.broadcast_to(sh, vec_ref.shape[1:])

    def normalise(rows):
        rstd = jnp.tile(rstd_ref[rows, :], (1, n_tiles))
        scale, shift = (jnp.tile(vec_ref[k], (NORM_ROWS // SUBLANES, 1)) for k in range(2))
        h_ref[rows, :] = (x_ref[rows, :] * rstd * scale + shift).astype(h_ref.dtype)

    _row_rstd(x_ref, rstd_ref, pieces)
    _once(vectors, pieces)
    _for_row_chunks(x_ref.shape[0], normalise, pieces)


def _gated_residual_rows(x_ref, acc_ref, o_ref, rstd_ref, vec_ref, g_ref, mod_ref, gate_row, gate_scale,
                         pieces=None):
    n_tiles = x_ref.shape[1] // LANES

    def vectors():
        gt = mod_ref[gate_row:gate_row + 1, :]
        if gate_scale != 1.0:
            gt = gate_scale * gt
        vec_ref[0] = jnp.broadcast_to(gt * g_ref[...], vec_ref.shape[1:])

    def residual(rows):
        rstd = jnp.tile(rstd_ref[rows, :], (1, n_tiles))
        scale = jnp.tile(vec_ref[0], (NORM_ROWS // SUBLANES, 1))
        o_ref[rows, :] = x_ref[rows, :] + acc_ref[rows, :] * rstd * scale

    _row_rstd(acc_ref, rstd_ref, pieces)
    _once(vectors, pieces)
    _for_row_chunks(x_ref.shape[0], residual, pieces)


def _on_part(r, n_first, refs_first, refs_second, fn):
    if all(a is b for a, b in zip(refs_first, refs_second)):
        fn(*refs_first)
        return
    pl.when(r < n_first)(lambda: fn(*refs_first))
    pl.when(r >= n_first)(lambda: fn(*refs_second))


def _ffn_kernel(*refs, mod_base, n_first, split_in, split_out):
    refs = list(refs)
    xa_ref = refs.pop(0)
    xb_ref = refs.pop(0) if split_in else xa_ref
    mod_ref, gpre_ref, gpost_ref, w1g_ref, w1u_ref, w2_ref, oa_ref = refs[:7]
    ob_ref = refs[7] if split_out else oa_ref
    h_ref, acc_ref, rstd_ref, vec_ref = refs[-4:]
    r, j = pl.program_id(0), pl.program_id(1)

    @pl.when(j == 0)
    def _():
        _on_part(r, n_first, (xa_ref,), (xb_ref,),
                 lambda x_ref: _modulated_norm_rows(x_ref, h_ref, rstd_ref, vec_ref, gpre_ref, mod_ref,
                                                    mod_base))
        acc_ref[...] = jnp.zeros_like(acc_ref)

    h = h_ref[...]
    g = jnp.dot(h, w1g_ref[...], preferred_element_type=_F32)
    u = jnp.dot(h, w1u_ref[...], preferred_element_type=_F32)
    act = (g * jax.nn.sigmoid(g) * u).astype(_BF16)
    acc_ref[...] += jnp.dot(act, w2_ref[...], preferred_element_type=_F32)

    @pl.when(j == pl.num_programs(1) - 1)
    def _():
        _on_part(r, n_first, (xa_ref, oa_ref), (xb_ref, ob_ref),
                 lambda x_ref, o_ref: _gated_residual_rows(x_ref, acc_ref, o_ref, rstd_ref, vec_ref, gpost_ref,
                                                           mod_ref, mod_base + 2, 0.5))


def _ffn(xs, mod, group_of_tile, gpre, gpost, w1, w2, l, i, mod_base, tiles, tf, n_prompt, split_out):
    d = xs[0].shape[1]
    n = sum(x.shape[0] for x in xs)
    split_in = len(xs) == 2
    tm = tiles[1] if (split_in or split_out) else tiles[0]
    n_first = n_prompt // tm
    nff = w2.shape[2] // tf

    def first(r, j):
        return jnp.minimum(r, n_first - 1), 0

    def second(r, j):
        return jnp.maximum(r - n_first, 0), 0

    def whole(r, j):
        return r, 0

    x_mode = dict(pipeline_mode=pl.Buffered(1)) if tm * d * 4 >= 8 * 1024 * 1024 else {}
    x_specs = ([pl.BlockSpec((tm, d), first, **x_mode), pl.BlockSpec((tm, d), second, **x_mode)] if split_in
               else [pl.BlockSpec((tm, d), whole, **x_mode)])
    if split_out:
        out_shape = (jax.ShapeDtypeStruct((n_first * tm, d), _F32), jax.ShapeDtypeStruct((n - n_first * tm, d), _F32))
        out_specs = (pl.BlockSpec((tm, d), first), pl.BlockSpec((tm, d), second))
    else:
        out_shape = jax.ShapeDtypeStruct((n, d), _F32)
        out_specs = pl.BlockSpec((tm, d), whole)
    return pl.pallas_call(
        functools.partial(_ffn_kernel, mod_base=mod_base, n_first=n_first, split_in=split_in, split_out=split_out),
        out_shape=out_shape,
        grid=(n // tm, nff),
        in_specs=x_specs + [
            pl.BlockSpec((None, N_MOD, d), lambda r, j: (group_of_tile(r, tm), 0, 0)),
            pl.BlockSpec((1, d), lambda r, j: (0, 0)),
            pl.BlockSpec((1, d), lambda r, j: (0, 0)),
            pl.BlockSpec((None, None, d, tf), lambda r, j: (l, i, 0, j)),
            pl.BlockSpec((None, None, d, tf), lambda r, j: (l, i, 0, j)),
            pl.BlockSpec((None, None, tf, d), lambda r, j: (l, i, j, 0)),
        ],
        out_specs=out_specs,
        scratch_shapes=[pltpu.VMEM((tm, d), _BF16), pltpu.VMEM((tm, d), _F32), pltpu.VMEM((tm, LANES), _F32),
                        pltpu.VMEM((2, SUBLANES, d), _F32)],
        compiler_params=_params("parallel", "arbitrary", vmem=VMEM_LIMIT_LARGE if x_mode else VMEM_LIMIT),
        name=f"ffn_l{l}_{i}",
    )(*xs, mod, gpre, gpost, *w1, w2)


def _ffn_overlap_kernel(x_ref, mod_ref, gpre_ref, gpost_ref, w1g_ref, w1u_ref, w2_ref, o_ref,
                        h_ref, h_next_ref, acc_ref, acc_done_ref, rstd_ref, vec_ref, *, mod_base, n_tiles):
    r, j = pl.program_id(0), pl.program_id(1)
    last = pl.num_programs(1) - 1

    def ffn_update(h):
        g = jnp.dot(h, w1g_ref[...], preferred_element_type=_F32)
        u = jnp.dot(h, w1u_ref[...], preferred_element_type=_F32)
        act = (g * jax.nn.sigmoid(g) * u).astype(_BF16)
        return jnp.dot(act, w2_ref[...], preferred_element_type=_F32)

    def run(pieces):
        for piece in pieces:
            piece()

    def finish_previous(pieces=None):
        _gated_residual_rows(x_ref, acc_done_ref, o_ref, rstd_ref, vec_ref, gpost_ref, mod_ref, mod_base + 2, 0.5,
                             pieces)

    def prepare_next(pieces=None):
        _modulated_norm_rows(x_ref, h_next_ref, rstd_ref, vec_ref, gpre_ref, mod_ref, mod_base, pieces)

    @pl.when((r == 0) & (j == 0))
    def _():
        prepare_next()
        acc_done_ref[...] = jnp.zeros_like(acc_done_ref)

    @pl.when(r < n_tiles)
    def _():
        @pl.when(j == 0)
        def _():
            h = h_next_ref[...]
            acc_ref[...] = ffn_update(h)
            h_ref[...] = h
            under_matmul = []
            finish_previous(under_matmul)
            run(under_matmul)

        @pl.when((j > 0) & (j < last))
        def _():
            acc_ref[...] += ffn_update(h_ref[...])

        @pl.when(j == last)
        def _():
            acc_done_ref[...] = acc_ref[...] + ffn_update(h_ref[...])
            under_matmul = []
            prepare_next(under_matmul)
            run(under_matmul)

    @pl.when((r == n_tiles) & (j == 0))
    def _():
        finish_previous()


def _ffn_overlapped(x, mod, group_of_tile, gpre, gpost, w1, w2, l, i, mod_base, tm, tf):
    n, d = x.shape
    n_tiles = n // tm
    nff = w2.shape[2] // tf

    def neighbour(r, j):
        return jnp.where(j == nff - 1, jnp.minimum(r + 1, n_tiles - 1), jnp.maximum(r - 1, 0))

    def weight_tile(r, j):
        return jnp.where(r < n_tiles, j, nff - 1)

    return pl.pallas_call(
        functools.partial(_ffn_overlap_kernel, mod_base=mod_base, n_tiles=n_tiles),
        out_shape=jax.ShapeDtypeStruct((n, d), _F32),
        grid=(n_tiles + 1, nff),
        in_specs=[
            pl.BlockSpec((tm, d), lambda r, j: (neighbour(r, j), 0)),
            pl.BlockSpec((None, N_MOD, d), lambda r, j: (group_of_tile(neighbour(r, j), tm), 0, 0)),
            pl.BlockSpec((1, d), lambda r, j: (0, 0)),
            pl.BlockSpec((1, d), lambda r, j: (0, 0)),
            pl.BlockSpec((None, None, d, tf), lambda r, j: (l, i, 0, weight_tile(r, j))),
            pl.BlockSpec((None, None, d, tf), lambda r, j: (l, i, 0, weight_tile(r, j))),
            pl.BlockSpec((None, None, tf, d), lambda r, j: (l, i, weight_tile(r, j), 0)),
        ],
        out_specs=pl.BlockSpec((tm, d), lambda r, j: (jnp.maximum(r - 1, 0), 0)),
        scratch_shapes=[pltpu.VMEM((tm, d), _BF16)] * 2 + [pltpu.VMEM((tm, d), _F32)] * 2
        + [pltpu.VMEM((tm, LANES), _F32), pltpu.VMEM((2, SUBLANES, d), _F32)],
        compiler_params=_params("arbitrary", "arbitrary"),
        name=f"ffn_l{l}_{i}",
    )(x, mod, gpre, gpost, *w1, w2)


def _inproj_kernel(x_ref, mod_ref, gpre_ref, w_ref, o_ref, h_ref, rstd_ref, vec_ref, *, mod_base):
    @pl.when(pl.program_id(1) == 0)
    def _():
        _modulated_norm_rows(x_ref, h_ref, rstd_ref, vec_ref, gpre_ref, mod_ref, mod_base)

    o_ref[...] = jnp.dot(h_ref[...], w_ref[...], preferred_element_type=_F32)


def _inproj(x, mod, group_of_tile, gpre, w_in, l, mod_base, tm, tn):
    n, d = x.shape
    cols = w_in.shape[2]
    return pl.pallas_call(
        functools.partial(_inproj_kernel, mod_base=mod_base),
        out_shape=jax.ShapeDtypeStruct((n, cols), _F32),
        grid=(n // tm, cols // tn),
        in_specs=[
            pl.BlockSpec((tm, d), lambda r, j: (r, 0)),
            pl.BlockSpec((None, N_MOD, d), lambda r, j: (group_of_tile(r, tm), 0, 0)),
            pl.BlockSpec((1, d), lambda r, j: (0, 0)),
            pl.BlockSpec((None, d, tn), lambda r, j: (l, 0, j)),
        ],
        out_specs=pl.BlockSpec((tm, tn), lambda r, j: (r, j)),
        scratch_shapes=[pltpu.VMEM((tm, d), _BF16), pltpu.VMEM((tm, LANES), _F32),
                        pltpu.VMEM((2, SUBLANES, d), _F32)],
        compiler_params=_params("parallel", "arbitrary"),
        name=f"inproj_l{l}",
    )(x, mod, gpre, w_in)


LRU_TILES = SUBLANES // 2
CONV_HALO = SUBLANES
XP_PITCH = 2


def _sigmoid(x):
    return 0.5 * jnp.tanh(0.5 * x) + 0.5


def _lru_kernel(*refs, seq, row_chunk, unroll, aliased):
    if aliased:
        refs = refs[:8] + refs[9:]
    (xr_ref, gr_ref, cw_ref, cb_ref, wg_ref, bg_ref, lam_ref, h0_ref, rec_ref, st_ref,
     xp_ref, a_ref, b_ref, h_ref) = refs
    nt = LRU_TILES
    halo = jnp.zeros((XP_PITCH * CONV_HALO, LANES), _F32)
    for n in range(nt):
        pair, member = divmod(n, XP_PITCH)
        xp_ref[pair, 0:XP_PITCH * CONV_HALO, :] = halo
        xp_ref[pair, XP_PITCH * (CONV_HALO + seq):XP_PITCH * (2 * CONV_HALO + seq), :] = halo
        xp_ref[pair, pl.ds(XP_PITCH * CONV_HALO + member, seq, stride=XP_PITCH), :] = (
            xr_ref[:, n * LANES:(n + 1) * LANES])

    neg_log_a_scale = (0.5 * LRU_C) * jax.nn.softplus(-lam_ref[...])
    exp2_scale = -math.log2(math.e) * neg_log_a_scale

    def coeffs(c, carry):
        r0 = c * row_chunk
        for n in range(nt):
            cs = slice(n * LANES, (n + 1) * LANES)
            pair, member = divmod(n, XP_PITCH)
            y = cb_ref[:, cs]
            for tap in range(cw_ref.shape[0]):
                first = XP_PITCH * (r0 + CONV_HALO + tap - CONV_PAD_LEFT) + member
                y = y + cw_ref[tap:tap + 1, cs] * xp_ref[pair, pl.ds(first, row_chunk, stride=XP_PITCH), :]
            g4 = jnp.dot(y.astype(_BF16), wg_ref[n], preferred_element_type=_F32)
            half_y = 0.5 * y
            for direction in range(2):
                ua = 1.0 + jnp.tanh(g4[:, (2 * direction) * LANES:(2 * direction + 1) * LANES]
                                    + bg_ref[2 * direction:2 * direction + 1, cs])
                ux = 1.0 + jnp.tanh(g4[:, (2 * direction + 1) * LANES:(2 * direction + 2) * LANES]
                                    + bg_ref[2 * direction + 1:2 * direction + 2, cs])
                a = jnp.exp2(ua * exp2_scale[direction:direction + 1, cs])
                one_minus_a2 = jnp.tanh(ua * neg_log_a_scale[direction:direction + 1, cs]) * (1.0 + a * a)
                mult = jnp.where(one_minus_a2 > 0.0, one_minus_a2 * lax.rsqrt(one_minus_a2), 0.0)
                dense_rows = pl.ds(r0 * SUBLANES + nt * direction + n, row_chunk, stride=SUBLANES)
                a_ref[dense_rows, :] = a
                b_ref[dense_rows, :] = mult * ux * half_y
        return carry

    for c in range(seq // row_chunk):
        coeffs(c, 0)

    is_fwd = lax.broadcasted_iota(jnp.int32, (SUBLANES, LANES), 0) < nt

    block = unroll * SUBLANES

    def scan(c, h):
        rf = pl.multiple_of(c * block, block)
        rb = pl.multiple_of((seq - unroll) * SUBLANES - c * block, block)
        a_f, a_b = a_ref[pl.ds(rf, block), :], a_ref[pl.ds(rb, block), :]
        b_f, b_b = b_ref[pl.ds(rf, block), :], b_ref[pl.ds(rb, block), :]
        def coeff(r):
            up = slice(r * SUBLANES, (r + 1) * SUBLANES)
            down = slice((unroll - 1 - r) * SUBLANES, (unroll - r) * SUBLANES)
            return jnp.where(is_fwd, a_f[up], a_b[down]), jnp.where(is_fwd, b_f[up], b_b[down])

        def emit(r, h_r):
            h_ref[pl.ds(rf + r * SUBLANES, nt), :] = h_r[0:nt]
            h_ref[pl.ds(rb + (unroll - 1 - r) * SUBLANES + nt, nt), :] = h_r[nt:2 * nt]

        for r in range(0, unroll, 2):
            (a0, b0), (a1, b1) = coeff(r), coeff(r + 1)
            emit(r, a0 * h + b0)
            h = (a1 * a0) * h + (a1 * b0 + b1)
            emit(r + 1, h)
        return h

    h0 = jnp.concatenate([h0_ref[d:d + 1, n * LANES:(n + 1) * LANES] for d in range(2) for n in range(nt)], axis=0)
    h = lax.fori_loop(0, seq // unroll, scan, h0)
    for n in range(nt):
        st_ref[0:1, n * LANES:(n + 1) * LANES] = h[n:n + 1]
        st_ref[1:2, n * LANES:(n + 1) * LANES] = h[nt + n:nt + n + 1]

    def gate(c, carry):
        r0 = pl.multiple_of(c * row_chunk, row_chunk)
        for n in range(nt):
            cs = slice(n * LANES, (n + 1) * LANES)
            hf = h_ref[pl.ds(r0 * SUBLANES + n, row_chunk, stride=SUBLANES), :]
            hb = h_ref[pl.ds(r0 * SUBLANES + nt + n, row_chunk, stride=SUBLANES), :]
            rec_ref[pl.ds(r0, row_chunk), cs] = (
                (hf + hb) * jax.nn.gelu(gr_ref[pl.ds(r0, row_chunk), cs])).astype(rec_ref.dtype)
        return carry

    lax.fori_loop(0, seq // row_chunk, gate, 0)


def _lru(proj, rec, conv_w, conv_b, wg, bg, lam, h0, l, first_row, seq, d_rnn):
    n_seq = h0.shape[0]
    rb0 = first_row // seq
    cg = LRU_TILES * LANES
    ncg = d_rnn // cg
    row_chunk = min(seq, 256)
    aliased = rec is not None
    in_specs = [
        pl.BlockSpec((seq, cg), lambda s, c: (rb0 + s, c)),
        pl.BlockSpec((seq, cg), lambda s, c: (rb0 + s, ncg + c)),
        pl.BlockSpec((None, conv_w.shape[1], cg), lambda s, c: (l, 0, c)),
        pl.BlockSpec((None, 1, cg), lambda s, c: (l, 0, c)),
        pl.BlockSpec((None, LRU_TILES, LANES, 4 * LANES), lambda s, c: (l, c, 0, 0)),
        pl.BlockSpec((None, 4, cg), lambda s, c: (l, 0, c)),
        pl.BlockSpec((None, 2, cg), lambda s, c: (l, 0, c)),
        pl.BlockSpec((None, 2, cg), lambda s, c: (s, 0, c)),
    ]
    args = [proj, proj, conv_w, conv_b, wg, bg, lam, h0]
    if aliased:
        in_specs.append(pl.BlockSpec(memory_space=pl.ANY))
        args.append(rec)
    return pl.pallas_call(
        functools.partial(_lru_kernel, seq=seq, row_chunk=row_chunk, unroll=8, aliased=aliased),
        out_shape=(jax.ShapeDtypeStruct((proj.shape[0], d_rnn), _BF16),
                   jax.ShapeDtypeStruct((n_seq, 2, d_rnn), _F32)),
        grid=(n_seq, ncg),
        in_specs=in_specs,
        out_specs=(pl.BlockSpec((seq, cg), lambda s, c: (rb0 + s, c)),
                   pl.BlockSpec((None, 2, cg), lambda s, c: (s, 0, c))),
        scratch_shapes=[pltpu.VMEM((LRU_TILES // XP_PITCH, XP_PITCH * (seq + 2 * CONV_HALO), LANES), _F32)]
        + [pltpu.VMEM((seq * SUBLANES, LANES), _F32)] * 3,
        input_output_aliases={8: 0} if aliased else {},
        compiler_params=_params("parallel", "parallel", vmem=VMEM_LIMIT_LARGE),
        name=f"rglru_l{l}_t{seq}",
    )(*args)


Q_SCALE = QK_DIM ** -0.5 * math.log2(math.e)


def _rope(x, cos, sin_signed):
    lane = lax.broadcasted_iota(jnp.int32, x.shape, 1)
    half = QK_DIM // 4
    partner = jnp.where((lane & half) == 0, pltpu.roll(x, LANES - half, 1), pltpu.roll(x, half, 1))
    return x * cos + partner * sin_signed


def _lambda(lam_ref, lam_init):
    lv = lam_ref[...]
    return (jnp.exp(jnp.sum(lv[0:1] * lv[1:2], axis=-1, keepdims=True))
            - jnp.exp(jnp.sum(lv[2:3] * lv[3:4], axis=-1, keepdims=True)) + lam_init)


def _scores(q, kb):
    lane = lax.broadcasted_iota(jnp.int32, q.shape, 1)
    qm = jnp.concatenate([jnp.where(lane < QK_DIM, q, 0.0), jnp.where(lane >= QK_DIM, q, 0.0)], axis=0)
    return lax.dot_general(qm.astype(_BF16), kb, (((1,), (1,)), ((), ())), preferred_element_type=_F32)


def _softmax_diff_pv(s, vb1, lam, g, lam_init):
    tq = s.shape[0] // 2
    e = jnp.exp2(s - jnp.max(s, axis=-1, keepdims=True)).astype(_BF16)
    ov = jnp.dot(e, vb1, preferred_element_type=_F32)
    o = (ov[0:tq, 0:LANES] / ov[0:tq, LANES:2 * LANES]
         - lam * (ov[tq:2 * tq, 0:LANES] / ov[tq:2 * tq, LANES:2 * LANES]))
    return _rms(o, g) * (1.0 - lam_init)


def _attn_ctx_kernel(q_ref, k_ref, v_ref, ck_ref, cv_ref, cq_ref, sq_ref, ckk_ref, skk_ref, lam_ref, g_ref,
                     att_in_ref, o_ref, kbuf, vbuf, *, past, seq, sub, lam_init):
    del att_in_ref

    @pl.when(pl.program_id(2) == 0)
    def _():
        kbuf[0:past, :] = ck_ref[...].astype(kbuf.dtype)
        kbuf[past:past + seq, :] = _rope(k_ref[...], ckk_ref[...], skk_ref[...]).astype(kbuf.dtype)
        vbuf[0:past, 0:LANES] = cv_ref[...].astype(vbuf.dtype)
        vbuf[past:past + seq, 0:LANES] = v_ref[...].astype(vbuf.dtype)
        vbuf[:, LANES:2 * LANES] = jnp.ones((past + seq, LANES), vbuf.dtype)

    lam = _lambda(lam_ref, lam_init)

    def scores(i):
        rows = slice(i * sub, (i + 1) * sub)
        return _scores(_rope(q_ref[rows, :], cq_ref[rows, :], sq_ref[rows, :]) * Q_SCALE, kbuf[...])

    s = scores(0)
    for i in range(q_ref.shape[0] // sub):
        s_next = scores(i + 1) if (i + 1) * sub < q_ref.shape[0] else None
        o = _softmax_diff_pv(s, vbuf[...], lam, g_ref[...], lam_init)
        o_ref[i * sub:(i + 1) * sub, :] = o.astype(o_ref.dtype)
        s = s_next


def _attn_prompt_kernel(*refs, heads, lam_init, aliased):
    if aliased:
        refs = refs[:5] + refs[7:]
    q_ref, k_ref, v_ref, lam_ref, g_ref, o_ref, nk_ref, nv_ref = refs
    nk_ref[...] = k_ref[...]
    nv_ref[...] = v_ref[...]
    lam = _lambda(lam_ref, lam_init)
    ones = jnp.ones((k_ref.shape[0], LANES), _BF16)
    for h in range(heads):
        hs = slice(h * LANES, (h + 1) * LANES)
        vb1 = jnp.concatenate([v_ref[:, hs].astype(_BF16), ones], axis=1)
        s = _scores(q_ref[:, hs] * Q_SCALE, k_ref[:, hs].astype(_BF16))
        o = _softmax_diff_pv(s, vb1, lam, g_ref[...], lam_init)
        o_ref[:, hs] = o.astype(o_ref.dtype)


def _attention_prompt(proj, lamv, gsub, caches, l, depth, n_seq, seq, heads, col_q, lam_init):
    width = heads * LANES
    cq = col_q // width
    aliased = caches is not None
    in_specs = [
        pl.BlockSpec((seq, width), lambda b: (b, cq)),
        pl.BlockSpec((seq, width), lambda b: (b, cq + 1)),
        pl.BlockSpec((seq, width), lambda b: (b, cq + 2)),
        pl.BlockSpec((None, 4, QK_DIM), lambda b: (l, 0, 0)),
        pl.BlockSpec((None, 1, LANES), lambda b: (l, 0, 0)),
    ]
    args = [proj, proj, proj, lamv, gsub]
    if aliased:
        in_specs += [pl.BlockSpec(memory_space=pl.ANY)] * 2
        args += list(caches)
    cache_sds = jax.ShapeDtypeStruct((n_seq, depth, seq, width), _F32)
    return pl.pallas_call(
        functools.partial(_attn_prompt_kernel, heads=heads, lam_init=lam_init, aliased=aliased),
        out_shape=(jax.ShapeDtypeStruct((proj.shape[0], width), _BF16), cache_sds, cache_sds),
        grid=(n_seq,),
        in_specs=in_specs,
        out_specs=(pl.BlockSpec((seq, width), lambda b: (b, 0)),
                   pl.BlockSpec((None, None, seq, width), lambda b: (b, l, 0, 0)),
                   pl.BlockSpec((None, None, seq, width), lambda b: (b, l, 0, 0))),
        input_output_aliases={5: 1, 6: 2} if aliased else {},
        compiler_params=_params("parallel"),
        name=f"diffattn_prompt_l{l}",
    )(*args)


def _attention_ctx(proj, att, lamv, gsub, cache_k, cache_v, cos, sin, l, first_row, n_seq, seq, tq, heads,
                   col_q, lam_init):
    past = cache_k.shape[2]
    rq0, rk0 = first_row // tq, first_row // seq
    nq = seq // tq
    hq, hk, hv = col_q // LANES, col_q // LANES + heads, col_q // LANES + 2 * heads
    in_specs = [
        pl.BlockSpec((tq, LANES), lambda b, h, t: (rq0 + b * nq + t, hq + h)),
        pl.BlockSpec((seq, LANES), lambda b, h, t: (rk0 + b, hk + h)),
        pl.BlockSpec((seq, LANES), lambda b, h, t: (rk0 + b, hv + h)),
        pl.BlockSpec((None, None, past, LANES), lambda b, h, t: (b, l, 0, h)),
        pl.BlockSpec((None, None, past, LANES), lambda b, h, t: (b, l, 0, h)),
        pl.BlockSpec((tq, LANES), lambda b, h, t: (t, 0)),
        pl.BlockSpec((tq, LANES), lambda b, h, t: (t, 0)),
        pl.BlockSpec((seq, LANES), lambda b, h, t: (0, 0)),
        pl.BlockSpec((seq, LANES), lambda b, h, t: (0, 0)),
        pl.BlockSpec((None, 4, QK_DIM), lambda b, h, t: (l, 0, 0)),
        pl.BlockSpec((None, 1, LANES), lambda b, h, t: (l, 0, 0)),
        pl.BlockSpec(memory_space=pl.ANY),
    ]
    return pl.pallas_call(
        functools.partial(_attn_ctx_kernel, past=past, seq=seq, sub=min(tq, 256), lam_init=lam_init),
        out_shape=jax.ShapeDtypeStruct(att.shape, att.dtype),
        grid=(n_seq, heads, nq),
        in_specs=in_specs,
        out_specs=pl.BlockSpec((tq, LANES), lambda b, h, t: (rq0 + b * nq + t, h)),
        scratch_shapes=[pltpu.VMEM((past + seq, LANES), _BF16), pltpu.VMEM((past + seq, 2 * LANES), _BF16)],
        input_output_aliases={11: 0},
        compiler_params=_params("parallel", "parallel", "arbitrary"),
        name=f"diffattn_ctx_l{l}",
    )(proj, proj, proj, cache_k, cache_v, cos, sin, cos, sin, lamv, gsub, att)


def _merge_kernel(x_ref, mod_ref, gpost_ref, rec_ref, att_ref, gm1_ref, gm2_ref, plru_ref, pattn_ref,
                  wout_ref, o_ref, acc_ref, rstd_ref, vec_ref, *, mod_base):
    j = pl.program_id(1)

    @pl.when(j == 0)
    def _():
        acc_ref[...] = jnp.zeros_like(acc_ref)

    a = jnp.dot(rec_ref[...], plru_ref[...], preferred_element_type=_F32)
    b = jnp.dot(att_ref[...], pattn_ref[...], preferred_element_type=_F32)
    merged = _sigmoid(gm1_ref[...]) * a + _sigmoid(gm2_ref[...]) * b
    acc_ref[...] += jnp.dot(merged.astype(_BF16), wout_ref[...], preferred_element_type=_F32)

    @pl.when(j == pl.num_programs(1) - 1)
    def _():
        _gated_residual_rows(x_ref, acc_ref, o_ref, rstd_ref, vec_ref, gpost_ref, mod_ref, mod_base + 2, 1.0)


def _merge(x, mod, group_of_tile, gpost, rec, att, proj, col_gm, p_lru, p_attn, w_out, l, mod_base, tm, tn):
    n, d = x.shape
    d_rnn, v_cols = rec.shape[1], att.shape[1]
    g1, g2 = col_gm // tn, (col_gm + d) // tn
    return pl.pallas_call(
        functools.partial(_merge_kernel, mod_base=mod_base),
        out_shape=jax.ShapeDtypeStruct((n, d), _F32),
        grid=(n // tm, d // tn),
        in_specs=[
            pl.BlockSpec((tm, d), lambda r, j: (r, 0)),
            pl.BlockSpec((None, N_MOD, d), lambda r, j: (group_of_tile(r, tm), 0, 0)),
            pl.BlockSpec((1, d), lambda r, j: (0, 0)),
            pl.BlockSpec((tm, d_rnn), lambda r, j: (r, 0)),
            pl.BlockSpec((tm, v_cols), lambda r, j: (r, 0)),
            pl.BlockSpec((tm, tn), lambda r, j: (r, g1 + j)),
            pl.BlockSpec((tm, tn), lambda r, j: (r, g2 + j)),
            pl.BlockSpec((None, d_rnn, tn), lambda r, j: (l, 0, j)),
            pl.BlockSpec((None, v_cols, tn), lambda r, j: (l, 0, j)),
            pl.BlockSpec((None, tn, d), lambda r, j: (l, j, 0)),
        ],
        out_specs=pl.BlockSpec((tm, d), lambda r, j: (r, 0)),
        scratch_shapes=[pltpu.VMEM((tm, d), _F32), pltpu.VMEM((tm, LANES), _F32),
                        pltpu.VMEM((2, SUBLANES, d), _F32)],
        compiler_params=_params("parallel", "arbitrary"),
        name=f"merge_l{l}",
    )(x, mod, gpost, rec, att, proj, proj, p_lru, p_attn, w_out)


def _rope_tables(n_tokens):
    t = jnp.arange(n_tokens)
    row = (t // GRID_W).astype(_F32)
    col = (t % GRID_W).astype(_F32)
    n_freq = QK_DIM // 4
    freqs = 1.0 / (ROPE_BASE ** (jnp.arange(0, 2 * n_freq, 2, dtype=_F32) / (2 * n_freq)))
    ar, ac = row[:, None] * freqs, col[:, None] * freqs
    cos = jnp.concatenate([jnp.cos(ar), jnp.cos(ar), jnp.cos(ac), jnp.cos(ac)], axis=-1)
    sin = jnp.concatenate([-jnp.sin(ar), jnp.sin(ar), -jnp.sin(ac), jnp.sin(ac)], axis=-1)
    reps = LANES // QK_DIM
    return jnp.tile(cos, (1, reps)), jnp.tile(sin, (1, reps))


def _largest_tile(candidates, *extents):
    return next(t for t in candidates if all(e % t == 0 for e in extents))


def kernel(x_prompt, x_sample, c, cache_k, cache_v, state_lru, c_ctx, w_mod, b_mod, g_pre, g_post, ffn_w1, ffn_w2, w_in, conv_w, conv_b, lru_wa, lru_ba, lru_wx, lru_bx, lru_lambda, lam_q1, lam_k1, lam_q2, lam_k2, attn_subln, p_lru, p_attn, w_out):
    batch, seq, d = x_prompt.shape
    dec_batch, dec_seq, _ = x_sample.shape
    depth = w_mod.shape[0]
    d_rnn = conv_w.shape[2]
    heads = cache_k.shape[3]
    past = cache_k.shape[2]
    v_cols = heads * cache_v.shape[4]
    qk_cols = heads * cache_k.shape[4]
    d_ff = ffn_w2.shape[2]
    n_prompt, n_sample = batch * seq, dec_batch * dec_seq
    assert cache_k.shape[4] == LANES and cache_v.shape[4] == LANES and 1 + dec_batch <= MOD_ROWS
    col_q = 2 * d_rnn
    col_gm = col_q + 2 * qk_cols + v_cols

    tm = _largest_tile((512, 256), n_prompt, dec_seq)
    tm_in = _largest_tile((1024, 512, 256), n_prompt, dec_seq)
    tq = _largest_tile((2048, 1024, 512, 256, 128), dec_seq)
    tf, tn_in, tn_merge = 512, 1024, 512

    def group_of_tile(r, rows):
        first = n_prompt // rows
        return jnp.where(r < first, 0, 1 + (r - first) // (dec_seq // rows))

    ffp = -(-d_ff // tf) * tf
    padc = ((0, 0), (0, 0), (0, 0), (0, ffp - d_ff))
    w1 = (jnp.pad(ffn_w1[..., :d_ff], padc).astype(_BF16), jnp.pad(ffn_w1[..., d_ff:], padc).astype(_BF16))
    w2 = jnp.pad(ffn_w2, ((0, 0), (0, 0), (0, ffp - d_ff), (0, 0))).astype(_BF16)
    w_in_b, p_lru_b, p_attn_b, w_out_b = (w.astype(_BF16) for w in (w_in, p_lru, p_attn, w_out))
    wg = (0.5 * jnp.concatenate([lru_wa[:, 0], lru_wx[:, 0], lru_wa[:, 1], lru_wx[:, 1]], axis=-1)).astype(_BF16)
    bg = 0.5 * jnp.stack([lru_ba[:, 0], lru_bx[:, 0], lru_ba[:, 1], lru_bx[:, 1]], axis=1)
    lamv = jnp.stack([lam_q1, lam_k1, lam_q2, lam_k2], axis=1)
    gsub = attn_subln.reshape(depth, 1, -1)
    conv_b3 = conv_b.reshape(depth, 1, d_rnn)
    cache_k2 = cache_k.reshape(dec_batch, depth, past, qk_cols)
    cache_v2 = cache_v.reshape(dec_batch, depth, past, v_cols)
    cos, sin = _rope_tables(dec_seq)
    zero_state = jnp.zeros((batch, 2, d_rnn), _F32)

    cond = jnp.concatenate([c_ctx[None], c, jnp.zeros((MOD_ROWS - 1 - dec_batch, d), _F32)], axis=0)
    mod_all = _modulation(cond, w_mod, b_mod).reshape(depth, MOD_ROWS, N_MOD, d)

    xs = (x_prompt.reshape(n_prompt, d), x_sample.reshape(n_sample, d))
    caches, new_s = None, []
    for l in range(depth):
        mod = mod_all[l]
        gpre = [g_pre[l, i].reshape(1, d) for i in range(3)]
        gpost = [g_post[l, i].reshape(1, d) for i in range(3)]
        lam_init = 0.8 - 0.6 * math.exp(-0.3 * l)

        if len(xs) == 2:
            x = _ffn(xs, mod, group_of_tile, gpre[0], gpost[0], w1, w2, l, 0, 0, (tm, tm), tf, n_prompt, False)
        else:
            x = _ffn_overlapped(xs[0], mod, group_of_tile, gpre[0], gpost[0], w1, w2, l, 0, 0, tm, tf)
        proj = _inproj(x, mod, group_of_tile, gpre[1], w_in_b, l, 3, tm_in, tn_in)
        rec, st_p = _lru(proj, None, conv_w, conv_b3, wg, bg, lru_lambda, zero_state, l, 0, seq, d_rnn)
        rec, _ = _lru(proj, rec, conv_w, conv_b3, wg, bg, lru_lambda, state_lru[:, l], l, n_prompt, dec_seq, d_rnn)
        att, *caches = _attention_prompt(proj, lamv, gsub, caches, l, depth, batch, seq, heads, col_q, lam_init)
        att = _attention_ctx(proj, att, lamv, gsub, cache_k2, cache_v2, cos, sin, l, n_prompt, dec_batch, dec_seq,
                             tq, heads, col_q, lam_init)
        x = _merge(x, mod, group_of_tile, gpost[1], rec, att, proj, col_gm, p_lru_b, p_attn_b, w_out_b,
                   l, 3, tm, tn_merge)
        last = l == depth - 1
        if last:
            xs = _ffn((x,), mod, group_of_tile, gpre[2], gpost[2], w1, w2, l, 1, 6, (tm, tm), tf, n_prompt, True)
        else:
            xs = (_ffn_overlapped(x, mod, group_of_tile, gpre[2], gpost[2], w1, w2, l, 1, 6, tm, tf),)
        new_s.append(st_p)

    y_prompt = xs[0].reshape(batch, seq, d)
    y_sample = xs[1].reshape(dec_batch, dec_seq, d)
    new_k, new_v = (a.reshape(batch, depth, seq, heads, -1) for a in caches)
    return (y_prompt, y_sample, new_k, new_v, jnp.stack(new_s, axis=1))
```

```python
import functools
import math

import jax
import jax.numpy as jnp
from jax import lax
from jax.experimental import pallas as pl
from jax.experimental.pallas import tpu as pltpu

NORM_EPS = 1e-6
LRU_C = 8.0
N_MOD = 9
GRID_W = 64
ROPE_BASE = 10000.0
QK_DIM = 64
CONV_PAD_LEFT = 2

LANES = 128
SUBLANES = 8
MOD_ROWS = 16
VMEM_LIMIT = 52 * 1024 * 1024
VMEM_LIMIT_LARGE = 58 * 1024 * 1024

_BF16 = jnp.bfloat16
_F32 = jnp.float32


def _rms(x, g):
    ms = jnp.mean(x * x, axis=-1, keepdims=True)
    return x * lax.rsqrt(ms + NORM_EPS) * g


def _params(*sem, vmem=VMEM_LIMIT):
    return pltpu.CompilerParams(dimension_semantics=sem, vmem_limit_bytes=vmem)


def _mod_kernel(c_ref, w_ref, b_ref, o_ref):
    c = c_ref[...]
    s = (c * jax.nn.sigmoid(c)).astype(_BF16)
    o_ref[...] = jnp.dot(s, w_ref[...].astype(_BF16), preferred_element_type=_F32) + b_ref[...]


def _modulation(cond, w_mod, b_mod, tn=1024):
    depth, d, n = w_mod.shape
    return pl.pallas_call(
        _mod_kernel,
        out_shape=jax.ShapeDtypeStruct((depth, MOD_ROWS, n), _F32),
        grid=(depth, n // tn),
        in_specs=[
            pl.BlockSpec((MOD_ROWS, d), lambda l, j: (0, 0)),
            pl.BlockSpec((None, d, tn), lambda l, j: (l, 0, j)),
            pl.BlockSpec((None, 1, tn), lambda l, j: (l, 0, j)),
        ],
        out_specs=pl.BlockSpec((None, MOD_ROWS, tn), lambda l, j: (l, 0, j)),
        compiler_params=_params("parallel", "parallel"),
        name="modulation",
    )(cond, w_mod, b_mod.reshape(depth, 1, n))


NORM_ROWS = 16


def _for_row_chunks(n_rows, fn, pieces=None):
    if pieces is not None:
        pieces.extend(functools.partial(fn, pl.ds(i * NORM_ROWS, NORM_ROWS)) for i in range(n_rows // NORM_ROWS))
        return

    def body(i, carry):
        fn(pl.ds(pl.multiple_of(i * NORM_ROWS, NORM_ROWS), NORM_ROWS))
        return carry

    lax.fori_loop(0, n_rows // NORM_ROWS, body, 0, unroll=2)


def _once(fn, pieces=None):
    fn() if pieces is None else pieces.append(fn)


def _row_rstd(x_ref, rstd_ref, pieces=None):
    n_tiles = x_ref.shape[1] // LANES

    def lane_partial_sums(rows):
        sq = jnp.square(x_ref[rows, :])
        part = sq[:, 0:LANES]
        for t in range(1, n_tiles):
            part = part + sq[:, t * LANES:(t + 1) * LANES]
        rstd_ref[rows, :] = part

    def reduce():
        ms = jnp.sum(rstd_ref[...], axis=-1, keepdims=True) * (1.0 / x_ref.shape[1])
        rstd_ref[...] = jnp.broadcast_to(lax.rsqrt(ms + NORM_EPS), rstd_ref.shape)

    _for_row_chunks(x_ref.shape[0], lane_partial_sums, pieces)
    _once(reduce, pieces)


def _modulated_norm_rows(x_ref, h_ref, rstd_ref, vec_ref, g_ref, mod_ref, mod_base, pieces=None):
    n_tiles = x_ref.shape[1] // LANES

    def vectors():
        sh = mod_ref[mod_base:mod_base + 1, :]
        sc = mod_ref[mod_base + 1:mod_base + 2, :]
        vec_ref[0] = jnp.broadcast_to(g_ref[...] * (1.0 + sc), vec_ref.shape[1:])
        vec_ref[1] = jnp.broadcast_to(sh, vec_ref.shape[1:])

    def normalise(rows):
        rstd = jnp.tile(rstd_ref[rows, :], (1, n_tiles))
        scale, shift = (jnp.tile(vec_ref[k], (NORM_ROWS // SUBLANES, 1)) for k in range(2))
        h_ref[rows, :] = (x_ref[rows, :] * rstd * scale + shift).astype(h_ref.dtype)

    _row_rstd(x_ref, rstd_ref, pieces)
    _once(vectors, pieces)
    _for_row_chunks(x_ref.shape[0], normalise, pieces)


def _gated_residual_rows(x_ref, acc_ref, o_ref, rstd_ref, vec_ref, g_ref, mod_ref, gate_row, gate_scale,
                         pieces=None):
    n_tiles = x_ref.shape[1] // LANES

    def vectors():
        gt = mod_ref[gate_row:gate_row + 1, :]
        if gate_scale != 1.0:
            gt = gate_scale * gt
        vec_ref[0] = jnp.broadcast_to(gt * g_ref[...], vec_ref.shape[1:])

    def residual(rows):
        rstd = jnp.tile(rstd_ref[rows, :], (1, n_tiles))
        scale = jnp.tile(vec_ref[0], (NORM_ROWS // SUBLANES, 1))
        o_ref[rows, :] = x_ref[rows, :] + acc_ref[rows, :] * rstd * scale

    _row_rstd(acc_ref, rstd_ref, pieces)
    _once(vectors, pieces)
    _for_row_chunks(x_ref.shape[0], residual, pieces)


def _on_part(r, n_first, refs_first, refs_second, fn):
    if all(a is b for a, b in zip(refs_first, refs_second)):
        fn(*refs_first)
        return
    pl.when(r < n_first)(lambda: fn(*refs_first))
    pl.when(r >= n_first)(lambda: fn(*refs_second))


def _ffn_kernel(*refs, mod_base, n_first, split_in, split_out):
    refs = list(refs)
    xa_ref = refs.pop(0)
    xb_ref = refs.pop(0) if split_in else xa_ref
    mod_ref, gpre_ref, gpost_ref, w1g_ref, w1u_ref, w2_ref, oa_ref = refs[:7]
    ob_ref = refs[7] if split_out else oa_ref
    h_ref, acc_ref, rstd_ref, vec_ref = refs[-4:]
    r, j = pl.program_id(0), pl.program_id(1)

    @pl.when(j == 0)
    def _():
        _on_part(r, n_first, (xa_ref,), (xb_ref,),
                 lambda x_ref: _modulated_norm_rows(x_ref, h_ref, rstd_ref, vec_ref, gpre_ref, mod_ref,
                                                    mod_base))
        acc_ref[...] = jnp.zeros_like(acc_ref)

    h = h_ref[...]
    g = jnp.dot(h, w1g_ref[...], preferred_element_type=_F32)
    u = jnp.dot(h, w1u_ref[...], preferred_element_type=_F32)
    act = (g * jax.nn.sigmoid(g) * u).astype(_BF16)
    acc_ref[...] += jnp.dot(act, w2_ref[...], preferred_element_type=_F32)

    @pl.when(j == pl.num_programs(1) - 1)
    def _():
        _on_part(r, n_first, (xa_ref, oa_ref), (xb_ref, ob_ref),
                 lambda x_ref, o_ref: _gated_residual_rows(x_ref, acc_ref, o_ref, rstd_ref, vec_ref, gpost_ref,
                                                           mod_ref, mod_base + 2, 0.5))


def _ffn(xs, mod, group_of_tile, gpre, gpost, w1, w2, l, i, mod_base, tiles, tf, n_prompt, split_out):
    d = xs[0].shape[1]
    n = sum(x.shape[0] for x in xs)
    split_in = len(xs) == 2
    tm = tiles[1] if (split_in or split_out) else tiles[0]
    n_first = n_prompt // tm
    nff = w2.shape[2] // tf

    def first(r, j):
        return jnp.minimum(r, n_first - 1), 0

    def second(r, j):
        return jnp.maximum(r - n_first, 0), 0

    def whole(r, j):
        return r, 0

    x_mode = dict(pipeline_mode=pl.Buffered(1)) if tm * d * 4 >= 8 * 1024 * 1024 else {}
    x_specs = ([pl.BlockSpec((tm, d), first, **x_mode), pl.BlockSpec((tm, d), second, **x_mode)] if split_in
               else [pl.BlockSpec((tm, d), whole, **x_mode)])
    if split_out:
        out_shape = (jax.ShapeDtypeStruct((n_first * tm, d), _F32), jax.ShapeDtypeStruct((n - n_first * tm, d), _F32))
        out_specs = (pl.BlockSpec((tm, d), first), pl.BlockSpec((tm, d), second))
    else:
        out_shape = jax.ShapeDtypeStruct((n, d), _F32)
        out_specs = pl.BlockSpec((tm, d), whole)
    return pl.pallas_call(
        functools.partial(_ffn_kernel, mod_base=mod_base, n_first=n_first, split_in=split_in, split_out=split_out),
        out_shape=out_shape,
        grid=(n // tm, nff),
        in_specs=x_specs + [
            pl.BlockSpec((None, N_MOD, d), lambda r, j: (group_of_tile(r, tm), 0, 0)),
            pl.BlockSpec((1, d), lambda r, j: (0, 0)),
            pl.BlockSpec((1, d), lambda r, j: (0, 0)),
            pl.BlockSpec((None, None, d, tf), lambda r, j: (l, i, 0, j)),
            pl.BlockSpec((None, None, d, tf), lambda r, j: (l, i, 0, j)),
            pl.BlockSpec((None, None, tf, d), lambda r, j: (l, i, j, 0)),
        ],
        out_specs=out_specs,
        scratch_shapes=[pltpu.VMEM((tm, d), _BF16), pltpu.VMEM((tm, d), _F32), pltpu.VMEM((tm, LANES), _F32),
                        pltpu.VMEM((2, SUBLANES, d), _F32)],
        compiler_params=_params("parallel", "arbitrary", vmem=VMEM_LIMIT_LARGE if x_mode else VMEM_LIMIT),
        name=f"ffn_l{l}_{i}",
    )(*xs, mod, gpre, gpost, *w1, w2)


def _ffn_overlap_kernel(x_ref, mod_ref, gpre_ref, gpost_ref, w1g_ref, w1u_ref, w2_ref, o_ref,
                        h_ref, h_next_ref, acc_ref, acc_done_ref, rstd_ref, vec_ref, *, mod_base, n_tiles):
    r, j = pl.program_id(0), pl.program_id(1)
    last = pl.num_programs(1) - 1

    def ffn_update(h):
        g = jnp.dot(h, w1g_ref[...], preferred_element_type=_F32)
        u = jnp.dot(h, w1u_ref[...], preferred_element_type=_F32)
        act = (g * jax.nn.sigmoid(g) * u).astype(_BF16)
        return jnp.dot(act, w2_ref[...], preferred_element_type=_F32)

    def run(pieces):
        for piece in pieces:
            piece()

    def finish_previous(pieces=None):
        _gated_residual_rows(x_ref, acc_done_ref, o_ref, rstd_ref, vec_ref, gpost_ref, mod_ref, mod_base + 2, 0.5,
                             pieces)

    def prepare_next(pieces=None):
        _modulated_norm_rows(x_ref, h_next_ref, rstd_ref, vec_ref, gpre_ref, mod_ref, mod_base, pieces)

    @pl.when((r == 0) & (j == 0))
    def _():
        prepare_next()
        acc_done_ref[...] = jnp.zeros_like(acc_done_ref)

    @pl.when(r < n_tiles)
    def _():
        @pl.when(j == 0)
        def _():
            h = h_next_ref[...]
            acc_ref[...] = ffn_update(h)
            h_ref[...] = h
            under_matmul = []
            finish_previous(under_matmul)
            run(under_matmul)

        @pl.when((j > 0) & (j < last))
        def _():
            acc_ref[...] += ffn_update(h_ref[...])

        @pl.when(j == last)
        def _():
            acc_done_ref[...] = acc_ref[...] + ffn_update(h_ref[...])
            under_matmul = []
            prepare_next(under_matmul)
            run(under_matmul)

    @pl.when((r == n_tiles) & (j == 0))
    def _():
        finish_previous()


def _ffn_overlapped(x, mod, group_of_tile, gpre, gpost, w1, w2, l, i, mod_base, tm, tf):
    n, d = x.shape
    n_tiles = n // tm
    nff = w2.shape[2] // tf

    def neighbour(r, j):
        return jnp.where(j == nff - 1, jnp.minimum(r + 1, n_tiles - 1), jnp.maximum(r - 1, 0))

    def weight_tile(r, j):
        return jnp.where(r < n_tiles, j, nff - 1)

    return pl.pallas_call(
        functools.partial(_ffn_overlap_kernel, mod_base=mod_base, n_tiles=n_tiles),
        out_shape=jax.ShapeDtypeStruct((n, d), _F32),
        grid=(n_tiles + 1, nff),
        in_specs=[
            pl.BlockSpec((tm, d), lambda r, j: (neighbour(r, j), 0)),
            pl.BlockSpec((None, N_MOD, d), lambda r, j: (group_of_tile(neighbour(r, j), tm), 0, 0)),
            pl.BlockSpec((1, d), lambda r, j: (0, 0)),
            pl.BlockSpec((1, d), lambda r, j: (0, 0)),
            pl.BlockSpec((None, None, d, tf), lambda r, j: (l, i, 0, weight_tile(r, j))),
            pl.BlockSpec((None, None, d, tf), lambda r, j: (l, i, 0, weight_tile(r, j))),
            pl.BlockSpec((None, None, tf, d), lambda r, j: (l, i, weight_tile(r, j), 0)),
        ],
        out_specs=pl.BlockSpec((tm, d), lambda r, j: (jnp.maximum(r - 1, 0), 0)),
        scratch_shapes=[pltpu.VMEM((tm, d), _BF16)] * 2 + [pltpu.VMEM((tm, d), _F32)] * 2
        + [pltpu.VMEM((tm, LANES), _F32), pltpu.VMEM((2, SUBLANES, d), _F32)],
        compiler_params=_params("arbitrary", "arbitrary"),
        name=f"ffn_l{l}_{i}",
    )(x, mod, gpre, gpost, *w1, w2)


def _inproj_kernel(x_ref, mod_ref, gpre_ref, w_ref, o_ref, h_ref, rstd_ref, vec_ref, *, mod_base):
    @pl.when(pl.program_id(1) == 0)
    def _():
        _modulated_norm_rows(x_ref, h_ref, rstd_ref, vec_ref, gpre_ref, mod_ref, mod_base)

    o_ref[...] = jnp.dot(h_ref[...], w_ref[...], preferred_element_type=_F32)


def _inproj(x, mod, group_of_tile, gpre, w_in, l, mod_base, tm, tn):
    n, d = x.shape
    cols = w_in.shape[2]
    return pl.pallas_call(
        functools.partial(_inproj_kernel, mod_base=mod_base),
        out_shape=jax.ShapeDtypeStruct((n, cols), _F32),
        grid=(n // tm, cols // tn),
        in_specs=[
            pl.BlockSpec((tm, d), lambda r, j: (r, 0)),
            pl.BlockSpec((None, N_MOD, d), lambda r, j: (group_of_tile(r, tm), 0, 0)),
            pl.BlockSpec((1, d), lambda r, j: (0, 0)),
            pl.BlockSpec((None, d, tn), lambda r, j: (l, 0, j)),
        ],
        out_specs=pl.BlockSpec((tm, tn), lambda r, j: (r, j)),
        scratch_shapes=[pltpu.VMEM((tm, d), _BF16), pltpu.VMEM((tm, LANES), _F32),
                        pltpu.VMEM((2, SUBLANES, d), _F32)],
        compiler_params=_params("parallel", "arbitrary"),
        name=f"inproj_l{l}",
    )(x, mod, gpre, w_in)


LRU_TILES = SUBLANES // 2
CONV_HALO = SUBLANES
XP_PITCH = 2


def _sigmoid(x):
    return 0.5 * jnp.tanh(0.5 * x) + 0.5


def _lru_kernel(*refs, seq, row_chunk, unroll, aliased):
    if aliased:
        refs = refs[:8] + refs[9:]
    (xr_ref, gr_ref, cw_ref, cb_ref, wg_ref, bg_ref, lam_ref, h0_ref, rec_ref, st_ref,
     xp_ref, a_ref, b_ref, h_ref) = refs
    nt = LRU_TILES
    halo = jnp.zeros((XP_PITCH * CONV_HALO, LANES), _F32)
    for n in range(nt):
        pair, member = divmod(n, XP_PITCH)
        xp_ref[pair, 0:XP_PITCH * CONV_HALO, :] = halo
        xp_ref[pair, XP_PITCH * (CONV_HALO + seq):XP_PITCH * (2 * CONV_HALO + seq), :] = halo
        xp_ref[pair, pl.ds(XP_PITCH * CONV_HALO + member, seq, stride=XP_PITCH), :] = (
            xr_ref[:, n * LANES:(n + 1) * LANES])

    neg_log_a_scale = (0.5 * LRU_C) * jax.nn.softplus(-lam_ref[...])
    exp2_scale = -math.log2(math.e) * neg_log_a_scale

    def coeffs(c, carry):
        r0 = c * row_chunk
        for n in range(nt):
            cs = slice(n * LANES, (n + 1) * LANES)
            pair, member = divmod(n, XP_PITCH)
            y = cb_ref[:, cs]
            for tap in range(cw_ref.shape[0]):
                first = XP_PITCH * (r0 + CONV_HALO + tap - CONV_PAD_LEFT) + member
                y = y + cw_ref[tap:tap + 1, cs] * xp_ref[pair, pl.ds(first, row_chunk, stride=XP_PITCH), :]
            g4 = jnp.dot(y.astype(_BF16), wg_ref[n], preferred_element_type=_F32)
            half_y = 0.5 * y
            for direction in range(2):
                ua = 1.0 + jnp.tanh(g4[:, (2 * direction) * LANES:(2 * direction + 1) * LANES]
                                    + bg_ref[2 * direction:2 * direction + 1, cs])
                ux = 1.0 + jnp.tanh(g4[:, (2 * direction + 1) * LANES:(2 * direction + 2) * LANES]
                                    + bg_ref[2 * direction + 1:2 * direction + 2, cs])
                a = jnp.exp2(ua * exp2_scale[direction:direction + 1, cs])
                one_minus_a2 = jnp.tanh(ua * neg_log_a_scale[direction:direction + 1, cs]) * (1.0 + a * a)
                mult = jnp.where(one_minus_a2 > 0.0, one_minus_a2 * lax.rsqrt(one_minus_a2), 0.0)
                dense_rows = pl.ds(r0 * SUBLANES + nt * direction + n, row_chunk, stride=SUBLANES)
                a_ref[dense_rows, :] = a
                b_ref[dense_rows, :] = mult * ux * half_y
        return carry

    for c in range(seq // row_chunk):
        coeffs(c, 0)

    is_fwd = lax.broadcasted_iota(jnp.int32, (SUBLANES, LANES), 0) < nt

    block = unroll * SUBLANES

    def scan(c, h):
        rf = pl.multiple_of(c * block, block)
        rb = pl.multiple_of((seq - unroll) * SUBLANES - c * block, block)
        a_f, a_b = a_ref[pl.ds(rf, block), :], a_ref[pl.ds(rb, block), :]
        b_f, b_b = b_ref[pl.ds(rf, block), :], b_ref[pl.ds(rb, block), :]
        def coeff(r):
            up = slice(r * SUBLANES, (r + 1) * SUBLANES)
            down = slice((unroll - 1 - r) * SUBLANES, (unroll - r) * SUBLANES)
            return jnp.where(is_fwd, a_f[up], a_b[down]), jnp.where(is_fwd, b_f[up], b_b[down])

        def emit(r, h_r):
            h_ref[pl.ds(rf + r * SUBLANES, nt), :] = h_r[0:nt]
            h_ref[pl.ds(rb + (unroll - 1 - r) * SUBLANES + nt, nt), :] = h_r[nt:2 * nt]

        for r in range(0, unroll, 2):
            (a0, b0), (a1, b1) = coeff(r), coeff(r + 1)
            emit(r, a0 * h + b0)
            h = (a1 * a0) * h + (a1 * b0 + b1)
            emit(r + 1, h)
        return h

    h0 = jnp.concatenate([h0_ref[d:d + 1, n * LANES:(n + 1) * LANES] for d in range(2) for n in range(nt)], axis=0)
    h = lax.fori_loop(0, seq // unroll, scan, h0)
    for n in range(nt):
        st_ref[0:1, n * LANES:(n + 1) * LANES] = h[n:n + 1]
        st_ref[1:2, n * LANES:(n + 1) * LANES] = h[nt + n:nt + n + 1]

    def gate(c, carry):
        r0 = pl.multiple_of(c * row_chunk, row_chunk)
        for n in range(nt):
            cs = slice(n * LANES, (n + 1) * LANES)
            hf = h_ref[pl.ds(r0 * SUBLANES + n, row_chunk, stride=SUBLANES), :]
            hb = h_ref[pl.ds(r0 * SUBLANES + nt + n, row_chunk, stride=SUBLANES), :]
            rec_ref[pl.ds(r0, row_chunk), cs] = (
                (hf + hb) * jax.nn.gelu(gr_ref[pl.ds(r0, row_chunk), cs])).astype(rec_ref.dtype)
        return carry

    lax.fori_loop(0, seq // row_chunk, gate, 0)


def _lru(proj, rec, conv_w, conv_b, wg, bg, lam, h0, l, first_row, seq, d_rnn):
    n_seq = h0.shape[0]
    rb0 = first_row // seq
    cg = LRU_TILES * LANES
    ncg = d_rnn // cg
    row_chunk = min(seq, 256)
    aliased = rec is not None
    in_specs = [
        pl.BlockSpec((seq, cg), lambda s, c: (rb0 + s, c)),
        pl.BlockSpec((seq, cg), lambda s, c: (rb0 + s, ncg + c)),
        pl.BlockSpec((None, conv_w.shape[1], cg), lambda s, c: (l, 0, c)),
        pl.BlockSpec((None, 1, cg), lambda s, c: (l, 0, c)),
        pl.BlockSpec((None, LRU_TILES, LANES, 4 * LANES), lambda s, c: (l, c, 0, 0)),
        pl.BlockSpec((None, 4, cg), lambda s, c: (l, 0, c)),
        pl.BlockSpec((None, 2, cg), lambda s, c: (l, 0, c)),
        pl.BlockSpec((None, 2, cg), lambda s, c: (s, 0, c)),
    ]
    args = [proj, proj, conv_w, conv_b, wg, bg, lam, h0]
    if aliased:
        in_specs.append(pl.BlockSpec(memory_space=pl.ANY))
        args.append(rec)
    return pl.pallas_call(
        functools.partial(_lru_kernel, seq=seq, row_chunk=row_chunk, unroll=8, aliased=aliased),
        out_shape=(jax.ShapeDtypeStruct((proj.shape[0], d_rnn), _BF16),
                   jax.ShapeDtypeStruct((n_seq, 2, d_rnn), _F32)),
        grid=(n_seq, ncg),
        in_specs=in_specs,
        out_specs=(pl.BlockSpec((seq, cg), lambda s, c: (rb0 + s, c)),
                   pl.BlockSpec((None, 2, cg), lambda s, c: (s, 0, c))),
        scratch_shapes=[pltpu.VMEM((LRU_TILES // XP_PITCH, XP_PITCH * (seq + 2 * CONV_HALO), LANES), _F32)]
        + [pltpu.VMEM((seq * SUBLANES, LANES), _F32)] * 3,
        input_output_aliases={8: 0} if aliased else {},
        compiler_params=_params("parallel", "parallel", vmem=VMEM_LIMIT_LARGE),
        name=f"rglru_l{l}_t{seq}",
    )(*args)


Q_SCALE = QK_DIM ** -0.5 * math.log2(math.e)


def _rope(x, cos, sin_signed):
    lane = lax.broadcasted_iota(jnp.int32, x.shape, 1)
    half = QK_DIM // 4
    partner = jnp.where((lane & half) == 0, pltpu.roll(x, LANES - half, 1), pltpu.roll(x, half, 1))
    return x * cos + partner * sin_signed


def _lambda(lam_ref, lam_init):
    lv = lam_ref[...]
    return (jnp.exp(jnp.sum(lv[0:1] * lv[1:2], axis=-1, keepdims=True))
            - jnp.exp(jnp.sum(lv[2:3] * lv[3:4], axis=-1, keepdims=True)) + lam_init)


def _scores(q, kb):
    lane = lax.broadcasted_iota(jnp.int32, q.shape, 1)
    qm = jnp.concatenate([jnp.where(lane < QK_DIM, q, 0.0), jnp.where(lane >= QK_DIM, q, 0.0)], axis=0)
    return lax.dot_general(qm.astype(_BF16), kb, (((1,), (1,)), ((), ())), preferred_element_type=_F32)


def _softmax_diff_pv(s, vb1, lam, g, lam_init):
    tq = s.shape[0] // 2
    e = jnp.exp2(s - jnp.max(s, axis=-1, keepdims=True)).astype(_BF16)
    ov = jnp.dot(e, vb1, preferred_element_type=_F32)
    o = (ov[0:tq, 0:LANES] / ov[0:tq, LANES:2 * LANES]
         - lam * (ov[tq:2 * tq, 0:LANES] / ov[tq:2 * tq, LANES:2 * LANES]))
    return _rms(o, g) * (1.0 - lam_init)


def _attn_ctx_kernel(q_ref, k_ref, v_ref, ck_ref, cv_ref, cq_ref, sq_ref, ckk_ref, skk_ref, lam_ref, g_ref,
                     att_in_ref, o_ref, kbuf, vbuf, *, past, seq, sub, lam_init):
    del att_in_ref

    @pl.when(pl.program_id(2) == 0)
    def _():
        kbuf[0:past, :] = ck_ref[...].astype(kbuf.dtype)
        kbuf[past:past + seq, :] = _rope(k_ref[...], ckk_ref[...], skk_ref[...]).astype(kbuf.dtype)
        vbuf[0:past, 0:LANES] = cv_ref[...].astype(vbuf.dtype)
        vbuf[past:past + seq, 0:LANES] = v_ref[...].astype(vbuf.dtype)
        vbuf[:, LANES:2 * LANES] = jnp.ones((past + seq, LANES), vbuf.dtype)

    lam = _lambda(lam_ref, lam_init)

    def scores(i):
        rows = slice(i * sub, (i + 1) * sub)
        return _scores(_rope(q_ref[rows, :], cq_ref[rows, :], sq_ref[rows, :]) * Q_SCALE, kbuf[...])

    s = scores(0)
    for i in range(q_ref.shape[0] // sub):
        s_next = scores(i + 1) if (i + 1) * sub < q_ref.shape[0] else None
        o = _softmax_diff_pv(s, vbuf[...], lam, g_ref[...], lam_init)
        o_ref[i * sub:(i + 1) * sub, :] = o.astype(o_ref.dtype)
        s = s_next


def _attn_prompt_kernel(*refs, heads, lam_init, aliased):
    if aliased:
        refs = refs[:5] + refs[7:]
    q_ref, k_ref, v_ref, lam_ref, g_ref, o_ref, nk_ref, nv_ref = refs
    nk_ref[...] = k_ref[...]
    nv_ref[...] = v_ref[...]
    lam = _lambda(lam_ref, lam_init)
    ones = jnp.ones((k_ref.shape[0], LANES), _BF16)
    for h in range(heads):
        hs = slice(h * LANES, (h + 1) * LANES)
        vb1 = jnp.concatenate([v_ref[:, hs].astype(_BF16), ones], axis=1)
        s = _scores(q_ref[:, hs] * Q_SCALE, k_ref[:, hs].astype(_BF16))
        o = _softmax_diff_pv(s, vb1, lam, g_ref[...], lam_init)
        o_ref[:, hs] = o.astype(o_ref.dtype)


def _attention_prompt(proj, lamv, gsub, caches, l, depth, n_seq, seq, heads, col_q, lam_init):
    width = heads * LANES
    cq = col_q // width
    aliased = caches is not None
    in_specs = [
        pl.BlockSpec((seq, width), lambda b: (b, cq)),
        pl.BlockSpec((seq, width), lambda b: (b, cq + 1)),
        pl.BlockSpec((seq, width), lambda b: (b, cq + 2)),
        pl.BlockSpec((None, 4, QK_DIM), lambda b: (l, 0, 0)),
        pl.BlockSpec((None, 1, LANES), lambda b: (l, 0, 0)),
    ]
    args = [proj, proj, proj, lamv, gsub]
    if aliased:
        in_specs += [pl.BlockSpec(memory_space=pl.ANY)] * 2
        args += list(caches)
    cache_sds = jax.ShapeDtypeStruct((n_seq, depth, seq, width), _F32)
    return pl.pallas_call(
        functools.partial(_attn_prompt_kernel, heads=heads, lam_init=lam_init, aliased=aliased),
        out_shape=(jax.ShapeDtypeStruct((proj.shape[0], width), _BF16), cache_sds, cache_sds),
        grid=(n_seq,),
        in_specs=in_specs,
        out_specs=(pl.BlockSpec((seq, width), lambda b: (b, 0)),
                   pl.BlockSpec((None, None, seq, width), lambda b: (b, l, 0, 0)),
                   pl.BlockSpec((None, None, seq, width), lambda b: (b, l, 0, 0))),
        input_output_aliases={5: 1, 6: 2} if aliased else {},
        compiler_params=_params("parallel"),
        name=f"diffattn_prompt_l{l}",
    )(*args)


def _attention_ctx(proj, att, lamv, gsub, cache_k, cache_v, cos, sin, l, first_row, n_seq, seq, tq, heads,
                   col_q, lam_init):
    past = cache_k.shape[2]
    rq0, rk0 = first_row // tq, first_row // seq
    nq = seq // tq
    hq, hk, hv = col_q // LANES, col_q // LANES + heads, col_q // LANES + 2 * heads
    in_specs = [
        pl.BlockSpec((tq, LANES), lambda b, h, t: (rq0 + b * nq + t, hq + h)),
        pl.BlockSpec((seq, LANES), lambda b, h, t: (rk0 + b, hk + h)),
        pl.BlockSpec((seq, LANES), lambda b, h, t: (rk0 + b, hv + h)),
        pl.BlockSpec((None, None, past, LANES), lambda b, h, t: (b, l, 0, h)),
        pl.BlockSpec((None, None, past, LANES), lambda b, h, t: (b, l, 0, h)),
        pl.BlockSpec((tq, LANES), lambda b, h, t: (t, 0)),
        pl.BlockSpec((tq, LANES), lambda b, h, t: (t, 0)),
        pl.BlockSpec((seq, LANES), lambda b, h, t: (0, 0)),
        pl.BlockSpec((seq, LANES), lambda b, h, t: (0, 0)),
        pl.BlockSpec((None, 4, QK_DIM), lambda b, h, t: (l, 0, 0)),
        pl.BlockSpec((None, 1, LANES), lambda b, h, t: (l, 0, 0)),
        pl.BlockSpec(memory_space=pl.ANY),
    ]
    return pl.pallas_call(
        functools.partial(_attn_ctx_kernel, past=past, seq=seq, sub=min(tq, 256), lam_init=lam_init),
        out_shape=jax.ShapeDtypeStruct(att.shape, att.dtype),
        grid=(n_seq, heads, nq),
        in_specs=in_specs,
        out_specs=pl.BlockSpec((tq, LANES), lambda b, h, t: (rq0 + b * nq + t, h)),
        scratch_shapes=[pltpu.VMEM((past + seq, LANES), _BF16), pltpu.VMEM((past + seq, 2 * LANES), _BF16)],
        input_output_aliases={11: 0},
        compiler_params=_params("parallel", "parallel", "arbitrary"),
        name=f"diffattn_ctx_l{l}",
    )(proj, proj, proj, cache_k, cache_v, cos, sin, cos, sin, lamv, gsub, att)


def _merge_kernel(x_ref, mod_ref, gpost_ref, rec_ref, att_ref, gm1a_ref, gm1b_ref, gm2a_ref, gm2b_ref,
                  plru_ref, pattn_ref, wout_ref, o_ref, acc_ref, rstd_ref, vec_ref, *, mod_base):
    rec, att = rec_ref[...], att_ref[...]
    tn = gm1a_ref.shape[1]
    for j, (gm1_ref, gm2_ref) in enumerate(((gm1a_ref, gm2a_ref), (gm1b_ref, gm2b_ref))):
        cols = slice(j * tn, (j + 1) * tn)
        a = jnp.dot(rec, plru_ref[:, cols], preferred_element_type=_F32)
        b = jnp.dot(att, pattn_ref[:, cols], preferred_element_type=_F32)
        merged = _sigmoid(gm1_ref[...]) * a + _sigmoid(gm2_ref[...]) * b
        part = jnp.dot(merged.astype(_BF16), wout_ref[cols, :], preferred_element_type=_F32)
        if j == 0:
            acc_ref[...] = part
        else:
            acc_ref[...] += part
    _gated_residual_rows(x_ref, acc_ref, o_ref, rstd_ref, vec_ref, gpost_ref, mod_ref, mod_base + 2, 1.0)


def _merge(x, mod, group_of_tile, gpost, rec, att, proj, col_gm, p_lru, p_attn, w_out, l, mod_base, tm):
    n, d = x.shape
    d_rnn, v_cols = rec.shape[1], att.shape[1]
    tn = d // 2
    g0 = col_gm // tn
    resident = dict(pipeline_mode=pl.Buffered(1))
    return pl.pallas_call(
        functools.partial(_merge_kernel, mod_base=mod_base),
        out_shape=jax.ShapeDtypeStruct((n, d), _F32),
        grid=(n // tm,),
        in_specs=[
            pl.BlockSpec((tm, d), lambda r: (r, 0)),
            pl.BlockSpec((None, N_MOD, d), lambda r: (group_of_tile(r, tm), 0, 0)),
            pl.BlockSpec((1, d), lambda r: (0, 0)),
            pl.BlockSpec((tm, d_rnn), lambda r: (r, 0)),
            pl.BlockSpec((tm, v_cols), lambda r: (r, 0)),
            pl.BlockSpec((tm, tn), lambda r: (r, g0)),
            pl.BlockSpec((tm, tn), lambda r: (r, g0 + 1)),
            pl.BlockSpec((tm, tn), lambda r: (r, g0 + 2)),
            pl.BlockSpec((tm, tn), lambda r: (r, g0 + 3)),
            pl.BlockSpec((None, d_rnn, d), lambda r: (l, 0, 0), **resident),
            pl.BlockSpec((None, v_cols, d), lambda r: (l, 0, 0), **resident),
            pl.BlockSpec((None, d, d), lambda r: (l, 0, 0), **resident),
        ],
        out_specs=pl.BlockSpec((tm, d), lambda r: (r, 0)),
        scratch_shapes=[pltpu.VMEM((tm, d), _F32), pltpu.VMEM((tm, LANES), _F32),
                        pltpu.VMEM((2, SUBLANES, d), _F32)],
        compiler_params=_params("parallel"),
        name=f"merge_l{l}",
    )(x, mod, gpost, rec, att, proj, proj, proj, proj, p_lru, p_attn, w_out)


def _rope_tables(n_tokens):
    t = jnp.arange(n_tokens)
    row = (t // GRID_W).astype(_F32)
    col = (t % GRID_W).astype(_F32)
    n_freq = QK_DIM // 4
    freqs = 1.0 / (ROPE_BASE ** (jnp.arange(0, 2 * n_freq, 2, dtype=_F32) / (2 * n_freq)))
    ar, ac = row[:, None] * freqs, col[:, None] * freqs
    cos = jnp.concatenate([jnp.cos(ar), jnp.cos(ar), jnp.cos(ac), jnp.cos(ac)], axis=-1)
    sin = jnp.concatenate([-jnp.sin(ar), jnp.sin(ar), -jnp.sin(ac), jnp.sin(ac)], axis=-1)
    reps = LANES // QK_DIM
    return jnp.tile(cos, (1, reps)), jnp.tile(sin, (1, reps))


def _largest_tile(candidates, *extents):
    return next(t for t in candidates if all(e % t == 0 for e in extents))


def kernel(x_prompt, x_sample, c, cache_k, cache_v, state_lru, c_ctx, w_mod, b_mod, g_pre, g_post, ffn_w1, ffn_w2, w_in, conv_w, conv_b, lru_wa, lru_ba, lru_wx, lru_bx, lru_lambda, lam_q1, lam_k1, lam_q2, lam_k2, attn_subln, p_lru, p_attn, w_out):
    batch, seq, d = x_prompt.shape
    dec_batch, dec_seq, _ = x_sample.shape
    depth = w_mod.shape[0]
    d_rnn = conv_w.shape[2]
    heads = cache_k.shape[3]
    past = cache_k.shape[2]
    v_cols = heads * cache_v.shape[4]
    qk_cols = heads * cache_k.shape[4]
    d_ff = ffn_w2.shape[2]
    n_prompt, n_sample = batch * seq, dec_batch * dec_seq
    assert cache_k.shape[4] == LANES and cache_v.shape[4] == LANES and 1 + dec_batch <= MOD_ROWS
    col_q = 2 * d_rnn
    col_gm = col_q + 2 * qk_cols + v_cols

    tm = _largest_tile((512, 256), n_prompt, dec_seq)
    tm_in = _largest_tile((1024, 512, 256), n_prompt, dec_seq)
    tq = _largest_tile((2048, 1024, 512, 256, 128), dec_seq)
    tm_merge = _largest_tile((256, 128), n_prompt, dec_seq)
    tf, tn_in = 512, 1024

    def group_of_tile(r, rows):
        first = n_prompt // rows
        return jnp.where(r < first, 0, 1 + (r - first) // (dec_seq // rows))

    ffp = -(-d_ff // tf) * tf
    padc = ((0, 0), (0, 0), (0, 0), (0, ffp - d_ff))
    w1 = (jnp.pad(ffn_w1[..., :d_ff], padc).astype(_BF16), jnp.pad(ffn_w1[..., d_ff:], padc).astype(_BF16))
    w2 = jnp.pad(ffn_w2, ((0, 0), (0, 0), (0, ffp - d_ff), (0, 0))).astype(_BF16)
    w_in_b, p_lru_b, p_attn_b, w_out_b = (w.astype(_BF16) for w in (w_in, p_lru, p_attn, w_out))
    wg = (0.5 * jnp.concatenate([lru_wa[:, 0], lru_wx[:, 0], lru_wa[:, 1], lru_wx[:, 1]], axis=-1)).astype(_BF16)
    bg = 0.5 * jnp.stack([lru_ba[:, 0], lru_bx[:, 0], lru_ba[:, 1], lru_bx[:, 1]], axis=1)
    lamv = jnp.stack([lam_q1, lam_k1, lam_q2, lam_k2], axis=1)
    gsub = attn_subln.reshape(depth, 1, -1)
    conv_b3 = conv_b.reshape(depth, 1, d_rnn)
    cache_k2 = cache_k.reshape(dec_batch, depth, past, qk_cols)
    cache_v2 = cache_v.reshape(dec_batch, depth, past, v_cols)
    cos, sin = _rope_tables(dec_seq)
    zero_state = jnp.zeros((batch, 2, d_rnn), _F32)

    cond = jnp.concatenate([c_ctx[None], c, jnp.zeros((MOD_ROWS - 1 - dec_batch, d), _F32)], axis=0)
    mod_all = _modulation(cond, w_mod, b_mod).reshape(depth, MOD_ROWS, N_MOD, d)

    xs = (x_prompt.reshape(n_prompt, d), x_sample.reshape(n_sample, d))
    caches, new_s = None, []
    for l in range(depth):
        mod = mod_all[l]
        gpre = [g_pre[l, i].reshape(1, d) for i in range(3)]
        gpost = [g_post[l, i].reshape(1, d) for i in range(3)]
        lam_init = 0.8 - 0.6 * math.exp(-0.3 * l)

        if len(xs) == 2:
            x = _ffn(xs, mod, group_of_tile, gpre[0], gpost[0], w1, w2, l, 0, 0, (tm, tm), tf, n_prompt, False)
        else:
            x = _ffn_overlapped(xs[0], mod, group_of_tile, gpre[0], gpost[0], w1, w2, l, 0, 0, tm, tf)
        proj = _inproj(x, mod, group_of_tile, gpre[1], w_in_b, l, 3, tm_in, tn_in)
        rec, st_p = _lru(proj, None, conv_w, conv_b3, wg, bg, lru_lambda, zero_state, l, 0, seq, d_rnn)
        rec, _ = _lru(proj, rec, conv_w, conv_b3, wg, bg, lru_lambda, state_lru[:, l], l, n_prompt, dec_seq, d_rnn)
        att, *caches = _attention_prompt(proj, lamv, gsub, caches, l, depth, batch, seq, heads, col_q, lam_init)
        att = _attention_ctx(proj, att, lamv, gsub, cache_k2, cache_v2, cos, sin, l, n_prompt, dec_batch, dec_seq,
                             tq, heads, col_q, lam_init)
        x = _merge(x, mod, group_of_tile, gpost[1], rec, att, proj, col_gm, p_lru_b, p_attn_b, w_out_b,
                   l, 3, tm_merge)
        last = l == depth - 1
        if last:
            xs = _ffn((x,), mod, group_of_tile, gpre[2], gpost[2], w1, w2, l, 1, 6, (tm, tm), tf, n_prompt, True)
        else:
            xs = (_ffn_overlapped(x, mod, group_of_tile, gpre[2], gpost[2], w1, w2, l, 1, 6, tm, tf),)
        new_s.append(st_p)

    y_prompt = xs[0].reshape(batch, seq, d)
    y_sample = xs[1].reshape(dec_batch, dec_seq, d)
    new_k, new_v = (a.reshape(batch, depth, seq, heads, -1) for a in caches)
    return (y_prompt, y_sample, new_k, new_v, jnp.stack(new_s, axis=1))
```

```python
import functools
import math

import jax
import jax.numpy as jnp
from jax import lax
from jax.experimental import pallas as pl
from jax.experimental.pallas import tpu as pltpu

NORM_EPS = 1e-6
LRU_C = 8.0
N_MOD = 9
GRID_W = 64
ROPE_BASE = 10000.0
QK_DIM = 64
CONV_PAD_LEFT = 2

LANES = 128
SUBLANES = 8
MOD_ROWS = 16
VMEM_LIMIT = 52 * 1024 * 1024
VMEM_LIMIT_LARGE = 58 * 1024 * 1024

_BF16 = jnp.bfloat16
_F32 = jnp.float32


def _rms(x, g):
    ms = jnp.mean(x * x, axis=-1, keepdims=True)
    return x * lax.rsqrt(ms + NORM_EPS) * g


def _params(*sem, vmem=VMEM_LIMIT):
    return pltpu.CompilerParams(dimension_semantics=sem, vmem_limit_bytes=vmem)


def _mod_kernel(c_ref, w_ref, b_ref, o_ref):
    c = c_ref[...]
    s = (c * jax.nn.sigmoid(c)).astype(_BF16)
    o_ref[...] = jnp.dot(s, w_ref[...].astype(_BF16), preferred_element_type=_F32) + b_ref[...]


def _modulation(cond, w_mod, b_mod, tn=1024):
    depth, d, n = w_mod.shape
    return pl.pallas_call(
        _mod_kernel,
        out_shape=jax.ShapeDtypeStruct((depth, MOD_ROWS, n), _F32),
        grid=(depth, n // tn),
        in_specs=[
            pl.BlockSpec((MOD_ROWS, d), lambda l, j: (0, 0)),
            pl.BlockSpec((None, d, tn), lambda l, j: (l, 0, j)),
            pl.BlockSpec((None, 1, tn), lambda l, j: (l, 0, j)),
        ],
        out_specs=pl.BlockSpec((None, MOD_ROWS, tn), lambda l, j: (l, 0, j)),
        compiler_params=_params("parallel", "parallel"),
        name="modulation",
    )(cond, w_mod, b_mod.reshape(depth, 1, n))


NORM_ROWS = 16


def _for_row_chunks(n_rows, fn, pieces=None):
    if pieces is not None:
        pieces.extend(functools.partial(fn, pl.ds(i * NORM_ROWS, NORM_ROWS)) for i in range(n_rows // NORM_ROWS))
        return

    def body(i, carry):
        fn(pl.ds(pl.multiple_of(i * NORM_ROWS, NORM_ROWS), NORM_ROWS))
        return carry

    lax.fori_loop(0, n_rows // NORM_ROWS, body, 0, unroll=2)


def _once(fn, pieces=None):
    fn() if pieces is None else pieces.append(fn)


def _row_rstd(x_ref, rstd_ref, pieces=None):
    n_tiles = x_ref.shape[1] // LANES

    def lane_partial_sums(rows):
        sq = jnp.square(x_ref[rows, :])
        part = sq[:, 0:LANES]
        for t in range(1, n_tiles):
            part = part + sq[:, t * LANES:(t + 1) * LANES]
        rstd_ref[rows, :] = part

    def reduce():
        ms = jnp.sum(rstd_ref[...], axis=-1, keepdims=True) * (1.0 / x_ref.shape[1])
        rstd_ref[...] = jnp.broadcast_to(lax.rsqrt(ms + NORM_EPS), rstd_ref.shape)

    _for_row_chunks(x_ref.shape[0], lane_partial_sums, pieces)
    _once(reduce, pieces)


def _modulated_norm_rows(x_ref, h_ref, rstd_ref, vec_ref, g_ref, mod_ref, mod_base, pieces=None):
    n_tiles = x_ref.shape[1] // LANES

    def vectors():
        sh = mod_ref[mod_base:mod_base + 1, :]
        sc = mod_ref[mod_base + 1:mod_base + 2, :]
        vec_ref[0] = jnp.broadcast_to(g_ref[...] * (1.0 + sc), vec_ref.shape[1:])
        vec_ref[1] = jnp.broadcast_to(sh, vec_ref.shape[1:])

    def normalise(rows):
        rstd = jnp.tile(rstd_ref[rows, :], (1, n_tiles))
        scale, shift = (jnp.tile(vec_ref[k], (NORM_ROWS // SUBLANES, 1)) for k in range(2))
        h_ref[rows, :] = (x_ref[rows, :] * rstd * scale + shift).astype(h_ref.dtype)

    _row_rstd(x_ref, rstd_ref, pieces)
    _once(vectors, pieces)
    _for_row_chunks(x_ref.shape[0], normalise, pieces)


def _gated_residual_rows(x_ref, acc_ref, o_ref, rstd_ref, vec_ref, g_ref, mod_ref, gate_row, gate_scale,
                         pieces=None):
    n_tiles = x_ref.shape[1] // LANES

    def vectors():
        gt = mod_ref[gate_row:gate_row + 1, :]
        if gate_scale != 1.0:
            gt = gate_scale * gt
        vec_ref[0] = jnp.broadcast_to(gt * g_ref[...], vec_ref.shape[1:])

    def residual(rows):
        rstd = jnp.tile(rstd_ref[rows, :], (1, n_tiles))
        scale = jnp.tile(vec_ref[0], (NORM_ROWS // SUBLANES, 1))
        o_ref[rows, :] = x_ref[rows, :] + acc_ref[rows, :] * rstd * scale

    _row_rstd(acc_ref, rstd_ref, pieces)
    _once(vectors, pieces)
    _for_row_chunks(x_ref.shape[0], residual, pieces)


def _on_part(r, n_first, refs_first, refs_second, fn):
    if all(a is b for a, b in zip(refs_first, refs_second)):
        fn(*refs_first)
        return
    pl.when(r < n_first)(lambda: fn(*refs_first))
    pl.when(r >= n_first)(lambda: fn(*refs_second))


def _ffn_kernel(*refs, mod_base, n_first, split_in, split_out):
    refs = list(refs)
    xa_ref = refs.pop(0)
    xb_ref = refs.pop(0) if split_in else xa_ref
    mod_ref, gpre_ref, gpost_ref, w1g_ref, w1u_ref, w2_ref, oa_ref = refs[:7]
    ob_ref = refs[7] if split_out else oa_ref
    h_ref, acc_ref, rstd_ref, vec_ref = refs[-4:]
    r, j = pl.program_id(0), pl.program_id(1)

    @pl.when(j == 0)
    def _():
        _on_part(r, n_first, (xa_ref,), (xb_ref,),
                 lambda x_ref: _modulated_norm_rows(x_ref, h_ref, rstd_ref, vec_ref, gpre_ref, mod_ref,
                                                    mod_base))
        acc_ref[...] = jnp.zeros_like(acc_ref)

    h = h_ref[...]
    g = jnp.dot(h, w1g_ref[...], preferred_element_type=_F32)
    u = jnp.dot(h, w1u_ref[...], preferred_element_type=_F32)
    act = (g * jax.nn.sigmoid(g) * u).astype(_BF16)
    acc_ref[...] += jnp.dot(act, w2_ref[...], preferred_element_type=_F32)

    @pl.when(j == pl.num_programs(1) - 1)
    def _():
        _on_part(r, n_first, (xa_ref, oa_ref), (xb_ref, ob_ref),
                 lambda x_ref, o_ref: _gated_residual_rows(x_ref, acc_ref, o_ref, rstd_ref, vec_ref, gpost_ref,
                                                           mod_ref, mod_base + 2, 0.5))


def _ffn(xs, mod, group_of_tile, gpre, gpost, w1, w2, l, i, mod_base, tiles, tf, n_prompt, split_out):
    d = xs[0].shape[1]
    n = sum(x.shape[0] for x in xs)
    split_in = len(xs) == 2
    tm = tiles[1] if (split_in or split_out) else tiles[0]
    n_first = n_prompt // tm
    nff = w2.shape[2] // tf

    def first(r, j):
        return jnp.minimum(r, n_first - 1), 0

    def second(r, j):
        return jnp.maximum(r - n_first, 0), 0

    def whole(r, j):
        return r, 0

    x_mode = dict(pipeline_mode=pl.Buffered(1)) if tm * d * 4 >= 8 * 1024 * 1024 else {}
    x_specs = ([pl.BlockSpec((tm, d), first, **x_mode), pl.BlockSpec((tm, d), second, **x_mode)] if split_in
               else [pl.BlockSpec((tm, d), whole, **x_mode)])
    if split_out:
        out_shape = (jax.ShapeDtypeStruct((n_first * tm, d), _F32), jax.ShapeDtypeStruct((n - n_first * tm, d), _F32))
        out_specs = (pl.BlockSpec((tm, d), first), pl.BlockSpec((tm, d), second))
    else:
        out_shape = jax.ShapeDtypeStruct((n, d), _F32)
        out_specs = pl.BlockSpec((tm, d), whole)
    return pl.pallas_call(
        functools.partial(_ffn_kernel, mod_base=mod_base, n_first=n_first, split_in=split_in, split_out=split_out),
        out_shape=out_shape,
        grid=(n // tm, nff),
        in_specs=x_specs + [
            pl.BlockSpec((None, N_MOD, d), lambda r, j: (group_of_tile(r, tm), 0, 0)),
            pl.BlockSpec((1, d), lambda r, j: (0, 0)),
            pl.BlockSpec((1, d), lambda r, j: (0, 0)),
            pl.BlockSpec((None, None, d, tf), lambda r, j: (l, i, 0, j)),
            pl.BlockSpec((None, None, d, tf), lambda r, j: (l, i, 0, j)),
            pl.BlockSpec((None, None, tf, d), lambda r, j: (l, i, j, 0)),
        ],
        out_specs=out_specs,
        scratch_shapes=[pltpu.VMEM((tm, d), _BF16), pltpu.VMEM((tm, d), _F32), pltpu.VMEM((tm, LANES), _F32),
                        pltpu.VMEM((2, SUBLANES, d), _F32)],
        compiler_params=_params("parallel", "arbitrary", vmem=VMEM_LIMIT_LARGE if x_mode else VMEM_LIMIT),
        name=f"ffn_l{l}_{i}",
    )(*xs, mod, gpre, gpost, *w1, w2)


def _ffn_overlap_kernel(x_ref, mod_ref, gpre_ref, gpost_ref, w1g_ref, w1u_ref, w2_ref, o_ref,
                        h_ref, h_next_ref, acc_ref, acc_done_ref, rstd_ref, vec_ref, *, mod_base, n_tiles):
    r, j = pl.program_id(0), pl.program_id(1)
    last = pl.num_programs(1) - 1

    def ffn_update(h):
        g = jnp.dot(h, w1g_ref[...], preferred_element_type=_F32)
        u = jnp.dot(h, w1u_ref[...], preferred_element_type=_F32)
        act = (g * jax.nn.sigmoid(g) * u).astype(_BF16)
        return jnp.dot(act, w2_ref[...], preferred_element_type=_F32)

    def run(pieces):
        for piece in pieces:
            piece()

    def finish_previous(pieces=None):
        _gated_residual_rows(x_ref, acc_done_ref, o_ref, rstd_ref, vec_ref, gpost_ref, mod_ref, mod_base + 2, 0.5,
                             pieces)

    def prepare_next(pieces=None):
        _modulated_norm_rows(x_ref, h_next_ref, rstd_ref, vec_ref, gpre_ref, mod_ref, mod_base, pieces)

    @pl.when((r == 0) & (j == 0))
    def _():
        prepare_next()
        acc_done_ref[...] = jnp.zeros_like(acc_done_ref)

    @pl.when(r < n_tiles)
    def _():
        @pl.when(j == 0)
        def _():
            h = h_next_ref[...]
            acc_ref[...] = ffn_update(h)
            h_ref[...] = h
            under_matmul = []
            finish_previous(under_matmul)
            run(under_matmul)

        @pl.when((j > 0) & (j < last))
        def _():
            acc_ref[...] += ffn_update(h_ref[...])

        @pl.when(j == last)
        def _():
            acc_done_ref[...] = acc_ref[...] + ffn_update(h_ref[...])
            under_matmul = []
            prepare_next(under_matmul)
            run(under_matmul)

    @pl.when((r == n_tiles) & (j == 0))
    def _():
        finish_previous()


def _ffn_overlapped(x, mod, group_of_tile, gpre, gpost, w1, w2, l, i, mod_base, tm, tf):
    n, d = x.shape
    n_tiles = n // tm
    nff = w2.shape[2] // tf

    def neighbour(r, j):
        return jnp.where(j == nff - 1, jnp.minimum(r + 1, n_tiles - 1), jnp.maximum(r - 1, 0))

    def weight_tile(r, j):
        return jnp.where(r < n_tiles, j, nff - 1)

    return pl.pallas_call(
        functools.partial(_ffn_overlap_kernel, mod_base=mod_base, n_tiles=n_tiles),
        out_shape=jax.ShapeDtypeStruct((n, d), _F32),
        grid=(n_tiles + 1, nff),
        in_specs=[
            pl.BlockSpec((tm, d), lambda r, j: (neighbour(r, j), 0)),
            pl.BlockSpec((None, N_MOD, d), lambda r, j: (group_of_tile(neighbour(r, j), tm), 0, 0)),
            pl.BlockSpec((1, d), lambda r, j: (0, 0)),
            pl.BlockSpec((1, d), lambda r, j: (0, 0)),
            pl.BlockSpec((None, None, d, tf), lambda r, j: (l, i, 0, weight_tile(r, j))),
            pl.BlockSpec((None, None, d, tf), lambda r, j: (l, i, 0, weight_tile(r, j))),
            pl.BlockSpec((None, None, tf, d), lambda r, j: (l, i, weight_tile(r, j), 0)),
        ],
        out_specs=pl.BlockSpec((tm, d), lambda r, j: (jnp.maximum(r - 1, 0), 0)),
        scratch_shapes=[pltpu.VMEM((tm, d), _BF16)] * 2 + [pltpu.VMEM((tm, d), _F32)] * 2
        + [pltpu.VMEM((tm, LANES), _F32), pltpu.VMEM((2, SUBLANES, d), _F32)],
        compiler_params=_params("arbitrary", "arbitrary"),
        name=f"ffn_l{l}_{i}",
    )(x, mod, gpre, gpost, *w1, w2)


def _inproj_kernel(x_ref, mod_ref, gpre_ref, w_ref, o_ref, h_ref, rstd_ref, vec_ref, *, mod_base):
    @pl.when(pl.program_id(1) == 0)
    def _():
        _modulated_norm_rows(x_ref, h_ref, rstd_ref, vec_ref, gpre_ref, mod_ref, mod_base)

    o_ref[...] = jnp.dot(h_ref[...], w_ref[...], preferred_element_type=_F32)


def _inproj(x, mod, group_of_tile, gpre, w_in, l, mod_base, tm, tn):
    n, d = x.shape
    cols = w_in.shape[2]
    return pl.pallas_call(
        functools.partial(_inproj_kernel, mod_base=mod_base),
        out_shape=jax.ShapeDtypeStruct((n, cols), _F32),
        grid=(n // tm, cols // tn),
        in_specs=[
            pl.BlockSpec((tm, d), lambda r, j: (r, 0)),
            pl.BlockSpec((None, N_MOD, d), lambda r, j: (group_of_tile(r, tm), 0, 0)),
            pl.BlockSpec((1, d), lambda r, j: (0, 0)),
            pl.BlockSpec((None, d, tn), lambda r, j: (l, 0, j)),
        ],
        out_specs=pl.BlockSpec((tm, tn), lambda r, j: (r, j)),
        scratch_shapes=[pltpu.VMEM((tm, d), _BF16), pltpu.VMEM((tm, LANES), _F32),
                        pltpu.VMEM((2, SUBLANES, d), _F32)],
        compiler_params=_params("parallel", "arbitrary"),
        name=f"inproj_l{l}",
    )(x, mod, gpre, w_in)


LRU_TILES = SUBLANES // 2
CONV_HALO = SUBLANES
XP_PITCH = 2


def _sigmoid(x):
    return 0.5 * jnp.tanh(0.5 * x) + 0.5


def _lru_kernel(*refs, seq, row_chunk, unroll, aliased):
    if aliased:
        refs = refs[:8] + refs[9:]
    (xr_ref, gr_ref, cw_ref, cb_ref, wg_ref, bg_ref, lam_ref, h0_ref, rec_ref, st_ref,
     xp_ref, a_ref, b_ref, h_ref) = refs
    nt = LRU_TILES
    halo = jnp.zeros((XP_PITCH * CONV_HALO, LANES), _F32)
    for n in range(nt):
        pair, member = divmod(n, XP_PITCH)
        xp_ref[pair, 0:XP_PITCH * CONV_HALO, :] = halo
        xp_ref[pair, XP_PITCH * (CONV_HALO + seq):XP_PITCH * (2 * CONV_HALO + seq), :] = halo
        xp_ref[pair, pl.ds(XP_PITCH * CONV_HALO + member, seq, stride=XP_PITCH), :] = (
            xr_ref[:, n * LANES:(n + 1) * LANES])

    neg_log_a_scale = (0.5 * LRU_C) * jax.nn.softplus(-lam_ref[...])
    exp2_scale = -math.log2(math.e) * neg_log_a_scale

    def coeffs(c, carry):
        r0 = c * row_chunk
        for n in range(nt):
            cs = slice(n * LANES, (n + 1) * LANES)
            pair, member = divmod(n, XP_PITCH)
            y = cb_ref[:, cs]
            for tap in range(cw_ref.shape[0]):
                first = XP_PITCH * (r0 + CONV_HALO + tap - CONV_PAD_LEFT) + member
                y = y + cw_ref[tap:tap + 1, cs] * xp_ref[pair, pl.ds(first, row_chunk, stride=XP_PITCH), :]
            g4 = jnp.dot(y.astype(_BF16), wg_ref[n], preferred_element_type=_F32)
            half_y = 0.5 * y
            for direction in range(2):
                ua = 1.0 + jnp.tanh(g4[:, (2 * direction) * LANES:(2 * direction + 1) * LANES]
                                    + bg_ref[2 * direction:2 * direction + 1, cs])
                ux = 1.0 + jnp.tanh(g4[:, (2 * direction + 1) * LANES:(2 * direction + 2) * LANES]
                                    + bg_ref[2 * direction + 1:2 * direction + 2, cs])
                a = jnp.exp2(ua * exp2_scale[direction:direction + 1, cs])
                one_minus_a2 = jnp.tanh(ua * neg_log_a_scale[direction:direction + 1, cs]) * (1.0 + a * a)
                mult = jnp.where(one_minus_a2 > 0.0, one_minus_a2 * lax.rsqrt(one_minus_a2), 0.0)
                dense_rows = pl.ds(r0 * SUBLANES + nt * direction + n, row_chunk, stride=SUBLANES)
                a_ref[dense_rows, :] = a
                b_ref[dense_rows, :] = mult * ux * half_y
        return carry

    for c in range(seq // row_chunk):
        coeffs(c, 0)

    is_fwd = lax.broadcasted_iota(jnp.int32, (SUBLANES, LANES), 0) < nt

    block = unroll * SUBLANES

    def scan(c, h):
        rf = pl.multiple_of(c * block, block)
        rb = pl.multiple_of((seq - unroll) * SUBLANES - c * block, block)
        a_f, a_b = a_ref[pl.ds(rf, block), :], a_ref[pl.ds(rb, block), :]
        b_f, b_b = b_ref[pl.ds(rf, block), :], b_ref[pl.ds(rb, block), :]
        def coeff(r):
            up = slice(r * SUBLANES, (r + 1) * SUBLANES)
            down = slice((unroll - 1 - r) * SUBLANES, (unroll - r) * SUBLANES)
            return jnp.where(is_fwd, a_f[up], a_b[down]), jnp.where(is_fwd, b_f[up], b_b[down])

        def emit(r, h_r):
            h_ref[pl.ds(rf + r * SUBLANES, nt), :] = h_r[0:nt]
            h_ref[pl.ds(rb + (unroll - 1 - r) * SUBLANES + nt, nt), :] = h_r[nt:2 * nt]

        for r in range(0, unroll, 2):
            (a0, b0), (a1, b1) = coeff(r), coeff(r + 1)
            emit(r, a0 * h + b0)
            h = (a1 * a0) * h + (a1 * b0 + b1)
            emit(r + 1, h)
        return h

    h0 = jnp.concatenate([h0_ref[d:d + 1, n * LANES:(n + 1) * LANES] for d in range(2) for n in range(nt)], axis=0)
    h = lax.fori_loop(0, seq // unroll, scan, h0)
    for n in range(nt):
        st_ref[0:1, n * LANES:(n + 1) * LANES] = h[n:n + 1]
        st_ref[1:2, n * LANES:(n + 1) * LANES] = h[nt + n:nt + n + 1]

    def gate(c, carry):
        r0 = pl.multiple_of(c * row_chunk, row_chunk)
        for n in range(nt):
            cs = slice(n * LANES, (n + 1) * LANES)
            hf = h_ref[pl.ds(r0 * SUBLANES + n, row_chunk, stride=SUBLANES), :]
            hb = h_ref[pl.ds(r0 * SUBLANES + nt + n, row_chunk, stride=SUBLANES), :]
            rec_ref[pl.ds(r0, row_chunk), cs] = (
                (hf + hb) * jax.nn.gelu(gr_ref[pl.ds(r0, row_chunk), cs])).astype(rec_ref.dtype)
        return carry

    lax.fori_loop(0, seq // row_chunk, gate, 0)


def _lru(proj, rec, conv_w, conv_b, wg, bg, lam, h0, l, first_row, seq, d_rnn):
    n_seq = h0.shape[0]
    rb0 = first_row // seq
    cg = LRU_TILES * LANES
    ncg = d_rnn // cg
    row_chunk = min(seq, 256)
    aliased = rec is not None
    in_specs = [
        pl.BlockSpec((seq, cg), lambda s, c: (rb0 + s, c)),
        pl.BlockSpec((seq, cg), lambda s, c: (rb0 + s, ncg + c)),
        pl.BlockSpec((None, conv_w.shape[1], cg), lambda s, c: (l, 0, c)),
        pl.BlockSpec((None, 1, cg), lambda s, c: (l, 0, c)),
        pl.BlockSpec((None, LRU_TILES, LANES, 4 * LANES), lambda s, c: (l, c, 0, 0)),
        pl.BlockSpec((None, 4, cg), lambda s, c: (l, 0, c)),
        pl.BlockSpec((None, 2, cg), lambda s, c: (l, 0, c)),
        pl.BlockSpec((None, 2, cg), lambda s, c: (s, 0, c)),
    ]
    args = [proj, proj, conv_w, conv_b, wg, bg, lam, h0]
    if aliased:
        in_specs.append(pl.BlockSpec(memory_space=pl.ANY))
        args.append(rec)
    return pl.pallas_call(
        functools.partial(_lru_kernel, seq=seq, row_chunk=row_chunk, unroll=16, aliased=aliased),
        out_shape=(jax.ShapeDtypeStruct((proj.shape[0], d_rnn), _BF16),
                   jax.ShapeDtypeStruct((n_seq, 2, d_rnn), _F32)),
        grid=(n_seq, ncg),
        in_specs=in_specs,
        out_specs=(pl.BlockSpec((seq, cg), lambda s, c: (rb0 + s, c)),
                   pl.BlockSpec((None, 2, cg), lambda s, c: (s, 0, c))),
        scratch_shapes=[pltpu.VMEM((LRU_TILES // XP_PITCH, XP_PITCH * (seq + 2 * CONV_HALO), LANES), _F32)]
        + [pltpu.VMEM((seq * SUBLANES, LANES), _F32)] * 3,
        input_output_aliases={8: 0} if aliased else {},
        compiler_params=_params("parallel", "parallel", vmem=VMEM_LIMIT_LARGE),
        name=f"rglru_l{l}_t{seq}",
    )(*args)


Q_SCALE = QK_DIM ** -0.5 * math.log2(math.e)


def _rope(x, cos, sin_signed):
    lane = lax.broadcasted_iota(jnp.int32, x.shape, 1)
    half = QK_DIM // 4
    partner = jnp.where((lane & half) == 0, pltpu.roll(x, LANES - half, 1), pltpu.roll(x, half, 1))
    return x * cos + partner * sin_signed


def _lambda(lam_ref, lam_init):
    lv = lam_ref[...]
    return (jnp.exp(jnp.sum(lv[0:1] * lv[1:2], axis=-1, keepdims=True))
            - jnp.exp(jnp.sum(lv[2:3] * lv[3:4], axis=-1, keepdims=True)) + lam_init)


def _scores(q, kb):
    lane = lax.broadcasted_iota(jnp.int32, q.shape, 1)
    qm = jnp.concatenate([jnp.where(lane < QK_DIM, q, 0.0), jnp.where(lane >= QK_DIM, q, 0.0)], axis=0)
    return lax.dot_general(qm.astype(_BF16), kb, (((1,), (1,)), ((), ())), preferred_element_type=_F32)


def _softmax_diff_pv(s, vb1, lam, g, lam_init):
    tq = s.shape[0] // 2
    e = jnp.exp2(s - jnp.max(s, axis=-1, keepdims=True)).astype(_BF16)
    ov = jnp.dot(e, vb1, preferred_element_type=_F32)
    o = (ov[0:tq, 0:LANES] / ov[0:tq, LANES:2 * LANES]
         - lam * (ov[tq:2 * tq, 0:LANES] / ov[tq:2 * tq, LANES:2 * LANES]))
    return _rms(o, g) * (1.0 - lam_init)


def _attn_ctx_kernel(q_ref, k_ref, v_ref, ck_ref, cv_ref, cq_ref, sq_ref, ckk_ref, skk_ref, lam_ref, g_ref,
                     att_in_ref, o_ref, kbuf, vbuf, *, past, seq, sub, lam_init):
    del att_in_ref

    @pl.when(pl.program_id(2) == 0)
    def _():
        kbuf[0:past, :] = ck_ref[...].astype(kbuf.dtype)
        kbuf[past:past + seq, :] = _rope(k_ref[...], ckk_ref[...], skk_ref[...]).astype(kbuf.dtype)
        vbuf[0:past, 0:LANES] = cv_ref[...].astype(vbuf.dtype)
        vbuf[past:past + seq, 0:LANES] = v_ref[...].astype(vbuf.dtype)
        vbuf[:, LANES:2 * LANES] = jnp.ones((past + seq, LANES), vbuf.dtype)

    lam = _lambda(lam_ref, lam_init)

    def scores(i):
        rows = slice(i * sub, (i + 1) * sub)
        return _scores(_rope(q_ref[rows, :], cq_ref[rows, :], sq_ref[rows, :]) * Q_SCALE, kbuf[...])

    s = scores(0)
    for i in range(q_ref.shape[0] // sub):
        s_next = scores(i + 1) if (i + 1) * sub < q_ref.shape[0] else None
        o = _softmax_diff_pv(s, vbuf[...], lam, g_ref[...], lam_init)
        o_ref[i * sub:(i + 1) * sub, :] = o.astype(o_ref.dtype)
        s = s_next


def _attn_prompt_kernel(*refs, heads, lam_init, aliased):
    if aliased:
        refs = refs[:5] + refs[7:]
    q_ref, k_ref, v_ref, lam_ref, g_ref, o_ref, nk_ref, nv_ref = refs
    nk_ref[...] = k_ref[...]
    nv_ref[...] = v_ref[...]
    lam = _lambda(lam_ref, lam_init)
    ones = jnp.ones((k_ref.shape[0], LANES), _BF16)
    for h in range(heads):
        hs = slice(h * LANES, (h + 1) * LANES)
        vb1 = jnp.concatenate([v_ref[:, hs].astype(_BF16), ones], axis=1)
        s = _scores(q_ref[:, hs] * Q_SCALE, k_ref[:, hs].astype(_BF16))
        o = _softmax_diff_pv(s, vb1, lam, g_ref[...], lam_init)
        o_ref[:, hs] = o.astype(o_ref.dtype)


def _attention_prompt(proj, lamv, gsub, caches, l, depth, n_seq, seq, heads, col_q, lam_init):
    width = heads * LANES
    cq = col_q // width
    aliased = caches is not None
    in_specs = [
        pl.BlockSpec((seq, width), lambda b: (b, cq)),
        pl.BlockSpec((seq, width), lambda b: (b, cq + 1)),
        pl.BlockSpec((seq, width), lambda b: (b, cq + 2)),
        pl.BlockSpec((None, 4, QK_DIM), lambda b: (l, 0, 0)),
        pl.BlockSpec((None, 1, LANES), lambda b: (l, 0, 0)),
    ]
    args = [proj, proj, proj, lamv, gsub]
    if aliased:
        in_specs += [pl.BlockSpec(memory_space=pl.ANY)] * 2
        args += list(caches)
    cache_sds = jax.ShapeDtypeStruct((n_seq, depth, seq, width), _F32)
    return pl.pallas_call(
        functools.partial(_attn_prompt_kernel, heads=heads, lam_init=lam_init, aliased=aliased),
        out_shape=(jax.ShapeDtypeStruct((proj.shape[0], width), _BF16), cache_sds, cache_sds),
        grid=(n_seq,),
        in_specs=in_specs,
        out_specs=(pl.BlockSpec((seq, width), lambda b: (b, 0)),
                   pl.BlockSpec((None, None, seq, width), lambda b: (b, l, 0, 0)),
                   pl.BlockSpec((None, None, seq, width), lambda b: (b, l, 0, 0))),
        input_output_aliases={5: 1, 6: 2} if aliased else {},
        compiler_params=_params("parallel"),
        name=f"diffattn_prompt_l{l}",
    )(*args)


def _attention_ctx(proj, att, lamv, gsub, cache_k, cache_v, cos, sin, l, first_row, n_seq, seq, tq, heads,
                   col_q, lam_init):
    past = cache_k.shape[2]
    rq0, rk0 = first_row // tq, first_row // seq
    nq = seq // tq
    hq, hk, hv = col_q // LANES, col_q // LANES + heads, col_q // LANES + 2 * heads
    in_specs = [
        pl.BlockSpec((tq, LANES), lambda b, h, t: (rq0 + b * nq + t, hq + h)),
        pl.BlockSpec((seq, LANES), lambda b, h, t: (rk0 + b, hk + h)),
        pl.BlockSpec((seq, LANES), lambda b, h, t: (rk0 + b, hv + h)),
        pl.BlockSpec((None, None, past, LANES), lambda b, h, t: (b, l, 0, h)),
        pl.BlockSpec((None, None, past, LANES), lambda b, h, t: (b, l, 0, h)),
        pl.BlockSpec((tq, LANES), lambda b, h, t: (t, 0)),
        pl.BlockSpec((tq, LANES), lambda b, h, t: (t, 0)),
        pl.BlockSpec((seq, LANES), lambda b, h, t: (0, 0)),
        pl.BlockSpec((seq, LANES), lambda b, h, t: (0, 0)),
        pl.BlockSpec((None, 4, QK_DIM), lambda b, h, t: (l, 0, 0)),
        pl.BlockSpec((None, 1, LANES), lambda b, h, t: (l, 0, 0)),
        pl.BlockSpec(memory_space=pl.ANY),
    ]
    return pl.pallas_call(
        functools.partial(_attn_ctx_kernel, past=past, seq=seq, sub=min(tq, 256), lam_init=lam_init),
        out_shape=jax.ShapeDtypeStruct(att.shape, att.dtype),
        grid=(n_seq, heads, nq),
        in_specs=in_specs,
        out_specs=pl.BlockSpec((tq, LANES), lambda b, h, t: (rq0 + b * nq + t, h)),
        scratch_shapes=[pltpu.VMEM((past + seq, LANES), _BF16), pltpu.VMEM((past + seq, 2 * LANES), _BF16)],
        input_output_aliases={11: 0},
        compiler_params=_params("parallel", "parallel", "arbitrary"),
        name=f"diffattn_ctx_l{l}",
    )(proj, proj, proj, cache_k, cache_v, cos, sin, cos, sin, lamv, gsub, att)


def _merge_kernel(x_ref, mod_ref, gpost_ref, rec_ref, att_ref, gm1a_ref, gm1b_ref, gm2a_ref, gm2b_ref,
                  plru_ref, pattn_ref, wout_ref, o_ref, acc_ref, rstd_ref, vec_ref, *, mod_base):
    rec, att = rec_ref[...], att_ref[...]
    tn = gm1a_ref.shape[1]
    for j, (gm1_ref, gm2_ref) in enumerate(((gm1a_ref, gm2a_ref), (gm1b_ref, gm2b_ref))):
        cols = slice(j * tn, (j + 1) * tn)
        a = jnp.dot(rec, plru_ref[:, cols], preferred_element_type=_F32)
        b = jnp.dot(att, pattn_ref[:, cols], preferred_element_type=_F32)
        merged = _sigmoid(gm1_ref[...]) * a + _sigmoid(gm2_ref[...]) * b
        part = jnp.dot(merged.astype(_BF16), wout_ref[cols, :], preferred_element_type=_F32)
        if j == 0:
            acc_ref[...] = part
        else:
            acc_ref[...] += part
    _gated_residual_rows(x_ref, acc_ref, o_ref, rstd_ref, vec_ref, gpost_ref, mod_ref, mod_base + 2, 1.0)


def _merge(x, mod, group_of_tile, gpost, rec, att, proj, col_gm, p_lru, p_attn, w_out, l, mod_base, tm):
    n, d = x.shape
    d_rnn, v_cols = rec.shape[1], att.shape[1]
    tn = d // 2
    g0 = col_gm // tn
    resident = dict(pipeline_mode=pl.Buffered(1))
    return pl.pallas_call(
        functools.partial(_merge_kernel, mod_base=mod_base),
        out_shape=jax.ShapeDtypeStruct((n, d), _F32),
        grid=(n // tm,),
        in_specs=[
            pl.BlockSpec((tm, d), lambda r: (r, 0)),
            pl.BlockSpec((None, N_MOD, d), lambda r: (group_of_tile(r, tm), 0, 0)),
            pl.BlockSpec((1, d), lambda r: (0, 0)),
            pl.BlockSpec((tm, d_rnn), lambda r: (r, 0)),
            pl.BlockSpec((tm, v_cols), lambda r: (r, 0)),
            pl.BlockSpec((tm, tn), lambda r: (r, g0)),
            pl.BlockSpec((tm, tn), lambda r: (r, g0 + 1)),
            pl.BlockSpec((tm, tn), lambda r: (r, g0 + 2)),
            pl.BlockSpec((tm, tn), lambda r: (r, g0 + 3)),
            pl.BlockSpec((None, d_rnn, d), lambda r: (l, 0, 0), **resident),
            pl.BlockSpec((None, v_cols, d), lambda r: (l, 0, 0), **resident),
            pl.BlockSpec((None, d, d), lambda r: (l, 0, 0), **resident),
        ],
        out_specs=pl.BlockSpec((tm, d), lambda r: (r, 0)),
        scratch_shapes=[pltpu.VMEM((tm, d), _F32), pltpu.VMEM((tm, LANES), _F32),
                        pltpu.VMEM((2, SUBLANES, d), _F32)],
        compiler_params=_params("parallel"),
        name=f"merge_l{l}",
    )(x, mod, gpost, rec, att, proj, proj, proj, proj, p_lru, p_attn, w_out)


def _rope_tables(n_tokens):
    t = jnp.arange(n_tokens)
    row = (t // GRID_W).astype(_F32)
    col = (t % GRID_W).astype(_F32)
    n_freq = QK_DIM // 4
    freqs = 1.0 / (ROPE_BASE ** (jnp.arange(0, 2 * n_freq, 2, dtype=_F32) / (2 * n_freq)))
    ar, ac = row[:, None] * freqs, col[:, None] * freqs
    cos = jnp.concatenate([jnp.cos(ar), jnp.cos(ar), jnp.cos(ac), jnp.cos(ac)], axis=-1)
    sin = jnp.concatenate([-jnp.sin(ar), jnp.sin(ar), -jnp.sin(ac), jnp.sin(ac)], axis=-1)
    reps = LANES // QK_DIM
    return jnp.tile(cos, (1, reps)), jnp.tile(sin, (1, reps))


def _largest_tile(candidates, *extents):
    return next(t for t in candidates if all(e % t == 0 for e in extents))


def kernel(x_prompt, x_sample, c, cache_k, cache_v, state_lru, c_ctx, w_mod, b_mod, g_pre, g_post, ffn_w1, ffn_w2, w_in, conv_w, conv_b, lru_wa, lru_ba, lru_wx, lru_bx, lru_lambda, lam_q1, lam_k1, lam_q2, lam_k2, attn_subln, p_lru, p_attn, w_out):
    batch, seq, d = x_prompt.shape
    dec_batch, dec_seq, _ = x_sample.shape
    depth = w_mod.shape[0]
    d_rnn = conv_w.shape[2]
    heads = cache_k.shape[3]
    past = cache_k.shape[2]
    v_cols = heads * cache_v.shape[4]
    qk_cols = heads * cache_k.shape[4]
    d_ff = ffn_w2.shape[2]
    n_prompt, n_sample = batch * seq, dec_batch * dec_seq
    assert cache_k.shape[4] == LANES and cache_v.shape[4] == LANES and 1 + dec_batch <= MOD_ROWS
    col_q = 2 * d_rnn
    col_gm = col_q + 2 * qk_cols + v_cols

    tm = _largest_tile((512, 256), n_prompt, dec_seq)
    tm_in = _largest_tile((1024, 512, 256), n_prompt, dec_seq)
    tq = _largest_tile((2048, 1024, 512, 256, 128), dec_seq)
    tm_merge = _largest_tile((256, 128), n_prompt, dec_seq)
    tf, tn_in = 512, 1024

    def group_of_tile(r, rows):
        first = n_prompt // rows
        return jnp.where(r < first, 0, 1 + (r - first) // (dec_seq // rows))

    ffp = -(-d_ff // tf) * tf
    padc = ((0, 0), (0, 0), (0, 0), (0, ffp - d_ff))
    w1 = (jnp.pad(ffn_w1[..., :d_ff], padc).astype(_BF16), jnp.pad(ffn_w1[..., d_ff:], padc).astype(_BF16))
    w2 = jnp.pad(ffn_w2, ((0, 0), (0, 0), (0, ffp - d_ff), (0, 0))).astype(_BF16)
    w_in_b, p_lru_b, p_attn_b, w_out_b = (w.astype(_BF16) for w in (w_in, p_lru, p_attn, w_out))
    wg = (0.5 * jnp.concatenate([lru_wa[:, 0], lru_wx[:, 0], lru_wa[:, 1], lru_wx[:, 1]], axis=-1)).astype(_BF16)
    bg = 0.5 * jnp.stack([lru_ba[:, 0], lru_bx[:, 0], lru_ba[:, 1], lru_bx[:, 1]], axis=1)
    lamv = jnp.stack([lam_q1, lam_k1, lam_q2, lam_k2], axis=1)
    gsub = attn_subln.reshape(depth, 1, -1)
    conv_b3 = conv_b.reshape(depth, 1, d_rnn)
    cache_k2 = cache_k.reshape(dec_batch, depth, past, qk_cols)
    cache_v2 = cache_v.reshape(dec_batch, depth, past, v_cols)
    cos, sin = _rope_tables(dec_seq)
    zero_state = jnp.zeros((batch, 2, d_rnn), _F32)

    cond = jnp.concatenate([c_ctx[None], c, jnp.zeros((MOD_ROWS - 1 - dec_batch, d), _F32)], axis=0)
    mod_all = _modulation(cond, w_mod, b_mod).reshape(depth, MOD_ROWS, N_MOD, d)

    xs = (x_prompt.reshape(n_prompt, d), x_sample.reshape(n_sample, d))
    caches, new_s = None, []
    for l in range(depth):
        mod = mod_all[l]
        gpre = [g_pre[l, i].reshape(1, d) for i in range(3)]
        gpost = [g_post[l, i].reshape(1, d) for i in range(3)]
        lam_init = 0.8 - 0.6 * math.exp(-0.3 * l)

        if len(xs) == 2:
            x = _ffn(xs, mod, group_of_tile, gpre[0], gpost[0], w1, w2, l, 0, 0, (tm, tm), tf, n_prompt, False)
        else:
            x = _ffn_overlapped(xs[0], mod, group_of_tile, gpre[0], gpost[0], w1, w2, l, 0, 0, tm, tf)
        proj = _inproj(x, mod, group_of_tile, gpre[1], w_in_b, l, 3, tm_in, tn_in)
        rec, st_p = _lru(proj, None, conv_w, conv_b3, wg, bg, lru_lambda, zero_state, l, 0, seq, d_rnn)
        rec, _ = _lru(proj, rec, conv_w, conv_b3, wg, bg, lru_lambda, state_lru[:, l], l, n_prompt, dec_seq, d_rnn)
        att, *caches = _attention_prompt(proj, lamv, gsub, caches, l, depth, batch, seq, heads, col_q, lam_init)
        att = _attention_ctx(proj, att, lamv, gsub, cache_k2, cache_v2, cos, sin, l, n_prompt, dec_batch, dec_seq,
                             tq, heads, col_q, lam_init)
        x = _merge(x, mod, group_of_tile, gpost[1], rec, att, proj, col_gm, p_lru_b, p_attn_b, w_out_b,
                   l, 3, tm_merge)
        last = l == depth - 1
        if last:
            xs = _ffn((x,), mod, group_of_tile, gpre[2], gpost[2], w1, w2, l, 1, 6, (tm, tm), tf, n_prompt, True)
        else:
            xs = (_ffn_overlapped(x, mod, group_of_tile, gpre[2], gpost[2], w1, w2, l, 1, 6, tm, tf),)
        new_s.append(st_p)

    y_prompt = xs[0].reshape(batch, seq, d)
    y_sample = xs[1].reshape(dec_batch, dec_seq, d)
    new_k, new_v = (a.reshape(batch, depth, seq, heads, -1) for a in caches)
    return (y_prompt, y_sample, new_k, new_v, jnp.stack(new_s, axis=1))
```

```python
import functools
import math

import jax
import jax.numpy as jnp
from jax import lax
from jax.experimental import pallas as pl
from jax.experimental.pallas import tpu as pltpu

NORM_EPS = 1e-6
LRU_C = 8.0
N_MOD = 9
GRID_W = 64
ROPE_BASE = 10000.0
QK_DIM = 64
CONV_PAD_LEFT = 2

LANES = 128
SUBLANES = 8
MOD_ROWS = 16
VMEM_LIMIT = 52 * 1024 * 1024
VMEM_LIMIT_LARGE = 58 * 1024 * 1024

_BF16 = jnp.bfloat16
_F32 = jnp.float32


def _rms(x, g):
    ms = jnp.mean(x * x, axis=-1, keepdims=True)
    return x * lax.rsqrt(ms + NORM_EPS) * g


def _params(*sem, vmem=VMEM_LIMIT):
    return pltpu.CompilerParams(dimension_semantics=sem, vmem_limit_bytes=vmem)


def _mod_kernel(c_ref, w_ref, b_ref, o_ref):
    c = c_ref[...]
    s = (c * jax.nn.sigmoid(c)).astype(_BF16)
    o_ref[...] = jnp.dot(s, w_ref[...].astype(_BF16), preferred_element_type=_F32) + b_ref[...]


def _modulation(cond, w_mod, b_mod, tn=1024):
    depth, d, n = w_mod.shape
    return pl.pallas_call(
        _mod_kernel,
        out_shape=jax.ShapeDtypeStruct((depth, MOD_ROWS, n), _F32),
        grid=(depth, n // tn),
        in_specs=[
            pl.BlockSpec((MOD_ROWS, d), lambda l, j: (0, 0)),
            pl.BlockSpec((None, d, tn), lambda l, j: (l, 0, j)),
            pl.BlockSpec((None, 1, tn), lambda l, j: (l, 0, j)),
        ],
        out_specs=pl.BlockSpec((None, MOD_ROWS, tn), lambda l, j: (l, 0, j)),
        compiler_params=_params("parallel", "parallel"),
        name="modulation",
    )(cond, w_mod, b_mod.reshape(depth, 1, n))


NORM_ROWS = 16


def _for_row_chunks(n_rows, fn, pieces=None):
    if pieces is not None:
        pieces.extend(functools.partial(fn, pl.ds(i * NORM_ROWS, NORM_ROWS)) for i in range(n_rows // NORM_ROWS))
        return

    def body(i, carry):
        fn(pl.ds(pl.multiple_of(i * NORM_ROWS, NORM_ROWS), NORM_ROWS))
        return carry

    lax.fori_loop(0, n_rows // NORM_ROWS, body, 0, unroll=2)


def _once(fn, pieces=None):
    fn() if pieces is None else pieces.append(fn)


def _row_rstd(x_ref, rstd_ref, pieces=None):
    n_tiles = x_ref.shape[1] // LANES

    def lane_partial_sums(rows):
        sq = jnp.square(x_ref[rows, :])
        part = sq[:, 0:LANES]
        for t in range(1, n_tiles):
            part = part + sq[:, t * LANES:(t + 1) * LANES]
        rstd_ref[rows, :] = part

    def reduce():
        ms = jnp.sum(rstd_ref[...], axis=-1, keepdims=True) * (1.0 / x_ref.shape[1])
        rstd_ref[...] = jnp.broadcast_to(lax.rsqrt(ms + NORM_EPS), rstd_ref.shape)

    _for_row_chunks(x_ref.shape[0], lane_partial_sums, pieces)
    _once(reduce, pieces)


def _modulated_norm_rows(x_ref, h_ref, rstd_ref, vec_ref, g_ref, mod_ref, mod_base, pieces=None):
    n_tiles = x_ref.shape[1] // LANES

    def vectors():
        sh = mod_ref[mod_base:mod_base + 1, :]
        sc = mod_ref[mod_base + 1:mod_base + 2, :]
        vec_ref[0] = jnp.broadcast_to(g_ref[...] * (1.0 + sc), vec_ref.shape[1:])
        vec_ref[1] = jnp.broadcast_to(sh, vec_ref.shape[1:])

    def normalise(rows):
        rstd = jnp.tile(rstd_ref[rows, :], (1, n_tiles))
        scale, shift = (jnp.tile(vec_ref[k], (NORM_ROWS // SUBLANES, 1)) for k in range(2))
        h_ref[rows, :] = (x_ref[rows, :] * rstd * scale + shift).astype(h_ref.dtype)

    _row_rstd(x_ref, rstd_ref, pieces)
    _once(vectors, pieces)
    _for_row_chunks(x_ref.shape[0], normalise, pieces)


def _gated_residual_rows(x_ref, acc_ref, o_ref, rstd_ref, vec_ref, g_ref, mod_ref, gate_row, gate_scale,
                         pieces=None):
    n_tiles = x_ref.shape[1] // LANES

    def vectors():
        gt = mod_ref[gate_row:gate_row + 1, :]
        if gate_scale != 1.0:
            gt = gate_scale * gt
        vec_ref[0] = jnp.broadcast_to(gt * g_ref[...], vec_ref.shape[1:])

    def residual(rows):
        rstd = jnp.tile(rstd_ref[rows, :], (1, n_tiles))
        scale = jnp.tile(vec_ref[0], (NORM_ROWS // SUBLANES, 1))
        o_ref[rows, :] = x_ref[rows, :] + acc_ref[rows, :] * rstd * scale

    _row_rstd(acc_ref, rstd_ref, pieces)
    _once(vectors, pieces)
    _for_row_chunks(x_ref.shape[0], residual, pieces)


def _on_part(r, n_first, refs_first, refs_second, fn):
    if all(a is b for a, b in zip(refs_first, refs_second)):
        fn(*refs_first)
        return
    pl.when(r < n_first)(lambda: fn(*refs_first))
    pl.when(r >= n_first)(lambda: fn(*refs_second))


def _ffn_kernel(*refs, mod_base, n_first, split_in, split_out):
    refs = list(refs)
    xa_ref = refs.pop(0)
    xb_ref = refs.pop(0) if split_in else xa_ref
    mod_ref, gpre_ref, gpost_ref, w1g_ref, w1u_ref, w2_ref, oa_ref = refs[:7]
    ob_ref = refs[7] if split_out else oa_ref
    h_ref, acc_ref, rstd_ref, vec_ref = refs[-4:]
    r, j = pl.program_id(0), pl.program_id(1)

    @pl.when(j == 0)
    def _():
        _on_part(r, n_first, (xa_ref,), (xb_ref,),
                 lambda x_ref: _modulated_norm_rows(x_ref, h_ref, rstd_ref, vec_ref, gpre_ref, mod_ref,
                                                    mod_base))
        acc_ref[...] = jnp.zeros_like(acc_ref)

    h = h_ref[...]
    g = jnp.dot(h, w1g_ref[...], preferred_element_type=_F32)
    u = jnp.dot(h, w1u_ref[...], preferred_element_type=_F32)
    act = (g * jax.nn.sigmoid(g) * u).astype(_BF16)
    acc_ref[...] += jnp.dot(act, w2_ref[...], preferred_element_type=_F32)

    @pl.when(j == pl.num_programs(1) - 1)
    def _():
        _on_part(r, n_first, (xa_ref, oa_ref), (xb_ref, ob_ref),
                 lambda x_ref, o_ref: _gated_residual_rows(x_ref, acc_ref, o_ref, rstd_ref, vec_ref, gpost_ref,
                                                           mod_ref, mod_base + 2, 0.5))


def _ffn(xs, mod, group_of_tile, gpre, gpost, w1, w2, l, i, mod_base, tm, tf, n_prompt, split_out):
    d = xs[0].shape[1]
    n = sum(x.shape[0] for x in xs)
    split_in = len(xs) == 2
    n_first = n_prompt // tm
    nff = w2.shape[2] // tf

    def first(r, j):
        return jnp.minimum(r, n_first - 1), 0

    def second(r, j):
        return jnp.maximum(r - n_first, 0), 0

    def whole(r, j):
        return r, 0

    x_specs = [pl.BlockSpec((tm, d), first), pl.BlockSpec((tm, d), second)] if split_in else [pl.BlockSpec((tm, d), whole)]
    if split_out:
        out_shape = (jax.ShapeDtypeStruct((n_first * tm, d), _F32), jax.ShapeDtypeStruct((n - n_first * tm, d), _F32))
        out_specs = (pl.BlockSpec((tm, d), first), pl.BlockSpec((tm, d), second))
    else:
        out_shape = jax.ShapeDtypeStruct((n, d), _F32)
        out_specs = pl.BlockSpec((tm, d), whole)
    return pl.pallas_call(
        functools.partial(_ffn_kernel, mod_base=mod_base, n_first=n_first, split_in=split_in, split_out=split_out),
        out_shape=out_shape,
        grid=(n // tm, nff),
        in_specs=x_specs + [
            pl.BlockSpec((None, N_MOD, d), lambda r, j: (group_of_tile(r, tm), 0, 0)),
            pl.BlockSpec((1, d), lambda r, j: (0, 0)),
            pl.BlockSpec((1, d), lambda r, j: (0, 0)),
            pl.BlockSpec((None, None, d, tf), lambda r, j: (l, i, 0, j)),
            pl.BlockSpec((None, None, d, tf), lambda r, j: (l, i, 0, j)),
            pl.BlockSpec((None, None, tf, d), lambda r, j: (l, i, j, 0)),
        ],
        out_specs=out_specs,
        scratch_shapes=[pltpu.VMEM((tm, d), _BF16), pltpu.VMEM((tm, d), _F32), pltpu.VMEM((tm, LANES), _F32),
                        pltpu.VMEM((2, SUBLANES, d), _F32)],
        compiler_params=_params("parallel", "arbitrary"),
        name=f"ffn_l{l}_{i}",
    )(*xs, mod, gpre, gpost, *w1, w2)


def _ffn_overlap_kernel(x_ref, mod_ref, gpre_ref, gpost_ref, w1g_ref, w1u_ref, w2_ref, o_ref,
                        h_ref, h_next_ref, acc_ref, acc_done_ref, rstd_ref, vec_ref, *, mod_base, n_tiles):
    r, j = pl.program_id(0), pl.program_id(1)
    last = pl.num_programs(1) - 1

    def ffn_update(h):
        g = jnp.dot(h, w1g_ref[...], preferred_element_type=_F32)
        u = jnp.dot(h, w1u_ref[...], preferred_element_type=_F32)
        act = (g * jax.nn.sigmoid(g) * u).astype(_BF16)
        return jnp.dot(act, w2_ref[...], preferred_element_type=_F32)

    def run(pieces):
        for piece in pieces:
            piece()

    def finish_previous(pieces=None):
        _gated_residual_rows(x_ref, acc_done_ref, o_ref, rstd_ref, vec_ref, gpost_ref, mod_ref, mod_base + 2, 0.5,
                             pieces)

    def prepare_next(pieces=None):
        _modulated_norm_rows(x_ref, h_next_ref, rstd_ref, vec_ref, gpre_ref, mod_ref, mod_base, pieces)

    @pl.when((r == 0) & (j == 0))
    def _():
        prepare_next()
        acc_done_ref[...] = jnp.zeros_like(acc_done_ref)

    @pl.when(r < n_tiles)
    def _():
        @pl.when(j == 0)
        def _():
            h = h_next_ref[...]
            acc_ref[...] = ffn_update(h)
            h_ref[...] = h
            under_matmul = []
            finish_previous(under_matmul)
            run(under_matmul)

        @pl.when((j > 0) & (j < last))
        def _():
            acc_ref[...] += ffn_update(h_ref[...])

        @pl.when(j == last)
        def _():
            acc_done_ref[...] = acc_ref[...] + ffn_update(h_ref[...])
            under_matmul = []
            prepare_next(under_matmul)
            run(under_matmul)

    @pl.when((r == n_tiles) & (j == 0))
    def _():
        finish_previous()


def _ffn_overlapped(x, mod, group_of_tile, gpre, gpost, w1, w2, l, i, mod_base, tm, tf):
    n, d = x.shape
    n_tiles = n // tm
    nff = w2.shape[2] // tf

    def neighbour(r, j):
        return jnp.where(j == nff - 1, jnp.minimum(r + 1, n_tiles - 1), jnp.maximum(r - 1, 0))

    def weight_tile(r, j):
        return jnp.where(r < n_tiles, j, nff - 1)

    return pl.pallas_call(
        functools.partial(_ffn_overlap_kernel, mod_base=mod_base, n_tiles=n_tiles),
        out_shape=jax.ShapeDtypeStruct((n, d), _F32),
        grid=(n_tiles + 1, nff),
        in_specs=[
            pl.BlockSpec((tm, d), lambda r, j: (neighbour(r, j), 0)),
            pl.BlockSpec((None, N_MOD, d), lambda r, j: (group_of_tile(neighbour(r, j), tm), 0, 0)),
            pl.BlockSpec((1, d), lambda r, j: (0, 0)),
            pl.BlockSpec((1, d), lambda r, j: (0, 0)),
            pl.BlockSpec((None, None, d, tf), lambda r, j: (l, i, 0, weight_tile(r, j))),
            pl.BlockSpec((None, None, d, tf), lambda r, j: (l, i, 0, weight_tile(r, j))),
            pl.BlockSpec((None, None, tf, d), lambda r, j: (l, i, weight_tile(r, j), 0)),
        ],
        out_specs=pl.BlockSpec((tm, d), lambda r, j: (jnp.maximum(r - 1, 0), 0)),
        scratch_shapes=[pltpu.VMEM((tm, d), _BF16)] * 2 + [pltpu.VMEM((tm, d), _F32)] * 2
        + [pltpu.VMEM((tm, LANES), _F32), pltpu.VMEM((2, SUBLANES, d), _F32)],
        compiler_params=_params("arbitrary", "arbitrary"),
        name=f"ffn_l{l}_{i}",
    )(x, mod, gpre, gpost, *w1, w2)


def _inproj_kernel(x_ref, mod_ref, gpre_ref, w_ref, o_ref, h_ref, rstd_ref, vec_ref, *, mod_base):
    @pl.when(pl.program_id(1) == 0)
    def _():
        _modulated_norm_rows(x_ref, h_ref, rstd_ref, vec_ref, gpre_ref, mod_ref, mod_base)

    o_ref[...] = jnp.dot(h_ref[...], w_ref[...], preferred_element_type=_F32)


def _inproj(x, mod, group_of_tile, gpre, w_in, l, mod_base, tm, tn):
    n, d = x.shape
    cols = w_in.shape[2]
    return pl.pallas_call(
        functools.partial(_inproj_kernel, mod_base=mod_base),
        out_shape=jax.ShapeDtypeStruct((n, cols), _F32),
        grid=(n // tm, cols // tn),
        in_specs=[
            pl.BlockSpec((tm, d), lambda r, j: (r, 0)),
            pl.BlockSpec((None, N_MOD, d), lambda r, j: (group_of_tile(r, tm), 0, 0)),
            pl.BlockSpec((1, d), lambda r, j: (0, 0)),
            pl.BlockSpec((None, d, tn), lambda r, j: (l, 0, j)),
        ],
        out_specs=pl.BlockSpec((tm, tn), lambda r, j: (r, j)),
        scratch_shapes=[pltpu.VMEM((tm, d), _BF16), pltpu.VMEM((tm, LANES), _F32),
                        pltpu.VMEM((2, SUBLANES, d), _F32)],
        compiler_params=_params("parallel", "arbitrary"),
        name=f"inproj_l{l}",
    )(x, mod, gpre, w_in)


LRU_TILES = SUBLANES // 2
CONV_HALO = SUBLANES
XP_PITCH = 2


def _sigmoid(x):
    return 0.5 * jnp.tanh(0.5 * x) + 0.5


def _lru_kernel(*refs, seq, row_chunk, unroll, aliased):
    if aliased:
        refs = refs[:8] + refs[9:]
    (xr_ref, gr_ref, cw_ref, cb_ref, wg_ref, bg_ref, lam_ref, h0_ref, rec_ref, st_ref,
     xp_ref, a_ref, b_ref, h_ref) = refs
    nt = LRU_TILES
    halo = jnp.zeros((XP_PITCH * CONV_HALO, LANES), _F32)
    for n in range(nt):
        pair, member = divmod(n, XP_PITCH)
        xp_ref[pair, 0:XP_PITCH * CONV_HALO, :] = halo
        xp_ref[pair, XP_PITCH * (CONV_HALO + seq):XP_PITCH * (2 * CONV_HALO + seq), :] = halo
        xp_ref[pair, pl.ds(XP_PITCH * CONV_HALO + member, seq, stride=XP_PITCH), :] = (
            xr_ref[:, n * LANES:(n + 1) * LANES])

    neg_log_a_scale = (0.5 * LRU_C) * jax.nn.softplus(-lam_ref[...])
    exp2_scale = -math.log2(math.e) * neg_log_a_scale

    def coeffs(c, carry):
        r0 = c * row_chunk
        for n in range(nt):
            cs = slice(n * LANES, (n + 1) * LANES)
            pair, member = divmod(n, XP_PITCH)
            y = cb_ref[:, cs]
            for tap in range(cw_ref.shape[0]):
                first = XP_PITCH * (r0 + CONV_HALO + tap - CONV_PAD_LEFT) + member
                y = y + cw_ref[tap:tap + 1, cs] * xp_ref[pair, pl.ds(first, row_chunk, stride=XP_PITCH), :]
            g4 = jnp.dot(y.astype(_BF16), wg_ref[n], preferred_element_type=_F32)
            half_y = 0.5 * y
            for direction in range(2):
                ua = 1.0 + jnp.tanh(g4[:, (2 * direction) * LANES:(2 * direction + 1) * LANES]
                                    + bg_ref[2 * direction:2 * direction + 1, cs])
                ux = 1.0 + jnp.tanh(g4[:, (2 * direction + 1) * LANES:(2 * direction + 2) * LANES]
                                    + bg_ref[2 * direction + 1:2 * direction + 2, cs])
                a = jnp.exp2(ua * exp2_scale[direction:direction + 1, cs])
                one_minus_a2 = jnp.tanh(ua * neg_log_a_scale[direction:direction + 1, cs]) * (1.0 + a * a)
                mult = jnp.where(one_minus_a2 > 0.0, one_minus_a2 * lax.rsqrt(one_minus_a2), 0.0)
                dense_rows = pl.ds(r0 * SUBLANES + nt * direction + n, row_chunk, stride=SUBLANES)
                a_ref[dense_rows, :] = a
                b_ref[dense_rows, :] = mult * ux * half_y
        return carry

    for c in range(seq // row_chunk):
        coeffs(c, 0)

    is_fwd = lax.broadcasted_iota(jnp.int32, (SUBLANES, LANES), 0) < nt

    block = unroll * SUBLANES

    def scan(c, h):
        rf = pl.multiple_of(c * block, block)
        rb = pl.multiple_of((seq - unroll) * SUBLANES - c * block, block)
        a_f, a_b = a_ref[pl.ds(rf, block), :], a_ref[pl.ds(rb, block), :]
        b_f, b_b = b_ref[pl.ds(rf, block), :], b_ref[pl.ds(rb, block), :]
        def coeff(r):
            up = slice(r * SUBLANES, (r + 1) * SUBLANES)
            down = slice((unroll - 1 - r) * SUBLANES, (unroll - r) * SUBLANES)
            return jnp.where(is_fwd, a_f[up], a_b[down]), jnp.where(is_fwd, b_f[up], b_b[down])

        def emit(r, h_r):
            h_ref[pl.ds(rf + r * SUBLANES, nt), :] = h_r[0:nt]
            h_ref[pl.ds(rb + (unroll - 1 - r) * SUBLANES + nt, nt), :] = h_r[nt:2 * nt]

        for r in range(0, unroll, 2):
            (a0, b0), (a1, b1) = coeff(r), coeff(r + 1)
            emit(r, a0 * h + b0)
            h = (a1 * a0) * h + (a1 * b0 + b1)
            emit(r + 1, h)
        return h

    h0 = jnp.concatenate([h0_ref[d:d + 1, n * LANES:(n + 1) * LANES] for d in range(2) for n in range(nt)], axis=0)
    h = lax.fori_loop(0, seq // unroll, scan, h0)
    for n in range(nt):
        st_ref[0:1, n * LANES:(n + 1) * LANES] = h[n:n + 1]
        st_ref[1:2, n * LANES:(n + 1) * LANES] = h[nt + n:nt + n + 1]

    def gate(c, carry):
        r0 = pl.multiple_of(c * row_chunk, row_chunk)
        for n in range(nt):
            cs = slice(n * LANES, (n + 1) * LANES)
            hf = h_ref[pl.ds(r0 * SUBLANES + n, row_chunk, stride=SUBLANES), :]
            hb = h_ref[pl.ds(r0 * SUBLANES + nt + n, row_chunk, stride=SUBLANES), :]
            rec_ref[pl.ds(r0, row_chunk), cs] = (
                (hf + hb) * jax.nn.gelu(gr_ref[pl.ds(r0, row_chunk), cs])).astype(rec_ref.dtype)
        return carry

    lax.fori_loop(0, seq // row_chunk, gate, 0)


def _lru(proj, rec, conv_w, conv_b, wg, bg, lam, h0, l, first_row, seq, d_rnn):
    n_seq = h0.shape[0]
    rb0 = first_row // seq
    cg = LRU_TILES * LANES
    ncg = d_rnn // cg
    row_chunk = min(seq, 256)
    aliased = rec is not None
    in_specs = [
        pl.BlockSpec((seq, cg), lambda s, c: (rb0 + s, c)),
        pl.BlockSpec((seq, cg), lambda s, c: (rb0 + s, ncg + c)),
        pl.BlockSpec((None, conv_w.shape[1], cg), lambda s, c: (l, 0, c)),
        pl.BlockSpec((None, 1, cg), lambda s, c: (l, 0, c)),
        pl.BlockSpec((None, LRU_TILES, LANES, 4 * LANES), lambda s, c: (l, c, 0, 0)),
        pl.BlockSpec((None, 4, cg), lambda s, c: (l, 0, c)),
        pl.BlockSpec((None, 2, cg), lambda s, c: (l, 0, c)),
        pl.BlockSpec((None, 2, cg), lambda s, c: (s, 0, c)),
    ]
    args = [proj, proj, conv_w, conv_b, wg, bg, lam, h0]
    if aliased:
        in_specs.append(pl.BlockSpec(memory_space=pl.ANY))
        args.append(rec)
    return pl.pallas_call(
        functools.partial(_lru_kernel, seq=seq, row_chunk=row_chunk, unroll=16, aliased=aliased),
        out_shape=(jax.ShapeDtypeStruct((proj.shape[0], d_rnn), _BF16),
                   jax.ShapeDtypeStruct((n_seq, 2, d_rnn), _F32)),
        grid=(n_seq, ncg),
        in_specs=in_specs,
        out_specs=(pl.BlockSpec((seq, cg), lambda s, c: (rb0 + s, c)),
                   pl.BlockSpec((None, 2, cg), lambda s, c: (s, 0, c))),
        scratch_shapes=[pltpu.VMEM((LRU_TILES // XP_PITCH, XP_PITCH * (seq + 2 * CONV_HALO), LANES), _F32)]
        + [pltpu.VMEM((seq * SUBLANES, LANES), _F32)] * 3,
        input_output_aliases={8: 0} if aliased else {},
        compiler_params=_params("parallel", "parallel", vmem=VMEM_LIMIT_LARGE),
        name=f"rglru_l{l}_t{seq}",
    )(*args)


Q_SCALE = QK_DIM ** -0.5 * math.log2(math.e)


def _rope(x, cos, sin_signed):
    lane = lax.broadcasted_iota(jnp.int32, x.shape, 1)
    half = QK_DIM // 4
    partner = jnp.where((lane & half) == 0, pltpu.roll(x, LANES - half, 1), pltpu.roll(x, half, 1))
    return x * cos + partner * sin_signed


def _lambda(lam_ref, lam_init):
    lv = lam_ref[...]
    return (jnp.exp(jnp.sum(lv[0:1] * lv[1:2], axis=-1, keepdims=True))
            - jnp.exp(jnp.sum(lv[2:3] * lv[3:4], axis=-1, keepdims=True)) + lam_init)


def _scores(q, kb):
    lane = lax.broadcasted_iota(jnp.int32, q.shape, 1)
    qm = jnp.concatenate([jnp.where(lane < QK_DIM, q, 0.0), jnp.where(lane >= QK_DIM, q, 0.0)], axis=0)
    return lax.dot_general(qm.astype(_BF16), kb, (((1,), (1,)), ((), ())), preferred_element_type=_F32)


def _softmax_diff_pv(s, vb1, lam, g, lam_init):
    tq = s.shape[0] // 2
    e = jnp.exp2(s - jnp.max(s, axis=-1, keepdims=True)).astype(_BF16)
    ov = jnp.dot(e, vb1, preferred_element_type=_F32)
    o = (ov[0:tq, 0:LANES] / ov[0:tq, LANES:2 * LANES]
         - lam * (ov[tq:2 * tq, 0:LANES] / ov[tq:2 * tq, LANES:2 * LANES]))
    return _rms(o, g) * (1.0 - lam_init)


def _attn_ctx_kernel(q_ref, k_ref, v_ref, ck_ref, cv_ref, cq_ref, sq_ref, ckk_ref, skk_ref, lam_ref, g_ref,
                     att_in_ref, o_ref, kbuf, vbuf, *, past, seq, sub, lam_init):
    del att_in_ref

    @pl.when(pl.program_id(2) == 0)
    def _():
        kbuf[0:past, :] = ck_ref[...].astype(kbuf.dtype)
        kbuf[past:past + seq, :] = _rope(k_ref[...], ckk_ref[...], skk_ref[...]).astype(kbuf.dtype)
        vbuf[0:past, 0:LANES] = cv_ref[...].astype(vbuf.dtype)
        vbuf[past:past + seq, 0:LANES] = v_ref[...].astype(vbuf.dtype)
        vbuf[:, LANES:2 * LANES] = jnp.ones((past + seq, LANES), vbuf.dtype)

    lam = _lambda(lam_ref, lam_init)

    def scores(i):
        rows = slice(i * sub, (i + 1) * sub)
        return _scores(_rope(q_ref[rows, :], cq_ref[rows, :], sq_ref[rows, :]) * Q_SCALE, kbuf[...])

    s = scores(0)
    for i in range(q_ref.shape[0] // sub):
        s_next = scores(i + 1) if (i + 1) * sub < q_ref.shape[0] else None
        o = _softmax_diff_pv(s, vbuf[...], lam, g_ref[...], lam_init)
        o_ref[i * sub:(i + 1) * sub, :] = o.astype(o_ref.dtype)
        s = s_next


def _attn_prompt_kernel(*refs, heads, lam_init, aliased):
    if aliased:
        refs = refs[:5] + refs[7:]
    q_ref, k_ref, v_ref, lam_ref, g_ref, o_ref, nk_ref, nv_ref = refs
    nk_ref[...] = k_ref[...]
    nv_ref[...] = v_ref[...]
    lam = _lambda(lam_ref, lam_init)
    ones = jnp.ones((k_ref.shape[0], LANES), _BF16)
    for h in range(heads):
        hs = slice(h * LANES, (h + 1) * LANES)
        vb1 = jnp.concatenate([v_ref[:, hs].astype(_BF16), ones], axis=1)
        s = _scores(q_ref[:, hs] * Q_SCALE, k_ref[:, hs].astype(_BF16))
        o = _softmax_diff_pv(s, vb1, lam, g_ref[...], lam_init)
        o_ref[:, hs] = o.astype(o_ref.dtype)


def _attention_prompt(proj, lamv, gsub, caches, l, depth, n_seq, seq, heads, col_q, lam_init):
    width = heads * LANES
    cq = col_q // width
    aliased = caches is not None
    in_specs = [
        pl.BlockSpec((seq, width), lambda b: (b, cq)),
        pl.BlockSpec((seq, width), lambda b: (b, cq + 1)),
        pl.BlockSpec((seq, width), lambda b: (b, cq + 2)),
        pl.BlockSpec((None, 4, QK_DIM), lambda b: (l, 0, 0)),
        pl.BlockSpec((None, 1, LANES), lambda b: (l, 0, 0)),
    ]
    args = [proj, proj, proj, lamv, gsub]
    if aliased:
        in_specs += [pl.BlockSpec(memory_space=pl.ANY)] * 2
        args += list(caches)
    cache_sds = jax.ShapeDtypeStruct((n_seq, depth, seq, width), _F32)
    return pl.pallas_call(
        functools.partial(_attn_prompt_kernel, heads=heads, lam_init=lam_init, aliased=aliased),
        out_shape=(jax.ShapeDtypeStruct((proj.shape[0], width), _BF16), cache_sds, cache_sds),
        grid=(n_seq,),
        in_specs=in_specs,
        out_specs=(pl.BlockSpec((seq, width), lambda b: (b, 0)),
                   pl.BlockSpec((None, None, seq, width), lambda b: (b, l, 0, 0)),
                   pl.BlockSpec((None, None, seq, width), lambda b: (b, l, 0, 0))),
        input_output_aliases={5: 1, 6: 2} if aliased else {},
        compiler_params=_params("parallel"),
        name=f"diffattn_prompt_l{l}",
    )(*args)


def _attention_ctx(proj, att, lamv, gsub, cache_k, cache_v, cos, sin, l, first_row, n_seq, seq, tq, heads,
                   col_q, lam_init):
    past = cache_k.shape[2]
    rq0, rk0 = first_row // tq, first_row // seq
    nq = seq // tq
    hq, hk, hv = col_q // LANES, col_q // LANES + heads, col_q // LANES + 2 * heads
    in_specs = [
        pl.BlockSpec((tq, LANES), lambda b, h, t: (rq0 + b * nq + t, hq + h)),
        pl.BlockSpec((seq, LANES), lambda b, h, t: (rk0 + b, hk + h)),
        pl.BlockSpec((seq, LANES), lambda b, h, t: (rk0 + b, hv + h)),
        pl.BlockSpec((None, None, past, LANES), lambda b, h, t: (b, l, 0, h)),
        pl.BlockSpec((None, None, past, LANES), lambda b, h, t: (b, l, 0, h)),
        pl.BlockSpec((tq, LANES), lambda b, h, t: (t, 0)),
        pl.BlockSpec((tq, LANES), lambda b, h, t: (t, 0)),
        pl.BlockSpec((seq, LANES), lambda b, h, t: (0, 0)),
        pl.BlockSpec((seq, LANES), lambda b, h, t: (0, 0)),
        pl.BlockSpec((None, 4, QK_DIM), lambda b, h, t: (l, 0, 0)),
        pl.BlockSpec((None, 1, LANES), lambda b, h, t: (l, 0, 0)),
        pl.BlockSpec(memory_space=pl.ANY),
    ]
    return pl.pallas_call(
        functools.partial(_attn_ctx_kernel, past=past, seq=seq, sub=min(tq, 256), lam_init=lam_init),
        out_shape=jax.ShapeDtypeStruct(att.shape, att.dtype),
        grid=(n_seq, heads, nq),
        in_specs=in_specs,
        out_specs=pl.BlockSpec((tq, LANES), lambda b, h, t: (rq0 + b * nq + t, h)),
        scratch_shapes=[pltpu.VMEM((past + seq, LANES), _BF16), pltpu.VMEM((past + seq, 2 * LANES), _BF16)],
        input_output_aliases={11: 0},
        compiler_params=_params("parallel", "parallel", "arbitrary"),
        name=f"diffattn_ctx_l{l}",
    )(proj, proj, proj, cache_k, cache_v, cos, sin, cos, sin, lamv, gsub, att)


def _merge_kernel(x_ref, mod_ref, gpost_ref, rec_ref, att_ref, gm1a_ref, gm1b_ref, gm2a_ref, gm2b_ref,
                  plru_ref, pattn_ref, wout_ref, o_ref, acc_ref, rstd_ref, vec_ref, *, mod_base):
    rec, att = rec_ref[...], att_ref[...]
    tn = gm1a_ref.shape[1]
    for j, (gm1_ref, gm2_ref) in enumerate(((gm1a_ref, gm2a_ref), (gm1b_ref, gm2b_ref))):
        cols = slice(j * tn, (j + 1) * tn)
        a = jnp.dot(rec, plru_ref[:, cols], preferred_element_type=_F32)
        b = jnp.dot(att, pattn_ref[:, cols], preferred_element_type=_F32)
        merged = _sigmoid(gm1_ref[...]) * a + _sigmoid(gm2_ref[...]) * b
        part = jnp.dot(merged.astype(_BF16), wout_ref[cols, :], preferred_element_type=_F32)
        if j == 0:
            acc_ref[...] = part
        else:
            acc_ref[...] += part
    _gated_residual_rows(x_ref, acc_ref, o_ref, rstd_ref, vec_ref, gpost_ref, mod_ref, mod_base + 2, 1.0)


def _merge(x, mod, group_of_tile, gpost, rec, att, proj, col_gm, p_lru, p_attn, w_out, l, mod_base, tm):
    n, d = x.shape
    d_rnn, v_cols = rec.shape[1], att.shape[1]
    tn = d // 2
    g0 = col_gm // tn
    resident = dict(pipeline_mode=pl.Buffered(1))
    return pl.pallas_call(
        functools.partial(_merge_kernel, mod_base=mod_base),
        out_shape=jax.ShapeDtypeStruct((n, d), _F32),
        grid=(n // tm,),
        in_specs=[
            pl.BlockSpec((tm, d), lambda r: (r, 0)),
            pl.BlockSpec((None, N_MOD, d), lambda r: (group_of_tile(r, tm), 0, 0)),
            pl.BlockSpec((1, d), lambda r: (0, 0)),
            pl.BlockSpec((tm, d_rnn), lambda r: (r, 0)),
            pl.BlockSpec((tm, v_cols), lambda r: (r, 0)),
            pl.BlockSpec((tm, tn), lambda r: (r, g0)),
            pl.BlockSpec((tm, tn), lambda r: (r, g0 + 1)),
            pl.BlockSpec((tm, tn), lambda r: (r, g0 + 2)),
            pl.BlockSpec((tm, tn), lambda r: (r, g0 + 3)),
            pl.BlockSpec((None, d_rnn, d), lambda r: (l, 0, 0), **resident),
            pl.BlockSpec((None, v_cols, d), lambda r: (l, 0, 0), **resident),
            pl.BlockSpec((None, d, d), lambda r: (l, 0, 0), **resident),
        ],
        out_specs=pl.BlockSpec((tm, d), lambda r: (r, 0)),
        scratch_shapes=[pltpu.VMEM((tm, d), _F32), pltpu.VMEM((tm, LANES), _F32),
                        pltpu.VMEM((2, SUBLANES, d), _F32)],
        compiler_params=_params("parallel"),
        name=f"merge_l{l}",
    )(x, mod, gpost, rec, att, proj, proj, proj, proj, p_lru, p_attn, w_out)


def _rope_tables(n_tokens):
    t = jnp.arange(n_tokens)
    row = (t // GRID_W).astype(_F32)
    col = (t % GRID_W).astype(_F32)
    n_freq = QK_DIM // 4
    freqs = 1.0 / (ROPE_BASE ** (jnp.arange(0, 2 * n_freq, 2, dtype=_F32) / (2 * n_freq)))
    ar, ac = row[:, None] * freqs, col[:, None] * freqs
    cos = jnp.concatenate([jnp.cos(ar), jnp.cos(ar), jnp.cos(ac), jnp.cos(ac)], axis=-1)
    sin = jnp.concatenate([-jnp.sin(ar), jnp.sin(ar), -jnp.sin(ac), jnp.sin(ac)], axis=-1)
    reps = LANES // QK_DIM
    return jnp.tile(cos, (1, reps)), jnp.tile(sin, (1, reps))


def _largest_tile(candidates, *extents):
    return next(t for t in candidates if all(e % t == 0 for e in extents))


def kernel(x_prompt, x_sample, c, cache_k, cache_v, state_lru, c_ctx, w_mod, b_mod, g_pre, g_post, ffn_w1, ffn_w2, w_in, conv_w, conv_b, lru_wa, lru_ba, lru_wx, lru_bx, lru_lambda, lam_q1, lam_k1, lam_q2, lam_k2, attn_subln, p_lru, p_attn, w_out):
    batch, seq, d = x_prompt.shape
    dec_batch, dec_seq, _ = x_sample.shape
    depth = w_mod.shape[0]
    d_rnn = conv_w.shape[2]
    heads = cache_k.shape[3]
    past = cache_k.shape[2]
    v_cols = heads * cache_v.shape[4]
    qk_cols = heads * cache_k.shape[4]
    d_ff = ffn_w2.shape[2]
    n_prompt, n_sample = batch * seq, dec_batch * dec_seq
    assert cache_k.shape[4] == LANES and cache_v.shape[4] == LANES and 1 + dec_batch <= MOD_ROWS
    col_q = 2 * d_rnn
    col_gm = col_q + 2 * qk_cols + v_cols

    tm = _largest_tile((512, 256), n_prompt, dec_seq)
    tm_in = _largest_tile((1024, 512, 256), n_prompt, dec_seq)
    tm_merge = _largest_tile((256, 128), n_prompt, dec_seq)
    tq = _largest_tile((2048, 1024, 512, 256, 128), dec_seq)
    tf, tn_in = 512, 1024

    def group_of_tile(r, rows):
        first = n_prompt // rows
        return jnp.where(r < first, 0, 1 + (r - first) // (dec_seq // rows))

    ffp = -(-d_ff // tf) * tf
    padc = ((0, 0), (0, 0), (0, 0), (0, ffp - d_ff))
    w1 = (jnp.pad(ffn_w1[..., :d_ff], padc).astype(_BF16), jnp.pad(ffn_w1[..., d_ff:], padc).astype(_BF16))
    w2 = jnp.pad(ffn_w2, ((0, 0), (0, 0), (0, ffp - d_ff), (0, 0))).astype(_BF16)
    w_in_b, p_lru_b, p_attn_b, w_out_b = (w.astype(_BF16) for w in (w_in, p_lru, p_attn, w_out))
    wg = (0.5 * jnp.concatenate([lru_wa[:, 0], lru_wx[:, 0], lru_wa[:, 1], lru_wx[:, 1]], axis=-1)).astype(_BF16)
    bg = 0.5 * jnp.stack([lru_ba[:, 0], lru_bx[:, 0], lru_ba[:, 1], lru_bx[:, 1]], axis=1)
    lamv = jnp.stack([lam_q1, lam_k1, lam_q2, lam_k2], axis=1)
    gsub = attn_subln.reshape(depth, 1, -1)
    conv_b3 = conv_b.reshape(depth, 1, d_rnn)
    cache_k2 = cache_k.reshape(dec_batch, depth, past, qk_cols)
    cache_v2 = cache_v.reshape(dec_batch, depth, past, v_cols)
    cos, sin = _rope_tables(dec_seq)
    zero_state = jnp.zeros((batch, 2, d_rnn), _F32)

    cond = jnp.concatenate([c_ctx[None], c, jnp.zeros((MOD_ROWS - 1 - dec_batch, d), _F32)], axis=0)
    mod_all = _modulation(cond, w_mod, b_mod).reshape(depth, MOD_ROWS, N_MOD, d)

    xs = (x_prompt.reshape(n_prompt, d), x_sample.reshape(n_sample, d))
    caches, new_s = None, []
    for l in range(depth):
        mod = mod_all[l]
        gpre = [g_pre[l, i].reshape(1, d) for i in range(3)]
        gpost = [g_post[l, i].reshape(1, d) for i in range(3)]
        lam_init = 0.8 - 0.6 * math.exp(-0.3 * l)

        if len(xs) == 2:
            x = _ffn(xs, mod, group_of_tile, gpre[0], gpost[0], w1, w2, l, 0, 0, tm, tf, n_prompt, False)
        else:
            x = _ffn_overlapped(xs[0], mod, group_of_tile, gpre[0], gpost[0], w1, w2, l, 0, 0, tm, tf)
        proj = _inproj(x, mod, group_of_tile, gpre[1], w_in_b, l, 3, tm_in, tn_in)
        rec, st_p = _lru(proj, None, conv_w, conv_b3, wg, bg, lru_lambda, zero_state, l, 0, seq, d_rnn)
        rec, _ = _lru(proj, rec, conv_w, conv_b3, wg, bg, lru_lambda, state_lru[:, l], l, n_prompt, dec_seq, d_rnn)
        att, *caches = _attention_prompt(proj, lamv, gsub, caches, l, depth, batch, seq, heads, col_q, lam_init)
        att = _attention_ctx(proj, att, lamv, gsub, cache_k2, cache_v2, cos, sin, l, n_prompt, dec_batch, dec_seq,
                             tq, heads, col_q, lam_init)
        x = _merge(x, mod, group_of_tile, gpost[1], rec, att, proj, col_gm, p_lru_b, p_attn_b, w_out_b,
                   l, 3, tm_merge)
        last = l == depth - 1
        if last:
            xs = _ffn((x,), mod, group_of_tile, gpre[2], gpost[2], w1, w2, l, 1, 6, tm, tf, n_prompt, True)
        else:
            xs = (_ffn_overlapped(x, mod, group_of_tile, gpre[2], gpost[2], w1, w2, l, 1, 6, tm, tf),)
        new_s.append(st_p)

    y_prompt = xs[0].reshape(batch, seq, d)
    y_sample = xs[1].reshape(dec_batch, dec_seq, d)
    new_k, new_v = (a.reshape(batch, depth, seq, heads, -1) for a in caches)
    return (y_prompt, y_sample, new_k, new_v, jnp.stack(new_s, axis=1))
```

```python
import functools
import math

import jax
import jax.numpy as jnp
from jax import lax
from jax.experimental import pallas as pl
from jax.experimental.pallas import tpu as pltpu

NORM_EPS = 1e-6
LRU_C = 8.0
N_MOD = 9
GRID_W = 64
ROPE_BASE = 10000.0
QK_DIM = 64
CONV_PAD_LEFT = 2

LANES = 128
SUBLANES = 8
MOD_ROWS = 16
VMEM_LIMIT = 52 * 1024 * 1024
VMEM_LIMIT_LARGE = 58 * 1024 * 1024

_BF16 = jnp.bfloat16
_F32 = jnp.float32


def _rms(x, g):
    ms = jnp.mean(x * x, axis=-1, keepdims=True)
    return x * lax.rsqrt(ms + NORM_EPS) * g


def _params(*sem, vmem=VMEM_LIMIT):
    return pltpu.CompilerParams(dimension_semantics=sem, vmem_limit_bytes=vmem)


def _mod_kernel(c_ref, w_ref, b_ref, o_ref):
    c = c_ref[...]
    s = (c * jax.nn.sigmoid(c)).astype(_BF16)
    o_ref[...] = jnp.dot(s, w_ref[...].astype(_BF16), preferred_element_type=_F32) + b_ref[...]


def _modulation(cond, w_mod, b_mod, tn=1024):
    depth, d, n = w_mod.shape
    return pl.pallas_call(
        _mod_kernel,
        out_shape=jax.ShapeDtypeStruct((depth, MOD_ROWS, n), _F32),
        grid=(depth, n // tn),
        in_specs=[
            pl.BlockSpec((MOD_ROWS, d), lambda l, j: (0, 0)),
            pl.BlockSpec((None, d, tn), lambda l, j: (l, 0, j)),
            pl.BlockSpec((None, 1, tn), lambda l, j: (l, 0, j)),
        ],
        out_specs=pl.BlockSpec((None, MOD_ROWS, tn), lambda l, j: (l, 0, j)),
        compiler_params=_params("parallel", "parallel"),
        name="modulation",
    )(cond, w_mod, b_mod.reshape(depth, 1, n))


NORM_ROWS = 16


def _for_row_chunks(n_rows, fn, pieces=None):
    if pieces is not None:
        pieces.extend(functools.partial(fn, pl.ds(i * NORM_ROWS, NORM_ROWS)) for i in range(n_rows // NORM_ROWS))
        return

    def body(i, carry):
        fn(pl.ds(pl.multiple_of(i * NORM_ROWS, NORM_ROWS), NORM_ROWS))
        return carry

    lax.fori_loop(0, n_rows // NORM_ROWS, body, 0, unroll=2)


def _once(fn, pieces=None):
    fn() if pieces is None else pieces.append(fn)


def _row_rstd(x_ref, rstd_ref, pieces=None):
    n_tiles = x_ref.shape[1] // LANES

    def lane_partial_sums(rows):
        sq = jnp.square(x_ref[rows, :])
        part = sq[:, 0:LANES]
        for t in range(1, n_tiles):
            part = part + sq[:, t * LANES:(t + 1) * LANES]
        rstd_ref[rows, :] = part

    def reduce():
        ms = jnp.sum(rstd_ref[...], axis=-1, keepdims=True) * (1.0 / x_ref.shape[1])
        rstd_ref[...] = jnp.broadcast_to(lax.rsqrt(ms + NORM_EPS), rstd_ref.shape)

    _for_row_chunks(x_ref.shape[0], lane_partial_sums, pieces)
    _once(reduce, pieces)


def _modulated_norm_rows(x_ref, h_ref, rstd_ref, vec_ref, g_ref, mod_ref, mod_base, pieces=None):
    n_tiles = x_ref.shape[1] // LANES

    def vectors():
        sh = mod_ref[mod_base:mod_base + 1, :]
        sc = mod_ref[mod_base + 1:mod_base + 2, :]
        vec_ref[0] = jnp.broadcast_to(g_ref[...] * (1.0 + sc), vec_ref.shape[1:])
        vec_ref[1] = jnp.broadcast_to(sh, vec_ref.shape[1:])

    def normalise(rows):
        rstd = jnp.tile(rstd_ref[rows, :], (1, n_tiles))
        scale, shift = (jnp.tile(vec_ref[k], (NORM_ROWS // SUBLANES, 1)) for k in range(2))
        h_ref[rows, :] = (x_ref[rows, :] * rstd * scale + shift).astype(h_ref.dtype)

    _row_rstd(x_ref, rstd_ref, pieces)
    _once(vectors, pieces)
    _for_row_chunks(x_ref.shape[0], normalise, pieces)


def _gated_residual_rows(x_ref, acc_ref, o_ref, rstd_ref, vec_ref, g_ref, mod_ref, gate_row, gate_scale,
                         pieces=None):
    n_tiles = x_ref.shape[1] // LANES

    def vectors():
        gt = mod_ref[gate_row:gate_row + 1, :]
        if gate_scale != 1.0:
            gt = gate_scale * gt
        vec_ref[0] = jnp.broadcast_to(gt * g_ref[...], vec_ref.shape[1:])

    def residual(rows):
        rstd = jnp.tile(rstd_ref[rows, :], (1, n_tiles))
        scale = jnp.tile(vec_ref[0], (NORM_ROWS // SUBLANES, 1))
        o_ref[rows, :] = x_ref[rows, :] + acc_ref[rows, :] * rstd * scale

    _row_rstd(acc_ref, rstd_ref, pieces)
    _once(vectors, pieces)
    _for_row_chunks(x_ref.shape[0], residual, pieces)


def _on_part(r, n_first, refs_first, refs_second, fn):
    if all(a is b for a, b in zip(refs_first, refs_second)):
        fn(*refs_first)
        return
    pl.when(r < n_first)(lambda: fn(*refs_first))
    pl.when(r >= n_first)(lambda: fn(*refs_second))


def _ffn_kernel(*refs, mod_base, n_first, split_in, split_out):
    refs = list(refs)
    xa_ref = refs.pop(0)
    xb_ref = refs.pop(0) if split_in else xa_ref
    mod_ref, gpre_ref, gpost_ref, w1_ref, w2_ref, oa_ref = refs[:6]
    ob_ref = refs[6] if split_out else oa_ref
    h_ref, acc_ref, rstd_ref, vec_ref = refs[-4:]
    r, j = pl.program_id(0), pl.program_id(1)

    @pl.when(j == 0)
    def _():
        _on_part(r, n_first, (xa_ref,), (xb_ref,),
                 lambda x_ref: _modulated_norm_rows(x_ref, h_ref, rstd_ref, vec_ref, gpre_ref, mod_ref,
                                                    mod_base))
        acc_ref[...] = jnp.zeros_like(acc_ref)

    gu = jnp.dot(h_ref[...], w1_ref[...], preferred_element_type=_F32)
    g, u = gu[:, :gu.shape[1] // 2], gu[:, gu.shape[1] // 2:]
    act = (g * jax.nn.sigmoid(g) * u).astype(_BF16)
    acc_ref[...] += jnp.dot(act, w2_ref[...], preferred_element_type=_F32)

    @pl.when(j == pl.num_programs(1) - 1)
    def _():
        _on_part(r, n_first, (xa_ref, oa_ref), (xb_ref, ob_ref),
                 lambda x_ref, o_ref: _gated_residual_rows(x_ref, acc_ref, o_ref, rstd_ref, vec_ref, gpost_ref,
                                                           mod_ref, mod_base + 2, 0.5))


def _ffn(xs, mod, group_of_tile, gpre, gpost, w1, w2, l, i, mod_base, tm, tf, n_prompt, split_out):
    d = xs[0].shape[1]
    n = sum(x.shape[0] for x in xs)
    split_in = len(xs) == 2
    n_first = n_prompt // tm
    nff = w2.shape[2] // tf

    def first(r, j):
        return jnp.minimum(r, n_first - 1), 0

    def second(r, j):
        return jnp.maximum(r - n_first, 0), 0

    def whole(r, j):
        return r, 0

    x_specs = [pl.BlockSpec((tm, d), first), pl.BlockSpec((tm, d), second)] if split_in else [pl.BlockSpec((tm, d), whole)]
    if split_out:
        out_shape = (jax.ShapeDtypeStruct((n_first * tm, d), _F32), jax.ShapeDtypeStruct((n - n_first * tm, d), _F32))
        out_specs = (pl.BlockSpec((tm, d), first), pl.BlockSpec((tm, d), second))
    else:
        out_shape = jax.ShapeDtypeStruct((n, d), _F32)
        out_specs = pl.BlockSpec((tm, d), whole)
    return pl.pallas_call(
        functools.partial(_ffn_kernel, mod_base=mod_base, n_first=n_first, split_in=split_in, split_out=split_out),
        out_shape=out_shape,
        grid=(n // tm, nff),
        in_specs=x_specs + [
            pl.BlockSpec((None, N_MOD, d), lambda r, j: (group_of_tile(r, tm), 0, 0)),
            pl.BlockSpec((1, d), lambda r, j: (0, 0)),
            pl.BlockSpec((1, d), lambda r, j: (0, 0)),
            pl.BlockSpec((None, None, None, d, 2 * tf), lambda r, j: (l, i, j, 0, 0)),
            pl.BlockSpec((None, None, tf, d), lambda r, j: (l, i, j, 0)),
        ],
        out_specs=out_specs,
        scratch_shapes=[pltpu.VMEM((tm, d), _BF16), pltpu.VMEM((tm, d), _F32), pltpu.VMEM((tm, LANES), _F32),
                        pltpu.VMEM((2, SUBLANES, d), _F32)],
        compiler_params=_params("parallel", "arbitrary"),
        name=f"ffn_l{l}_{i}",
    )(*xs, mod, gpre, gpost, w1, w2)


def _ffn_overlap_kernel(x_ref, mod_ref, gpre_ref, gpost_ref, w1_ref, w2_ref, o_ref,
                        h_ref, h_next_ref, acc_ref, acc_done_ref, rstd_ref, vec_ref, *, mod_base, n_tiles):
    r, j = pl.program_id(0), pl.program_id(1)
    last = pl.num_programs(1) - 1

    def ffn_update(h):
        gu = jnp.dot(h, w1_ref[...], preferred_element_type=_F32)
        g, u = gu[:, :gu.shape[1] // 2], gu[:, gu.shape[1] // 2:]
        act = (g * jax.nn.sigmoid(g) * u).astype(_BF16)
        return jnp.dot(act, w2_ref[...], preferred_element_type=_F32)

    def run(pieces):
        for piece in pieces:
            piece()

    def finish_previous(pieces=None):
        _gated_residual_rows(x_ref, acc_done_ref, o_ref, rstd_ref, vec_ref, gpost_ref, mod_ref, mod_base + 2, 0.5,
                             pieces)

    def prepare_next(pieces=None):
        _modulated_norm_rows(x_ref, h_next_ref, rstd_ref, vec_ref, gpre_ref, mod_ref, mod_base, pieces)

    @pl.when((r == 0) & (j == 0))
    def _():
        prepare_next()
        acc_done_ref[...] = jnp.zeros_like(acc_done_ref)

    @pl.when(r < n_tiles)
    def _():
        @pl.when(j == 0)
        def _():
            h = h_next_ref[...]
            acc_ref[...] = ffn_update(h)
            h_ref[...] = h
            under_matmul = []
            finish_previous(under_matmul)
            run(under_matmul)

        @pl.when((j > 0) & (j < last))
        def _():
            acc_ref[...] += ffn_update(h_ref[...])

        @pl.when(j == last)
        def _():
            acc_done_ref[...] = acc_ref[...] + ffn_update(h_ref[...])
            under_matmul = []
            prepare_next(under_matmul)
            run(under_matmul)

    @pl.when((r == n_tiles) & (j == 0))
    def _():
        finish_previous()


def _ffn_overlapped(x, mod, group_of_tile, gpre, gpost, w1, w2, l, i, mod_base, tm, tf):
    n, d = x.shape
    n_tiles = n // tm
    nff = w2.shape[2] // tf

    def neighbour(r, j):
        return jnp.where(j == nff - 1, jnp.minimum(r + 1, n_tiles - 1), jnp.maximum(r - 1, 0))

    def weight_tile(r, j):
        return jnp.where(r < n_tiles, j, nff - 1)

    return pl.pallas_call(
        functools.partial(_ffn_overlap_kernel, mod_base=mod_base, n_tiles=n_tiles),
        out_shape=jax.ShapeDtypeStruct((n, d), _F32),
        grid=(n_tiles + 1, nff),
        in_specs=[
            pl.BlockSpec((tm, d), lambda r, j: (neighbour(r, j), 0)),
            pl.BlockSpec((None, N_MOD, d), lambda r, j: (group_of_tile(neighbour(r, j), tm), 0, 0)),
            pl.BlockSpec((1, d), lambda r, j: (0, 0)),
            pl.BlockSpec((1, d), lambda r, j: (0, 0)),
            pl.BlockSpec((None, None, None, d, 2 * tf), lambda r, j: (l, i, weight_tile(r, j), 0, 0)),
            pl.BlockSpec((None, None, tf, d), lambda r, j: (l, i, weight_tile(r, j), 0)),
        ],
        out_specs=pl.BlockSpec((tm, d), lambda r, j: (jnp.maximum(r - 1, 0), 0)),
        scratch_shapes=[pltpu.VMEM((tm, d), _BF16)] * 2 + [pltpu.VMEM((tm, d), _F32)] * 2
        + [pltpu.VMEM((tm, LANES), _F32), pltpu.VMEM((2, SUBLANES, d), _F32)],
        compiler_params=_params("arbitrary", "arbitrary"),
        name=f"ffn_l{l}_{i}",
    )(x, mod, gpre, gpost, w1, w2)


def _inproj_kernel(x_ref, mod_ref, gpre_ref, w_ref, o_ref, h_ref, rstd_ref, vec_ref, *, mod_base):
    @pl.when(pl.program_id(1) == 0)
    def _():
        _modulated_norm_rows(x_ref, h_ref, rstd_ref, vec_ref, gpre_ref, mod_ref, mod_base)

    o_ref[...] = jnp.dot(h_ref[...], w_ref[...], preferred_element_type=_F32)


def _inproj(x, mod, group_of_tile, gpre, w_in, l, mod_base, tm, tn):
    n, d = x.shape
    cols = w_in.shape[2]
    return pl.pallas_call(
        functools.partial(_inproj_kernel, mod_base=mod_base),
        out_shape=jax.ShapeDtypeStruct((n, cols), _F32),
        grid=(n // tm, cols // tn),
        in_specs=[
            pl.BlockSpec((tm, d), lambda r, j: (r, 0)),
            pl.BlockSpec((None, N_MOD, d), lambda r, j: (group_of_tile(r, tm), 0, 0)),
            pl.BlockSpec((1, d), lambda r, j: (0, 0)),
            pl.BlockSpec((None, d, tn), lambda r, j: (l, 0, j)),
        ],
        out_specs=pl.BlockSpec((tm, tn), lambda r, j: (r, j)),
        scratch_shapes=[pltpu.VMEM((tm, d), _BF16), pltpu.VMEM((tm, LANES), _F32),
                        pltpu.VMEM((2, SUBLANES, d), _F32)],
        compiler_params=_params("parallel", "arbitrary"),
        name=f"inproj_l{l}",
    )(x, mod, gpre, w_in)


LRU_TILES = SUBLANES // 2
CONV_HALO = SUBLANES
XP_PITCH = 2


def _sigmoid(x):
    return 0.5 * jnp.tanh(0.5 * x) + 0.5


def _lru_kernel(*refs, seq, row_chunk, unroll, aliased):
    if aliased:
        refs = refs[:8] + refs[9:]
    (xr_ref, gr_ref, cw_ref, cb_ref, wg_ref, bg_ref, lam_ref, h0_ref, rec_ref, st_ref,
     xp_ref, a_ref, b_ref, h_ref) = refs
    nt = LRU_TILES
    halo = jnp.zeros((XP_PITCH * CONV_HALO, LANES), _F32)
    for n in range(nt):
        pair, member = divmod(n, XP_PITCH)
        xp_ref[pair, 0:XP_PITCH * CONV_HALO, :] = halo
        xp_ref[pair, XP_PITCH * (CONV_HALO + seq):XP_PITCH * (2 * CONV_HALO + seq), :] = halo
        xp_ref[pair, pl.ds(XP_PITCH * CONV_HALO + member, seq, stride=XP_PITCH), :] = (
            xr_ref[:, n * LANES:(n + 1) * LANES])

    neg_log_a_scale = (0.5 * LRU_C) * jax.nn.softplus(-lam_ref[...])
    exp2_scale = -math.log2(math.e) * neg_log_a_scale

    def coeffs(c, carry):
        r0 = c * row_chunk
        for n in range(nt):
            cs = slice(n * LANES, (n + 1) * LANES)
            pair, member = divmod(n, XP_PITCH)
            y = cb_ref[:, cs]
            for tap in range(cw_ref.shape[0]):
                first = XP_PITCH * (r0 + CONV_HALO + tap - CONV_PAD_LEFT) + member
                y = y + cw_ref[tap:tap + 1, cs] * xp_ref[pair, pl.ds(first, row_chunk, stride=XP_PITCH), :]
            g4 = jnp.dot(y.astype(_BF16), wg_ref[n], preferred_element_type=_F32)
            half_y = 0.5 * y
            for direction in range(2):
                ua = 1.0 + jnp.tanh(g4[:, (2 * direction) * LANES:(2 * direction + 1) * LANES]
                                    + bg_ref[2 * direction:2 * direction + 1, cs])
                ux = 1.0 + jnp.tanh(g4[:, (2 * direction + 1) * LANES:(2 * direction + 2) * LANES]
                                    + bg_ref[2 * direction + 1:2 * direction + 2, cs])
                a = jnp.exp2(ua * exp2_scale[direction:direction + 1, cs])
                one_minus_a2 = jnp.tanh(ua * neg_log_a_scale[direction:direction + 1, cs]) * (1.0 + a * a)
                mult = jnp.where(one_minus_a2 > 0.0, one_minus_a2 * lax.rsqrt(one_minus_a2), 0.0)
                dense_rows = pl.ds(r0 * SUBLANES + nt * direction + n, row_chunk, stride=SUBLANES)
                a_ref[dense_rows, :] = a
                b_ref[dense_rows, :] = mult * ux * half_y
        return carry

    for c in range(seq // row_chunk):
        coeffs(c, 0)

    is_fwd = lax.broadcasted_iota(jnp.int32, (SUBLANES, LANES), 0) < nt

    block = unroll * SUBLANES

    def scan(c, h):
        rf = pl.multiple_of(c * block, block)
        rb = pl.multiple_of((seq - unroll) * SUBLANES - c * block, block)
        a_f, a_b = a_ref[pl.ds(rf, block), :], a_ref[pl.ds(rb, block), :]
        b_f, b_b = b_ref[pl.ds(rf, block), :], b_ref[pl.ds(rb, block), :]
        def coeff(r):
            up = slice(r * SUBLANES, (r + 1) * SUBLANES)
            down = slice((unroll - 1 - r) * SUBLANES, (unroll - r) * SUBLANES)
            return jnp.where(is_fwd, a_f[up], a_b[down]), jnp.where(is_fwd, b_f[up], b_b[down])

        def emit(r, h_r):
            h_ref[pl.ds(rf + r * SUBLANES, nt), :] = h_r[0:nt]
            h_ref[pl.ds(rb + (unroll - 1 - r) * SUBLANES + nt, nt), :] = h_r[nt:2 * nt]

        for r in range(0, unroll, 2):
            (a0, b0), (a1, b1) = coeff(r), coeff(r + 1)
            emit(r, a0 * h + b0)
            h = (a1 * a0) * h + (a1 * b0 + b1)
            emit(r + 1, h)
        return h

    h0 = jnp.concatenate([h0_ref[d:d + 1, n * LANES:(n + 1) * LANES] for d in range(2) for n in range(nt)], axis=0)
    h = lax.fori_loop(0, seq // unroll, scan, h0)
    for n in range(nt):
        st_ref[0:1, n * LANES:(n + 1) * LANES] = h[n:n + 1]
        st_ref[1:2, n * LANES:(n + 1) * LANES] = h[nt + n:nt + n + 1]

    def gate(c, carry):
        r0 = pl.multiple_of(c * row_chunk, row_chunk)
        for n in range(nt):
            cs = slice(n * LANES, (n + 1) * LANES)
            hf = h_ref[pl.ds(r0 * SUBLANES + n, row_chunk, stride=SUBLANES), :]
            hb = h_ref[pl.ds(r0 * SUBLANES + nt + n, row_chunk, stride=SUBLANES), :]
            rec_ref[pl.ds(r0, row_chunk), cs] = (
                (hf + hb) * jax.nn.gelu(gr_ref[pl.ds(r0, row_chunk), cs])).astype(rec_ref.dtype)
        return carry

    lax.fori_loop(0, seq // row_chunk, gate, 0)


def _lru(proj, rec, conv_w, conv_b, wg, bg, lam, h0, l, first_row, seq, d_rnn):
    n_seq = h0.shape[0]
    rb0 = first_row // seq
    cg = LRU_TILES * LANES
    ncg = d_rnn // cg
    row_chunk = min(seq, 256)
    aliased = rec is not None
    in_specs = [
        pl.BlockSpec((seq, cg), lambda s, c: (rb0 + s, c)),
        pl.BlockSpec((seq, cg), lambda s, c: (rb0 + s, ncg + c)),
        pl.BlockSpec((None, conv_w.shape[1], cg), lambda s, c: (l, 0, c)),
        pl.BlockSpec((None, 1, cg), lambda s, c: (l, 0, c)),
        pl.BlockSpec((None, LRU_TILES, LANES, 4 * LANES), lambda s, c: (l, c, 0, 0)),
        pl.BlockSpec((None, 4, cg), lambda s, c: (l, 0, c)),
        pl.BlockSpec((None, 2, cg), lambda s, c: (l, 0, c)),
        pl.BlockSpec((None, 2, cg), lambda s, c: (s, 0, c)),
    ]
    args = [proj, proj, conv_w, conv_b, wg, bg, lam, h0]
    if aliased:
        in_specs.append(pl.BlockSpec(memory_space=pl.ANY))
        args.append(rec)
    return pl.pallas_call(
        functools.partial(_lru_kernel, seq=seq, row_chunk=row_chunk, unroll=16, aliased=aliased),
        out_shape=(jax.ShapeDtypeStruct((proj.shape[0], d_rnn), _BF16),
                   jax.ShapeDtypeStruct((n_seq, 2, d_rnn), _F32)),
        grid=(n_seq, ncg),
        in_specs=in_specs,
        out_specs=(pl.BlockSpec((seq, cg), lambda s, c: (rb0 + s, c)),
                   pl.BlockSpec((None, 2, cg), lambda s, c: (s, 0, c))),
        scratch_shapes=[pltpu.VMEM((LRU_TILES // XP_PITCH, XP_PITCH * (seq + 2 * CONV_HALO), LANES), _F32)]
        + [pltpu.VMEM((seq * SUBLANES, LANES), _F32)] * 3,
        input_output_aliases={8: 0} if aliased else {},
        compiler_params=_params("parallel", "parallel", vmem=VMEM_LIMIT_LARGE),
        name=f"rglru_l{l}_t{seq}",
    )(*args)


Q_SCALE = QK_DIM ** -0.5 * math.log2(math.e)


def _rope(x, cos, sin_signed):
    lane = lax.broadcasted_iota(jnp.int32, x.shape, 1)
    half = QK_DIM // 4
    partner = jnp.where((lane & half) == 0, pltpu.roll(x, LANES - half, 1), pltpu.roll(x, half, 1))
    return x * cos + partner * sin_signed


def _lambda(lam_ref, lam_init):
    lv = lam_ref[...]
    return (jnp.exp(jnp.sum(lv[0:1] * lv[1:2], axis=-1, keepdims=True))
            - jnp.exp(jnp.sum(lv[2:3] * lv[3:4], axis=-1, keepdims=True)) + lam_init)


def _scores(q, kb):
    lane = lax.broadcasted_iota(jnp.int32, q.shape, 1)
    qm = jnp.concatenate([jnp.where(lane < QK_DIM, q, 0.0), jnp.where(lane >= QK_DIM, q, 0.0)], axis=0)
    return lax.dot_general(qm.astype(_BF16), kb, (((1,), (1,)), ((), ())), preferred_element_type=_F32)


def _softmax_diff_pv(s, vb1, lam, g, lam_init):
    tq = s.shape[0] // 2
    e = jnp.exp2(s - jnp.max(s, axis=-1, keepdims=True)).astype(_BF16)
    ov = jnp.dot(e, vb1, preferred_element_type=_F32)
    o = (ov[0:tq, 0:LANES] / ov[0:tq, LANES:2 * LANES]
         - lam * (ov[tq:2 * tq, 0:LANES] / ov[tq:2 * tq, LANES:2 * LANES]))
    return _rms(o, g) * (1.0 - lam_init)


def _attn_ctx_kernel(q_ref, k_ref, v_ref, ck_ref, cv_ref, cq_ref, sq_ref, ckk_ref, skk_ref, lam_ref, g_ref,
                     att_in_ref, o_ref, kbuf, vbuf, *, past, seq, sub, lam_init):
    del att_in_ref

    @pl.when(pl.program_id(2) == 0)
    def _():
        kbuf[0:past, :] = ck_ref[...].astype(kbuf.dtype)
        kbuf[past:past + seq, :] = _rope(k_ref[...], ckk_ref[...], skk_ref[...]).astype(kbuf.dtype)
        vbuf[0:past, 0:LANES] = cv_ref[...].astype(vbuf.dtype)
        vbuf[past:past + seq, 0:LANES] = v_ref[...].astype(vbuf.dtype)
        vbuf[:, LANES:2 * LANES] = jnp.ones((past + seq, LANES), vbuf.dtype)

    lam = _lambda(lam_ref, lam_init)

    def scores(i):
        rows = slice(i * sub, (i + 1) * sub)
        return _scores(_rope(q_ref[rows, :], cq_ref[rows, :], sq_ref[rows, :]) * Q_SCALE, kbuf[...])

    s = scores(0)
    for i in range(q_ref.shape[0] // sub):
        s_next = scores(i + 1) if (i + 1) * sub < q_ref.shape[0] else None
        o = _softmax_diff_pv(s, vbuf[...], lam, g_ref[...], lam_init)
        o_ref[i * sub:(i + 1) * sub, :] = o.astype(o_ref.dtype)
        s = s_next


def _attn_prompt_kernel(*refs, heads, lam_init, aliased):
    if aliased:
        refs = refs[:5] + refs[7:]
    q_ref, k_ref, v_ref, lam_ref, g_ref, o_ref, nk_ref, nv_ref = refs
    nk_ref[...] = k_ref[...]
    nv_ref[...] = v_ref[...]
    lam = _lambda(lam_ref, lam_init)
    ones = jnp.ones((k_ref.shape[0], LANES), _BF16)
    for h in range(heads):
        hs = slice(h * LANES, (h + 1) * LANES)
        vb1 = jnp.concatenate([v_ref[:, hs].astype(_BF16), ones], axis=1)
        s = _scores(q_ref[:, hs] * Q_SCALE, k_ref[:, hs].astype(_BF16))
        o = _softmax_diff_pv(s, vb1, lam, g_ref[...], lam_init)
        o_ref[:, hs] = o.astype(o_ref.dtype)


def _attention_prompt(proj, lamv, gsub, caches, l, depth, n_seq, seq, heads, col_q, lam_init):
    width = heads * LANES
    cq = col_q // width
    aliased = caches is not None
    in_specs = [
        pl.BlockSpec((seq, width), lambda b: (b, cq)),
        pl.BlockSpec((seq, width), lambda b: (b, cq + 1)),
        pl.BlockSpec((seq, width), lambda b: (b, cq + 2)),
        pl.BlockSpec((None, 4, QK_DIM), lambda b: (l, 0, 0)),
        pl.BlockSpec((None, 1, LANES), lambda b: (l, 0, 0)),
    ]
    args = [proj, proj, proj, lamv, gsub]
    if aliased:
        in_specs += [pl.BlockSpec(memory_space=pl.ANY)] * 2
        args += list(caches)
    cache_sds = jax.ShapeDtypeStruct((n_seq, depth, seq, width), _F32)
    return pl.pallas_call(
        functools.partial(_attn_prompt_kernel, heads=heads, lam_init=lam_init, aliased=aliased),
        out_shape=(jax.ShapeDtypeStruct((proj.shape[0], width), _BF16), cache_sds, cache_sds),
        grid=(n_seq,),
        in_specs=in_specs,
        out_specs=(pl.BlockSpec((seq, width), lambda b: (b, 0)),
                   pl.BlockSpec((None, None, seq, width), lambda b: (b, l, 0, 0)),
                   pl.BlockSpec((None, None, seq, width), lambda b: (b, l, 0, 0))),
        input_output_aliases={5: 1, 6: 2} if aliased else {},
        compiler_params=_params("parallel"),
        name=f"diffattn_prompt_l{l}",
    )(*args)


def _attention_ctx(proj, att, lamv, gsub, cache_k, cache_v, cos, sin, l, first_row, n_seq, seq, tq, heads,
                   col_q, lam_init):
    past = cache_k.shape[2]
    rq0, rk0 = first_row // tq, first_row // seq
    nq = seq // tq
    hq, hk, hv = col_q // LANES, col_q // LANES + heads, col_q // LANES + 2 * heads
    in_specs = [
        pl.BlockSpec((tq, LANES), lambda b, h, t: (rq0 + b * nq + t, hq + h)),
        pl.BlockSpec((seq, LANES), lambda b, h, t: (rk0 + b, hk + h)),
        pl.BlockSpec((seq, LANES), lambda b, h, t: (rk0 + b, hv + h)),
        pl.BlockSpec((None, None, past, LANES), lambda b, h, t: (b, l, 0, h)),
        pl.BlockSpec((None, None, past, LANES), lambda b, h, t: (b, l, 0, h)),
        pl.BlockSpec((tq, LANES), lambda b, h, t: (t, 0)),
        pl.BlockSpec((tq, LANES), lambda b, h, t: (t, 0)),
        pl.BlockSpec((seq, LANES), lambda b, h, t: (0, 0)),
        pl.BlockSpec((seq, LANES), lambda b, h, t: (0, 0)),
        pl.BlockSpec((None, 4, QK_DIM), lambda b, h, t: (l, 0, 0)),
        pl.BlockSpec((None, 1, LANES), lambda b, h, t: (l, 0, 0)),
        pl.BlockSpec(memory_space=pl.ANY),
    ]
    return pl.pallas_call(
        functools.partial(_attn_ctx_kernel, past=past, seq=seq, sub=min(tq, 256), lam_init=lam_init),
        out_shape=jax.ShapeDtypeStruct(att.shape, att.dtype),
        grid=(n_seq, heads, nq),
        in_specs=in_specs,
        out_specs=pl.BlockSpec((tq, LANES), lambda b, h, t: (rq0 + b * nq + t, h)),
        scratch_shapes=[pltpu.VMEM((past + seq, LANES), _BF16), pltpu.VMEM((past + seq, 2 * LANES), _BF16)],
        input_output_aliases={11: 0},
        compiler_params=_params("parallel", "parallel", "arbitrary"),
        name=f"diffattn_ctx_l{l}",
    )(proj, proj, proj, cache_k, cache_v, cos, sin, cos, sin, lamv, gsub, att)


def _merge_kernel(x_ref, mod_ref, gpost_ref, rec_ref, att_ref, gm1a_ref, gm1b_ref, gm2a_ref, gm2b_ref,
                  plru_ref, pattn_ref, wout_ref, o_ref, acc_ref, rstd_ref, vec_ref, *, mod_base):
    rec, att = rec_ref[...], att_ref[...]
    tn = gm1a_ref.shape[1]
    for j, (gm1_ref, gm2_ref) in enumerate(((gm1a_ref, gm2a_ref), (gm1b_ref, gm2b_ref))):
        cols = slice(j * tn, (j + 1) * tn)
        a = jnp.dot(rec, plru_ref[:, cols], preferred_element_type=_F32)
        b = jnp.dot(att, pattn_ref[:, cols], preferred_element_type=_F32)
        merged = _sigmoid(gm1_ref[...]) * a + _sigmoid(gm2_ref[...]) * b
        part = jnp.dot(merged.astype(_BF16), wout_ref[cols, :], preferred_element_type=_F32)
        if j == 0:
            acc_ref[...] = part
        else:
            acc_ref[...] += part
    _gated_residual_rows(x_ref, acc_ref, o_ref, rstd_ref, vec_ref, gpost_ref, mod_ref, mod_base + 2, 1.0)


def _merge(x, mod, group_of_tile, gpost, rec, att, proj, col_gm, p_lru, p_attn, w_out, l, mod_base, tm):
    n, d = x.shape
    d_rnn, v_cols = rec.shape[1], att.shape[1]
    tn = d // 2
    g0 = col_gm // tn
    resident = dict(pipeline_mode=pl.Buffered(1))
    return pl.pallas_call(
        functools.partial(_merge_kernel, mod_base=mod_base),
        out_shape=jax.ShapeDtypeStruct((n, d), _F32),
        grid=(n // tm,),
        in_specs=[
            pl.BlockSpec((tm, d), lambda r: (r, 0)),
            pl.BlockSpec((None, N_MOD, d), lambda r: (group_of_tile(r, tm), 0, 0)),
            pl.BlockSpec((1, d), lambda r: (0, 0)),
            pl.BlockSpec((tm, d_rnn), lambda r: (r, 0)),
            pl.BlockSpec((tm, v_cols), lambda r: (r, 0)),
            pl.BlockSpec((tm, tn), lambda r: (r, g0)),
            pl.BlockSpec((tm, tn), lambda r: (r, g0 + 1)),
            pl.BlockSpec((tm, tn), lambda r: (r, g0 + 2)),
            pl.BlockSpec((tm, tn), lambda r: (r, g0 + 3)),
            pl.BlockSpec((None, d_rnn, d), lambda r: (l, 0, 0), **resident),
            pl.BlockSpec((None, v_cols, d), lambda r: (l, 0, 0), **resident),
            pl.BlockSpec((None, d, d), lambda r: (l, 0, 0), **resident),
        ],
        out_specs=pl.BlockSpec((tm, d), lambda r: (r, 0)),
        scratch_shapes=[pltpu.VMEM((tm, d), _F32), pltpu.VMEM((tm, LANES), _F32),
                        pltpu.VMEM((2, SUBLANES, d), _F32)],
        compiler_params=_params("parallel"),
        name=f"merge_l{l}",
    )(x, mod, gpost, rec, att, proj, proj, proj, proj, p_lru, p_attn, w_out)


def _rope_tables(n_tokens):
    t = jnp.arange(n_tokens)
    row = (t // GRID_W).astype(_F32)
    col = (t % GRID_W).astype(_F32)
    n_freq = QK_DIM // 4
    freqs = 1.0 / (ROPE_BASE ** (jnp.arange(0, 2 * n_freq, 2, dtype=_F32) / (2 * n_freq)))
    ar, ac = row[:, None] * freqs, col[:, None] * freqs
    cos = jnp.concatenate([jnp.cos(ar), jnp.cos(ar), jnp.cos(ac), jnp.cos(ac)], axis=-1)
    sin = jnp.concatenate([-jnp.sin(ar), jnp.sin(ar), -jnp.sin(ac), jnp.sin(ac)], axis=-1)
    reps = LANES // QK_DIM
    return jnp.tile(cos, (1, reps)), jnp.tile(sin, (1, reps))


def _largest_tile(candidates, *extents):
    return next(t for t in candidates if all(e % t == 0 for e in extents))


def kernel(x_prompt, x_sample, c, cache_k, cache_v, state_lru, c_ctx, w_mod, b_mod, g_pre, g_post, ffn_w1, ffn_w2, w_in, conv_w, conv_b, lru_wa, lru_ba, lru_wx, lru_bx, lru_lambda, lam_q1, lam_k1, lam_q2, lam_k2, attn_subln, p_lru, p_attn, w_out):
    batch, seq, d = x_prompt.shape
    dec_batch, dec_seq, _ = x_sample.shape
    depth = w_mod.shape[0]
    d_rnn = conv_w.shape[2]
    heads = cache_k.shape[3]
    past = cache_k.shape[2]
    v_cols = heads * cache_v.shape[4]
    qk_cols = heads * cache_k.shape[4]
    d_ff = ffn_w2.shape[2]
    n_prompt, n_sample = batch * seq, dec_batch * dec_seq
    assert cache_k.shape[4] == LANES and cache_v.shape[4] == LANES and 1 + dec_batch <= MOD_ROWS
    col_q = 2 * d_rnn
    col_gm = col_q + 2 * qk_cols + v_cols

    tm = _largest_tile((512, 256), n_prompt, dec_seq)
    tm_in = _largest_tile((1024, 512, 256), n_prompt, dec_seq)
    tm_merge = _largest_tile((256, 128), n_prompt, dec_seq)
    tq = _largest_tile((2048, 1024, 512, 256, 128), dec_seq)
    tf, tn_in = 512, 1024

    def group_of_tile(r, rows):
        first = n_prompt // rows
        return jnp.where(r < first, 0, 1 + (r - first) // (dec_seq // rows))

    ffp = -(-d_ff // tf) * tf
    padc = ((0, 0), (0, 0), (0, 0), (0, ffp - d_ff))
    w1 = jnp.concatenate([jnp.pad(ffn_w1[..., :d_ff], padc).reshape(depth, 2, d, ffp // tf, tf),
                          jnp.pad(ffn_w1[..., d_ff:], padc).reshape(depth, 2, d, ffp // tf, tf)], axis=-1)
    w1 = jnp.moveaxis(w1, 3, 2).astype(_BF16)
    w2 = jnp.pad(ffn_w2, ((0, 0), (0, 0), (0, ffp - d_ff), (0, 0))).astype(_BF16)
    w_in_b, p_lru_b, p_attn_b, w_out_b = (w.astype(_BF16) for w in (w_in, p_lru, p_attn, w_out))
    wg = (0.5 * jnp.concatenate([lru_wa[:, 0], lru_wx[:, 0], lru_wa[:, 1], lru_wx[:, 1]], axis=-1)).astype(_BF16)
    bg = 0.5 * jnp.stack([lru_ba[:, 0], lru_bx[:, 0], lru_ba[:, 1], lru_bx[:, 1]], axis=1)
    lamv = jnp.stack([lam_q1, lam_k1, lam_q2, lam_k2], axis=1)
    gsub = attn_subln.reshape(depth, 1, -1)
    conv_b3 = conv_b.reshape(depth, 1, d_rnn)
    cache_k2 = cache_k.reshape(dec_batch, depth, past, qk_cols)
    cache_v2 = cache_v.reshape(dec_batch, depth, past, v_cols)
    cos, sin = _rope_tables(dec_seq)
    zero_state = jnp.zeros((batch, 2, d_rnn), _F32)

    cond = jnp.concatenate([c_ctx[None], c, jnp.zeros((MOD_ROWS - 1 - dec_batch, d), _F32)], axis=0)
    mod_all = _modulation(cond, w_mod, b_mod).reshape(depth, MOD_ROWS, N_MOD, d)

    xs = (x_prompt.reshape(n_prompt, d), x_sample.reshape(n_sample, d))
    caches, new_s = None, []
    for l in range(depth):
        mod = mod_all[l]
        gpre = [g_pre[l, i].reshape(1, d) for i in range(3)]
        gpost = [g_post[l, i].reshape(1, d) for i in range(3)]
        lam_init = 0.8 - 0.6 * math.exp(-0.3 * l)

        if len(xs) == 2:
            x = _ffn(xs, mod, group_of_tile, gpre[0], gpost[0], w1, w2, l, 0, 0, tm, tf, n_prompt, False)
        else:
            x = _ffn_overlapped(xs[0], mod, group_of_tile, gpre[0], gpost[0], w1, w2, l, 0, 0, tm, tf)
        proj = _inproj(x, mod, group_of_tile, gpre[1], w_in_b, l, 3, tm_in, tn_in)
        rec, st_p = _lru(proj, None, conv_w, conv_b3, wg, bg, lru_lambda, zero_state, l, 0, seq, d_rnn)
        rec, _ = _lru(proj, rec, conv_w, conv_b3, wg, bg, lru_lambda, state_lru[:, l], l, n_prompt, dec_seq, d_rnn)
        att, *caches = _attention_prompt(proj, lamv, gsub, caches, l, depth, batch, seq, heads, col_q, lam_init)
        att = _attention_ctx(proj, att, lamv, gsub, cache_k2, cache_v2, cos, sin, l, n_prompt, dec_batch, dec_seq,
                             tq, heads, col_q, lam_init)
        x = _merge(x, mod, group_of_tile, gpost[1], rec, att, proj, col_gm, p_lru_b, p_attn_b, w_out_b,
                   l, 3, tm_merge)
        last = l == depth - 1
        if last:
            xs = _ffn((x,), mod, group_of_tile, gpre[2], gpost[2], w1, w2, l, 1, 6, tm, tf, n_prompt, True)
        else:
            xs = (_ffn_overlapped(x, mod, group_of_tile, gpre[2], gpost[2], w1, w2, l, 1, 6, tm, tf),)
        new_s.append(st_p)

    y_prompt = xs[0].reshape(batch, seq, d)
    y_sample = xs[1].reshape(dec_batch, dec_seq, d)
    new_k, new_v = (a.reshape(batch, depth, seq, heads, -1) for a in caches)
    return (y_prompt, y_sample, new_k, new_v, jnp.stack(new_s, axis=1))
```

```python
import functools
import math

import jax
import jax.numpy as jnp
from jax import lax
from jax.experimental import pallas as pl
from jax.experimental.pallas import tpu as pltpu

NORM_EPS = 1e-6
LRU_C = 8.0
N_MOD = 9
GRID_W = 64
ROPE_BASE = 10000.0
QK_DIM = 64
CONV_PAD_LEFT = 2

LANES = 128
SUBLANES = 8
MOD_ROWS = 16
VMEM_LIMIT = 52 * 1024 * 1024
VMEM_LIMIT_LARGE = 58 * 1024 * 1024

_BF16 = jnp.bfloat16
_F32 = jnp.float32


def _rms(x, g):
    ms = jnp.mean(x * x, axis=-1, keepdims=True)
    return x * lax.rsqrt(ms + NORM_EPS) * g


def _params(*sem, vmem=VMEM_LIMIT):
    return pltpu.CompilerParams(dimension_semantics=sem, vmem_limit_bytes=vmem)


def _mod_kernel(c_ref, w_ref, b_ref, o_ref):
    c = c_ref[...]
    s = (c * jax.nn.sigmoid(c)).astype(_BF16)
    o_ref[...] = jnp.dot(s, w_ref[...].astype(_BF16), preferred_element_type=_F32) + b_ref[...]


def _modulation(cond, w_mod, b_mod, tn=1024):
    depth, d, n = w_mod.shape
    return pl.pallas_call(
        _mod_kernel,
        out_shape=jax.ShapeDtypeStruct((depth, MOD_ROWS, n), _F32),
        grid=(depth, n // tn),
        in_specs=[
            pl.BlockSpec((MOD_ROWS, d), lambda l, j: (0, 0)),
            pl.BlockSpec((None, d, tn), lambda l, j: (l, 0, j)),
            pl.BlockSpec((None, 1, tn), lambda l, j: (l, 0, j)),
        ],
        out_specs=pl.BlockSpec((None, MOD_ROWS, tn), lambda l, j: (l, 0, j)),
        compiler_params=_params("parallel", "parallel"),
        name="modulation",
    )(cond, w_mod, b_mod.reshape(depth, 1, n))


NORM_ROWS = 16


def _for_row_chunks(n_rows, fn, pieces=None):
    if pieces is not None:
        pieces.extend(functools.partial(fn, pl.ds(i * NORM_ROWS, NORM_ROWS)) for i in range(n_rows // NORM_ROWS))
        return

    def body(i, carry):
        fn(pl.ds(pl.multiple_of(i * NORM_ROWS, NORM_ROWS), NORM_ROWS))
        return carry

    lax.fori_loop(0, n_rows // NORM_ROWS, body, 0, unroll=2)


def _once(fn, pieces=None):
    fn() if pieces is None else pieces.append(fn)


def _row_rstd(x_ref, rstd_ref, pieces=None):
    n_tiles = x_ref.shape[1] // LANES

    def lane_partial_sums(rows):
        sq = jnp.square(x_ref[rows, :])
        part = sq[:, 0:LANES]
        for t in range(1, n_tiles):
            part = part + sq[:, t * LANES:(t + 1) * LANES]
        rstd_ref[rows, :] = part

    def reduce():
        ms = jnp.sum(rstd_ref[...], axis=-1, keepdims=True) * (1.0 / x_ref.shape[1])
        rstd_ref[...] = jnp.broadcast_to(lax.rsqrt(ms + NORM_EPS), rstd_ref.shape)

    _for_row_chunks(x_ref.shape[0], lane_partial_sums, pieces)
    _once(reduce, pieces)


def _modulated_norm_rows(x_ref, h_ref, rstd_ref, vec_ref, g_ref, mod_ref, mod_base, pieces=None):
    n_tiles = x_ref.shape[1] // LANES

    def vectors():
        sh = mod_ref[mod_base:mod_base + 1, :]
        sc = mod_ref[mod_base + 1:mod_base + 2, :]
        vec_ref[0] = jnp.broadcast_to(g_ref[...] * (1.0 + sc), vec_ref.shape[1:])
        vec_ref[1] = jnp.broadcast_to(sh, vec_ref.shape[1:])

    def normalise(rows):
        rstd = jnp.tile(rstd_ref[rows, :], (1, n_tiles))
        scale, shift = (jnp.tile(vec_ref[k], (NORM_ROWS // SUBLANES, 1)) for k in range(2))
        h_ref[rows, :] = (x_ref[rows, :] * rstd * scale + shift).astype(h_ref.dtype)

    _row_rstd(x_ref, rstd_ref, pieces)
    _once(vectors, pieces)
    _for_row_chunks(x_ref.shape[0], normalise, pieces)


def _gated_residual_rows(x_ref, acc_ref, o_ref, rstd_ref, vec_ref, g_ref, mod_ref, gate_row, gate_scale,
                         pieces=None):
    n_tiles = x_ref.shape[1] // LANES

    def vectors():
        gt = mod_ref[gate_row:gate_row + 1, :]
        if gate_scale != 1.0:
            gt = gate_scale * gt
        vec_ref[0] = jnp.broadcast_to(gt * g_ref[...], vec_ref.shape[1:])

    def residual(rows):
        rstd = jnp.tile(rstd_ref[rows, :], (1, n_tiles))
        scale = jnp.tile(vec_ref[0], (NORM_ROWS // SUBLANES, 1))
        o_ref[rows, :] = x_ref[rows, :] + acc_ref[rows, :] * rstd * scale

    _row_rstd(acc_ref, rstd_ref, pieces)
    _once(vectors, pieces)
    _for_row_chunks(x_ref.shape[0], residual, pieces)


def _on_part(r, n_first, refs_first, refs_second, fn):
    if all(a is b for a, b in zip(refs_first, refs_second)):
        fn(*refs_first)
        return
    pl.when(r < n_first)(lambda: fn(*refs_first))
    pl.when(r >= n_first)(lambda: fn(*refs_second))


def _ffn_kernel(*refs, mod_base, n_first, split_in, split_out):
    refs = list(refs)
    xa_ref = refs.pop(0)
    xb_ref = refs.pop(0) if split_in else xa_ref
    mod_ref, gpre_ref, gpost_ref, w1g_ref, w1u_ref, w2_ref, oa_ref = refs[:7]
    ob_ref = refs[7] if split_out else oa_ref
    h_ref, acc_ref, rstd_ref, vec_ref = refs[-4:]
    r, j = pl.program_id(0), pl.program_id(1)

    @pl.when(j == 0)
    def _():
        _on_part(r, n_first, (xa_ref,), (xb_ref,),
                 lambda x_ref: _modulated_norm_rows(x_ref, h_ref, rstd_ref, vec_ref, gpre_ref, mod_ref,
                                                    mod_base))
        acc_ref[...] = jnp.zeros_like(acc_ref)

    h = h_ref[...]
    g = jnp.dot(h, w1g_ref[...], preferred_element_type=_F32)
    u = jnp.dot(h, w1u_ref[...], preferred_element_type=_F32)
    act = (g * jax.nn.sigmoid(g) * u).astype(_BF16)
    acc_ref[...] += jnp.dot(act, w2_ref[...], preferred_element_type=_F32)

    @pl.when(j == pl.num_programs(1) - 1)
    def _():
        _on_part(r, n_first, (xa_ref, oa_ref), (xb_ref, ob_ref),
                 lambda x_ref, o_ref: _gated_residual_rows(x_ref, acc_ref, o_ref, rstd_ref, vec_ref, gpost_ref,
                                                           mod_ref, mod_base + 2, 0.5))


def _ffn(xs, mod, group_of_tile, gpre, gpost, w1, w2, l, i, mod_base, tm, tf, n_prompt, split_out):
    d = xs[0].shape[1]
    n = sum(x.shape[0] for x in xs)
    split_in = len(xs) == 2
    n_first = n_prompt // tm
    nff = w2.shape[2] // tf

    def first(r, j):
        return jnp.minimum(r, n_first - 1), 0

    def second(r, j):
        return jnp.maximum(r - n_first, 0), 0

    def whole(r, j):
        return r, 0

    x_specs = [pl.BlockSpec((tm, d), first), pl.BlockSpec((tm, d), second)] if split_in else [pl.BlockSpec((tm, d), whole)]
    if split_out:
        out_shape = (jax.ShapeDtypeStruct((n_first * tm, d), _F32), jax.ShapeDtypeStruct((n - n_first * tm, d), _F32))
        out_specs = (pl.BlockSpec((tm, d), first), pl.BlockSpec((tm, d), second))
    else:
        out_shape = jax.ShapeDtypeStruct((n, d), _F32)
        out_specs = pl.BlockSpec((tm, d), whole)
    return pl.pallas_call(
        functools.partial(_ffn_kernel, mod_base=mod_base, n_first=n_first, split_in=split_in, split_out=split_out),
        out_shape=out_shape,
        grid=(n // tm, nff),
        in_specs=x_specs + [
            pl.BlockSpec((None, N_MOD, d), lambda r, j: (group_of_tile(r, tm), 0, 0)),
            pl.BlockSpec((1, d), lambda r, j: (0, 0)),
            pl.BlockSpec((1, d), lambda r, j: (0, 0)),
            pl.BlockSpec((None, None, d, tf), lambda r, j: (l, i, 0, j)),
            pl.BlockSpec((None, None, d, tf), lambda r, j: (l, i, 0, j)),
            pl.BlockSpec((None, None, tf, d), lambda r, j: (l, i, j, 0)),
        ],
        out_specs=out_specs,
        scratch_shapes=[pltpu.VMEM((tm, d), _BF16), pltpu.VMEM((tm, d), _F32), pltpu.VMEM((tm, LANES), _F32),
                        pltpu.VMEM((2, SUBLANES, d), _F32)],
        compiler_params=_params("parallel", "arbitrary"),
        name=f"ffn_l{l}_{i}",
    )(*xs, mod, gpre, gpost, *w1, w2)


def _ffn_overlap_kernel(x_ref, mod_ref, gpre_ref, gpost_ref, w1g_ref, w1u_ref, w2_ref, o_ref,
                        h_ref, h_next_ref, acc_ref, acc_done_ref, rstd_ref, vec_ref, *, mod_base, n_tiles):
    r, j = pl.program_id(0), pl.program_id(1)
    last = pl.num_programs(1) - 1

    def ffn_update(h):
        g = jnp.dot(h, w1g_ref[...], preferred_element_type=_F32)
        u = jnp.dot(h, w1u_ref[...], preferred_element_type=_F32)
        act = (g * jax.nn.sigmoid(g) * u).astype(_BF16)
        return jnp.dot(act, w2_ref[...], preferred_element_type=_F32)

    def run(pieces):
        for piece in pieces:
            piece()

    def finish_previous(pieces=None):
        _gated_residual_rows(x_ref, acc_done_ref, o_ref, rstd_ref, vec_ref, gpost_ref, mod_ref, mod_base + 2, 0.5,
                             pieces)

    def prepare_next(pieces=None):
        _modulated_norm_rows(x_ref, h_next_ref, rstd_ref, vec_ref, gpre_ref, mod_ref, mod_base, pieces)

    @pl.when((r == 0) & (j == 0))
    def _():
        prepare_next()
        acc_done_ref[...] = jnp.zeros_like(acc_done_ref)

    @pl.when(r < n_tiles)
    def _():
        @pl.when(j == 0)
        def _():
            h = h_next_ref[...]
            acc_ref[...] = ffn_update(h)
            h_ref[...] = h
            under_matmul = []
            finish_previous(under_matmul)
            run(under_matmul)

        @pl.when((j > 0) & (j < last))
        def _():
            acc_ref[...] += ffn_update(h_ref[...])

        @pl.when(j == last)
        def _():
            acc_done_ref[...] = acc_ref[...] + ffn_update(h_ref[...])
            under_matmul = []
            prepare_next(under_matmul)
            run(under_matmul)

    @pl.when((r == n_tiles) & (j == 0))
    def _():
        finish_previous()


def _ffn_overlapped(x, mod, group_of_tile, gpre, gpost, w1, w2, l, i, mod_base, tm, tf):
    n, d = x.shape
    n_tiles = n // tm
    nff = w2.shape[2] // tf

    def neighbour(r, j):
        return jnp.where(j == nff - 1, jnp.minimum(r + 1, n_tiles - 1), jnp.maximum(r - 1, 0))

    def weight_tile(r, j):
        return jnp.where(r < n_tiles, j, nff - 1)

    return pl.pallas_call(
        functools.partial(_ffn_overlap_kernel, mod_base=mod_base, n_tiles=n_tiles),
        out_shape=jax.ShapeDtypeStruct((n, d), _F32),
        grid=(n_tiles + 1, nff),
        in_specs=[
            pl.BlockSpec((tm, d), lambda r, j: (neighbour(r, j), 0)),
            pl.BlockSpec((None, N_MOD, d), lambda r, j: (group_of_tile(neighbour(r, j), tm), 0, 0)),
            pl.BlockSpec((1, d), lambda r, j: (0, 0)),
            pl.BlockSpec((1, d), lambda r, j: (0, 0)),
            pl.BlockSpec((None, None, d, tf), lambda r, j: (l, i, 0, weight_tile(r, j))),
            pl.BlockSpec((None, None, d, tf), lambda r, j: (l, i, 0, weight_tile(r, j))),
            pl.BlockSpec((None, None, tf, d), lambda r, j: (l, i, weight_tile(r, j), 0)),
        ],
        out_specs=pl.BlockSpec((tm, d), lambda r, j: (jnp.maximum(r - 1, 0), 0)),
        scratch_shapes=[pltpu.VMEM((tm, d), _BF16)] * 2 + [pltpu.VMEM((tm, d), _F32)] * 2
        + [pltpu.VMEM((tm, LANES), _F32), pltpu.VMEM((2, SUBLANES, d), _F32)],
        compiler_params=_params("arbitrary", "arbitrary"),
        name=f"ffn_l{l}_{i}",
    )(x, mod, gpre, gpost, *w1, w2)


def _inproj_kernel(x_ref, mod_ref, gpre_ref, w_ref, o_ref, h_ref, rstd_ref, vec_ref, *, mod_base):
    @pl.when(pl.program_id(1) == 0)
    def _():
        _modulated_norm_rows(x_ref, h_ref, rstd_ref, vec_ref, gpre_ref, mod_ref, mod_base)

    o_ref[...] = jnp.dot(h_ref[...], w_ref[...], preferred_element_type=_F32)


def _inproj(x, mod, group_of_tile, gpre, w_in, l, mod_base, tm, tn):
    n, d = x.shape
    cols = w_in.shape[2]
    return pl.pallas_call(
        functools.partial(_inproj_kernel, mod_base=mod_base),
        out_shape=jax.ShapeDtypeStruct((n, cols), _F32),
        grid=(n // tm, cols // tn),
        in_specs=[
            pl.BlockSpec((tm, d), lambda r, j: (r, 0)),
            pl.BlockSpec((None, N_MOD, d), lambda r, j: (group_of_tile(r, tm), 0, 0)),
            pl.BlockSpec((1, d), lambda r, j: (0, 0)),
            pl.BlockSpec((None, d, tn), lambda r, j: (l, 0, j)),
        ],
        out_specs=pl.BlockSpec((tm, tn), lambda r, j: (r, j)),
        scratch_shapes=[pltpu.VMEM((tm, d), _BF16), pltpu.VMEM((tm, LANES), _F32),
                        pltpu.VMEM((2, SUBLANES, d), _F32)],
        compiler_params=_params("parallel", "arbitrary"),
        name=f"inproj_l{l}",
    )(x, mod, gpre, w_in)


LRU_TILES = SUBLANES // 2
CONV_HALO = SUBLANES
XP_PITCH = 2


def _sigmoid(x):
    return 0.5 * jnp.tanh(0.5 * x) + 0.5


def _lru_kernel(*refs, seq, row_chunk, unroll, aliased):
    if aliased:
        refs = refs[:8] + refs[9:]
    (xr_ref, gr_ref, cw_ref, cb_ref, wg_ref, bg_ref, lam_ref, h0_ref, rec_ref, st_ref,
     xp_ref, a_ref, b_ref, h_ref) = refs
    nt = LRU_TILES
    halo = jnp.zeros((XP_PITCH * CONV_HALO, LANES), _F32)
    for n in range(nt):
        pair, member = divmod(n, XP_PITCH)
        xp_ref[pair, 0:XP_PITCH * CONV_HALO, :] = halo
        xp_ref[pair, XP_PITCH * (CONV_HALO + seq):XP_PITCH * (2 * CONV_HALO + seq), :] = halo
        xp_ref[pair, pl.ds(XP_PITCH * CONV_HALO + member, seq, stride=XP_PITCH), :] = (
            xr_ref[:, n * LANES:(n + 1) * LANES])

    neg_log_a_scale = (0.5 * LRU_C) * jax.nn.softplus(-lam_ref[...])
    exp2_scale = -math.log2(math.e) * neg_log_a_scale

    def coeffs(c, carry):
        r0 = c * row_chunk
        for n in range(nt):
            cs = slice(n * LANES, (n + 1) * LANES)
            pair, member = divmod(n, XP_PITCH)
            y = cb_ref[:, cs]
            for tap in range(cw_ref.shape[0]):
                first = XP_PITCH * (r0 + CONV_HALO + tap - CONV_PAD_LEFT) + member
                y = y + cw_ref[tap:tap + 1, cs] * xp_ref[pair, pl.ds(first, row_chunk, stride=XP_PITCH), :]
            g4 = jnp.dot(y.astype(_BF16), wg_ref[n], preferred_element_type=_F32)
            half_y = 0.5 * y
            for direction in range(2):
                ua = 1.0 + jnp.tanh(g4[:, (2 * direction) * LANES:(2 * direction + 1) * LANES]
                                    + bg_ref[2 * direction:2 * direction + 1, cs])
                ux = 1.0 + jnp.tanh(g4[:, (2 * direction + 1) * LANES:(2 * direction + 2) * LANES]
                                    + bg_ref[2 * direction + 1:2 * direction + 2, cs])
                a = jnp.exp2(ua * exp2_scale[direction:direction + 1, cs])
                one_minus_a2 = jnp.tanh(ua * neg_log_a_scale[direction:direction + 1, cs]) * (1.0 + a * a)
                mult = jnp.where(one_minus_a2 > 0.0, one_minus_a2 * lax.rsqrt(one_minus_a2), 0.0)
                dense_rows = pl.ds(r0 * SUBLANES + nt * direction + n, row_chunk, stride=SUBLANES)
                a_ref[dense_rows, :] = a
                b_ref[dense_rows, :] = mult * ux * half_y
        return carry

    for c in range(seq // row_chunk):
        coeffs(c, 0)

    is_fwd = lax.broadcasted_iota(jnp.int32, (SUBLANES, LANES), 0) < nt

    block = unroll * SUBLANES

    def scan(c, h):
        rf = pl.multiple_of(c * block, block)
        rb = pl.multiple_of((seq - unroll) * SUBLANES - c * block, block)
        a_f, a_b = a_ref[pl.ds(rf, block), :], a_ref[pl.ds(rb, block), :]
        b_f, b_b = b_ref[pl.ds(rf, block), :], b_ref[pl.ds(rb, block), :]
        def coeff(r):
            up = slice(r * SUBLANES, (r + 1) * SUBLANES)
            down = slice((unroll - 1 - r) * SUBLANES, (unroll - r) * SUBLANES)
            return jnp.where(is_fwd, a_f[up], a_b[down]), jnp.where(is_fwd, b_f[up], b_b[down])

        def emit(r, h_r):
            h_ref[pl.ds(rf + r * SUBLANES, nt), :] = h_r[0:nt]
            h_ref[pl.ds(rb + (unroll - 1 - r) * SUBLANES + nt, nt), :] = h_r[nt:2 * nt]

        for r in range(0, unroll, 2):
            (a0, b0), (a1, b1) = coeff(r), coeff(r + 1)
            emit(r, a0 * h + b0)
            h = (a1 * a0) * h + (a1 * b0 + b1)
            emit(r + 1, h)
        return h

    h0 = jnp.concatenate([h0_ref[d:d + 1, n * LANES:(n + 1) * LANES] for d in range(2) for n in range(nt)], axis=0)
    h = lax.fori_loop(0, seq // unroll, scan, h0)
    for n in range(nt):
        st_ref[0:1, n * LANES:(n + 1) * LANES] = h[n:n + 1]
        st_ref[1:2, n * LANES:(n + 1) * LANES] = h[nt + n:nt + n + 1]

    def gate(c, carry):
        r0 = pl.multiple_of(c * row_chunk, row_chunk)
        for n in range(nt):
            cs = slice(n * LANES, (n + 1) * LANES)
            hf = h_ref[pl.ds(r0 * SUBLANES + n, row_chunk, stride=SUBLANES), :]
            hb = h_ref[pl.ds(r0 * SUBLANES + nt + n, row_chunk, stride=SUBLANES), :]
            rec_ref[pl.ds(r0, row_chunk), cs] = (
                (hf + hb) * jax.nn.gelu(gr_ref[pl.ds(r0, row_chunk), cs])).astype(rec_ref.dtype)
        return carry

    lax.fori_loop(0, seq // row_chunk, gate, 0)


def _lru(proj, rec, conv_w, conv_b, wg, bg, lam, h0, l, first_row, seq, d_rnn):
    n_seq = h0.shape[0]
    rb0 = first_row // seq
    cg = LRU_TILES * LANES
    ncg = d_rnn // cg
    row_chunk = min(seq, 256)
    aliased = rec is not None
    in_specs = [
        pl.BlockSpec((seq, cg), lambda s, c: (rb0 + s, c)),
        pl.BlockSpec((seq, cg), lambda s, c: (rb0 + s, ncg + c)),
        pl.BlockSpec((None, conv_w.shape[1], cg), lambda s, c: (l, 0, c)),
        pl.BlockSpec((None, 1, cg), lambda s, c: (l, 0, c)),
        pl.BlockSpec((None, LRU_TILES, LANES, 4 * LANES), lambda s, c: (l, c, 0, 0)),
        pl.BlockSpec((None, 4, cg), lambda s, c: (l, 0, c)),
        pl.BlockSpec((None, 2, cg), lambda s, c: (l, 0, c)),
        pl.BlockSpec((None, 2, cg), lambda s, c: (s, 0, c)),
    ]
    args = [proj, proj, conv_w, conv_b, wg, bg, lam, h0]
    if aliased:
        in_specs.append(pl.BlockSpec(memory_space=pl.ANY))
        args.append(rec)
    return pl.pallas_call(
        functools.partial(_lru_kernel, seq=seq, row_chunk=row_chunk, unroll=16, aliased=aliased),
        out_shape=(jax.ShapeDtypeStruct((proj.shape[0], d_rnn), _BF16),
                   jax.ShapeDtypeStruct((n_seq, 2, d_rnn), _F32)),
        grid=(n_seq, ncg),
        in_specs=in_specs,
        out_specs=(pl.BlockSpec((seq, cg), lambda s, c: (rb0 + s, c)),
                   pl.BlockSpec((None, 2, cg), lambda s, c: (s, 0, c))),
        scratch_shapes=[pltpu.VMEM((LRU_TILES // XP_PITCH, XP_PITCH * (seq + 2 * CONV_HALO), LANES), _F32)]
        + [pltpu.VMEM((seq * SUBLANES, LANES), _F32)] * 3,
        input_output_aliases={8: 0} if aliased else {},
        compiler_params=_params("parallel", "parallel", vmem=VMEM_LIMIT_LARGE),
        name=f"rglru_l{l}_t{seq}",
    )(*args)


Q_SCALE = QK_DIM ** -0.5 * math.log2(math.e)


def _rope(x, cos, sin_signed):
    lane = lax.broadcasted_iota(jnp.int32, x.shape, 1)
    half = QK_DIM // 4
    partner = jnp.where((lane & half) == 0, pltpu.roll(x, LANES - half, 1), pltpu.roll(x, half, 1))
    return x * cos + partner * sin_signed


def _lambda(lam_ref, lam_init):
    lv = lam_ref[...]
    return (jnp.exp(jnp.sum(lv[0:1] * lv[1:2], axis=-1, keepdims=True))
            - jnp.exp(jnp.sum(lv[2:3] * lv[3:4], axis=-1, keepdims=True)) + lam_init)


def _scores(q, kb):
    lane = lax.broadcasted_iota(jnp.int32, q.shape, 1)
    qm = jnp.concatenate([jnp.where(lane < QK_DIM, q, 0.0), jnp.where(lane >= QK_DIM, q, 0.0)], axis=0)
    return lax.dot_general(qm.astype(_BF16), kb, (((1,), (1,)), ((), ())), preferred_element_type=_F32)


def _softmax_diff_pv(s, vb1, lam, g, lam_init):
    tq = s.shape[0] // 2
    m = jnp.max(s, axis=-1, keepdims=True)
    keys = s.shape[1]
    halves = 2 if keys % (2 * LANES) == 0 and keys > 4 * LANES else 1
    ov = None
    for c in range(halves):
        ks = slice(c * keys // halves, (c + 1) * keys // halves)
        e = jnp.exp2(s[:, ks] - m).astype(_BF16)
        part = jnp.dot(e, vb1[ks, :], preferred_element_type=_F32)
        ov = part if ov is None else ov + part
    o = (ov[0:tq, 0:LANES] / ov[0:tq, LANES:2 * LANES]
         - lam * (ov[tq:2 * tq, 0:LANES] / ov[tq:2 * tq, LANES:2 * LANES]))
    return _rms(o, g) * (1.0 - lam_init)


def _attn_ctx_kernel(q_ref, k_ref, v_ref, ck_ref, cv_ref, cq_ref, sq_ref, ckk_ref, skk_ref, lam_ref, g_ref,
                     att_in_ref, o_ref, kbuf, vbuf, *, past, seq, sub, lam_init):
    del att_in_ref

    @pl.when(pl.program_id(2) == 0)
    def _():
        kbuf[0:past, :] = ck_ref[...].astype(kbuf.dtype)
        kbuf[past:past + seq, :] = _rope(k_ref[...], ckk_ref[...], skk_ref[...]).astype(kbuf.dtype)
        vbuf[0:past, 0:LANES] = cv_ref[...].astype(vbuf.dtype)
        vbuf[past:past + seq, 0:LANES] = v_ref[...].astype(vbuf.dtype)
        vbuf[:, LANES:2 * LANES] = jnp.ones((past + seq, LANES), vbuf.dtype)

    lam = _lambda(lam_ref, lam_init)

    def scores(i):
        rows = slice(i * sub, (i + 1) * sub)
        return _scores(_rope(q_ref[rows, :], cq_ref[rows, :], sq_ref[rows, :]) * Q_SCALE, kbuf[...])

    s = scores(0)
    for i in range(q_ref.shape[0] // sub):
        s_next = scores(i + 1) if (i + 1) * sub < q_ref.shape[0] else None
        o = _softmax_diff_pv(s, vbuf[...], lam, g_ref[...], lam_init)
        o_ref[i * sub:(i + 1) * sub, :] = o.astype(o_ref.dtype)
        s = s_next


def _attn_prompt_kernel(*refs, heads, lam_init, aliased):
    if aliased:
        refs = refs[:5] + refs[7:]
    q_ref, k_ref, v_ref, lam_ref, g_ref, o_ref, nk_ref, nv_ref = refs
    nk_ref[...] = k_ref[...]
    nv_ref[...] = v_ref[...]
    lam = _lambda(lam_ref, lam_init)
    ones = jnp.ones((k_ref.shape[0], LANES), _BF16)
    for h in range(heads):
        hs = slice(h * LANES, (h + 1) * LANES)
        vb1 = jnp.concatenate([v_ref[:, hs].astype(_BF16), ones], axis=1)
        s = _scores(q_ref[:, hs] * Q_SCALE, k_ref[:, hs].astype(_BF16))
        o = _softmax_diff_pv(s, vb1, lam, g_ref[...], lam_init)
        o_ref[:, hs] = o.astype(o_ref.dtype)


def _attention_prompt(proj, lamv, gsub, caches, l, depth, n_seq, seq, heads, col_q, lam_init):
    width = heads * LANES
    cq = col_q // width
    aliased = caches is not None
    in_specs = [
        pl.BlockSpec((seq, width), lambda b: (b, cq)),
        pl.BlockSpec((seq, width), lambda b: (b, cq + 1)),
        pl.BlockSpec((seq, width), lambda b: (b, cq + 2)),
        pl.BlockSpec((None, 4, QK_DIM), lambda b: (l, 0, 0)),
        pl.BlockSpec((None, 1, LANES), lambda b: (l, 0, 0)),
    ]
    args = [proj, proj, proj, lamv, gsub]
    if aliased:
        in_specs += [pl.BlockSpec(memory_space=pl.ANY)] * 2
        args += list(caches)
    cache_sds = jax.ShapeDtypeStruct((n_seq, depth, seq, width), _F32)
    return pl.pallas_call(
        functools.partial(_attn_prompt_kernel, heads=heads, lam_init=lam_init, aliased=aliased),
        out_shape=(jax.ShapeDtypeStruct((proj.shape[0], width), _BF16), cache_sds, cache_sds),
        grid=(n_seq,),
        in_specs=in_specs,
        out_specs=(pl.BlockSpec((seq, width), lambda b: (b, 0)),
                   pl.BlockSpec((None, None, seq, width), lambda b: (b, l, 0, 0)),
                   pl.BlockSpec((None, None, seq, width), lambda b: (b, l, 0, 0))),
        input_output_aliases={5: 1, 6: 2} if aliased else {},
        compiler_params=_params("parallel"),
        name=f"diffattn_prompt_l{l}",
    )(*args)


def _attention_ctx(proj, att, lamv, gsub, cache_k, cache_v, cos, sin, l, first_row, n_seq, seq, tq, heads,
                   col_q, lam_init):
    past = cache_k.shape[2]
    rq0, rk0 = first_row // tq, first_row // seq
    nq = seq // tq
    hq, hk, hv = col_q // LANES, col_q // LANES + heads, col_q // LANES + 2 * heads
    in_specs = [
        pl.BlockSpec((tq, LANES), lambda b, h, t: (rq0 + b * nq + t, hq + h)),
        pl.BlockSpec((seq, LANES), lambda b, h, t: (rk0 + b, hk + h)),
        pl.BlockSpec((seq, LANES), lambda b, h, t: (rk0 + b, hv + h)),
        pl.BlockSpec((None, None, past, LANES), lambda b, h, t: (b, l, 0, h)),
        pl.BlockSpec((None, None, past, LANES), lambda b, h, t: (b, l, 0, h)),
        pl.BlockSpec((tq, LANES), lambda b, h, t: (t, 0)),
        pl.BlockSpec((tq, LANES), lambda b, h, t: (t, 0)),
        pl.BlockSpec((seq, LANES), lambda b, h, t: (0, 0)),
        pl.BlockSpec((seq, LANES), lambda b, h, t: (0, 0)),
        pl.BlockSpec((None, 4, QK_DIM), lambda b, h, t: (l, 0, 0)),
        pl.BlockSpec((None, 1, LANES), lambda b, h, t: (l, 0, 0)),
        pl.BlockSpec(memory_space=pl.ANY),
    ]
    return pl.pallas_call(
        functools.partial(_attn_ctx_kernel, past=past, seq=seq, sub=min(tq, 256), lam_init=lam_init),
        out_shape=jax.ShapeDtypeStruct(att.shape, att.dtype),
        grid=(n_seq, heads, nq),
        in_specs=in_specs,
        out_specs=pl.BlockSpec((tq, LANES), lambda b, h, t: (rq0 + b * nq + t, h)),
        scratch_shapes=[pltpu.VMEM((past + seq, LANES), _BF16), pltpu.VMEM((past + seq, 2 * LANES), _BF16)],
        input_output_aliases={11: 0},
        compiler_params=_params("parallel", "parallel", "arbitrary"),
        name=f"diffattn_ctx_l{l}",
    )(proj, proj, proj, cache_k, cache_v, cos, sin, cos, sin, lamv, gsub, att)


def _merge_kernel(x_ref, mod_ref, gpost_ref, rec_ref, att_ref, gm1a_ref, gm1b_ref, gm2a_ref, gm2b_ref,
                  plru_ref, pattn_ref, wout_ref, o_ref, acc_ref, rstd_ref, vec_ref, *, mod_base):
    rec, att = rec_ref[...], att_ref[...]
    tn = gm1a_ref.shape[1]
    for j, (gm1_ref, gm2_ref) in enumerate(((gm1a_ref, gm2a_ref), (gm1b_ref, gm2b_ref))):
        cols = slice(j * tn, (j + 1) * tn)
        a = jnp.dot(rec, plru_ref[:, cols], preferred_element_type=_F32)
        b = jnp.dot(att, pattn_ref[:, cols], preferred_element_type=_F32)
        merged = _sigmoid(gm1_ref[...]) * a + _sigmoid(gm2_ref[...]) * b
        part = jnp.dot(merged.astype(_BF16), wout_ref[cols, :], preferred_element_type=_F32)
        if j == 0:
            acc_ref[...] = part
        else:
            acc_ref[...] += part
    _gated_residual_rows(x_ref, acc_ref, o_ref, rstd_ref, vec_ref, gpost_ref, mod_ref, mod_base + 2, 1.0)


def _merge(x, mod, group_of_tile, gpost, rec, att, proj, col_gm, p_lru, p_attn, w_out, l, mod_base, tm):
    n, d = x.shape
    d_rnn, v_cols = rec.shape[1], att.shape[1]
    tn = d // 2
    g0 = col_gm // tn
    resident = dict(pipeline_mode=pl.Buffered(1))
    return pl.pallas_call(
        functools.partial(_merge_kernel, mod_base=mod_base),
        out_shape=jax.ShapeDtypeStruct((n, d), _F32),
        grid=(n // tm,),
        in_specs=[
            pl.BlockSpec((tm, d), lambda r: (r, 0)),
            pl.BlockSpec((None, N_MOD, d), lambda r: (group_of_tile(r, tm), 0, 0)),
            pl.BlockSpec((1, d), lambda r: (0, 0)),
            pl.BlockSpec((tm, d_rnn), lambda r: (r, 0)),
            pl.BlockSpec((tm, v_cols), lambda r: (r, 0)),
            pl.BlockSpec((tm, tn), lambda r: (r, g0)),
            pl.BlockSpec((tm, tn), lambda r: (r, g0 + 1)),
            pl.BlockSpec((tm, tn), lambda r: (r, g0 + 2)),
            pl.BlockSpec((tm, tn), lambda r: (r, g0 + 3)),
            pl.BlockSpec((None, d_rnn, d), lambda r: (l, 0, 0), **resident),
            pl.BlockSpec((None, v_cols, d), lambda r: (l, 0, 0), **resident),
            pl.BlockSpec((None, d, d), lambda r: (l, 0, 0), **resident),
        ],
        out_specs=pl.BlockSpec((tm, d), lambda r: (r, 0)),
        scratch_shapes=[pltpu.VMEM((tm, d), _F32), pltpu.VMEM((tm, LANES), _F32),
                        pltpu.VMEM((2, SUBLANES, d), _F32)],
        compiler_params=_params("parallel"),
        name=f"merge_l{l}",
    )(x, mod, gpost, rec, att, proj, proj, proj, proj, p_lru, p_attn, w_out)


def _rope_tables(n_tokens):
    t = jnp.arange(n_tokens)
    row = (t // GRID_W).astype(_F32)
    col = (t % GRID_W).astype(_F32)
    n_freq = QK_DIM // 4
    freqs = 1.0 / (ROPE_BASE ** (jnp.arange(0, 2 * n_freq, 2, dtype=_F32) / (2 * n_freq)))
    ar, ac = row[:, None] * freqs, col[:, None] * freqs
    cos = jnp.concatenate([jnp.cos(ar), jnp.cos(ar), jnp.cos(ac), jnp.cos(ac)], axis=-1)
    sin = jnp.concatenate([-jnp.sin(ar), jnp.sin(ar), -jnp.sin(ac), jnp.sin(ac)], axis=-1)
    reps = LANES // QK_DIM
    return jnp.tile(cos, (1, reps)), jnp.tile(sin, (1, reps))


def _largest_tile(candidates, *extents):
    return next(t for t in candidates if all(e % t == 0 for e in extents))


def kernel(x_prompt, x_sample, c, cache_k, cache_v, state_lru, c_ctx, w_mod, b_mod, g_pre, g_post, ffn_w1, ffn_w2, w_in, conv_w, conv_b, lru_wa, lru_ba, lru_wx, lru_bx, lru_lambda, lam_q1, lam_k1, lam_q2, lam_k2, attn_subln, p_lru, p_attn, w_out):
    batch, seq, d = x_prompt.shape
    dec_batch, dec_seq, _ = x_sample.shape
    depth = w_mod.shape[0]
    d_rnn = conv_w.shape[2]
    heads = cache_k.shape[3]
    past = cache_k.shape[2]
    v_cols = heads * cache_v.shape[4]
    qk_cols = heads * cache_k.shape[4]
    d_ff = ffn_w2.shape[2]
    n_prompt, n_sample = batch * seq, dec_batch * dec_seq
    assert cache_k.shape[4] == LANES and cache_v.shape[4] == LANES and 1 + dec_batch <= MOD_ROWS
    col_q = 2 * d_rnn
    col_gm = col_q + 2 * qk_cols + v_cols

    tm = _largest_tile((512, 256), n_prompt, dec_seq)
    tm_in = _largest_tile((1024, 512, 256), n_prompt, dec_seq)
    tm_merge = _largest_tile((256, 128), n_prompt, dec_seq)
    tq = _largest_tile((2048, 1024, 512, 256, 128), dec_seq)
    tf, tn_in = 512, 1024

    def group_of_tile(r, rows):
        first = n_prompt // rows
        return jnp.where(r < first, 0, 1 + (r - first) // (dec_seq // rows))

    ffp = -(-d_ff // tf) * tf
    padc = ((0, 0), (0, 0), (0, 0), (0, ffp - d_ff))
    w1 = (jnp.pad(ffn_w1[..., :d_ff], padc).astype(_BF16), jnp.pad(ffn_w1[..., d_ff:], padc).astype(_BF16))
    w2 = jnp.pad(ffn_w2, ((0, 0), (0, 0), (0, ffp - d_ff), (0, 0))).astype(_BF16)
    w_in_b, p_lru_b, p_attn_b, w_out_b = (w.astype(_BF16) for w in (w_in, p_lru, p_attn, w_out))
    wg = (0.5 * jnp.concatenate([lru_wa[:, 0], lru_wx[:, 0], lru_wa[:, 1], lru_wx[:, 1]], axis=-1)).astype(_BF16)
    bg = 0.5 * jnp.stack([lru_ba[:, 0], lru_bx[:, 0], lru_ba[:, 1], lru_bx[:, 1]], axis=1)
    lamv = jnp.stack([lam_q1, lam_k1, lam_q2, lam_k2], axis=1)
    gsub = attn_subln.reshape(depth, 1, -1)
    conv_b3 = conv_b.reshape(depth, 1, d_rnn)
    cache_k2 = cache_k.reshape(dec_batch, depth, past, qk_cols)
    cache_v2 = cache_v.reshape(dec_batch, depth, past, v_cols)
    cos, sin = _rope_tables(dec_seq)
    zero_state = jnp.zeros((batch, 2, d_rnn), _F32)

    cond = jnp.concatenate([c_ctx[None], c, jnp.zeros((MOD_ROWS - 1 - dec_batch, d), _F32)], axis=0)
    mod_all = _modulation(cond, w_mod, b_mod).reshape(depth, MOD_ROWS, N_MOD, d)

    xs = (x_prompt.reshape(n_prompt, d), x_sample.reshape(n_sample, d))
    caches, new_s = None, []
    for l in range(depth):
        mod = mod_all[l]
        gpre = [g_pre[l, i].reshape(1, d) for i in range(3)]
        gpost = [g_post[l, i].reshape(1, d) for i in range(3)]
        lam_init = 0.8 - 0.6 * math.exp(-0.3 * l)

        if len(xs) == 2:
            x = _ffn(xs, mod, group_of_tile, gpre[0], gpost[0], w1, w2, l, 0, 0, tm, tf, n_prompt, False)
        else:
            x = _ffn_overlapped(xs[0], mod, group_of_tile, gpre[0], gpost[0], w1, w2, l, 0, 0, tm, tf)
        proj = _inproj(x, mod, group_of_tile, gpre[1], w_in_b, l, 3, tm_in, tn_in)
        rec, st_p = _lru(proj, None, conv_w, conv_b3, wg, bg, lru_lambda, zero_state, l, 0, seq, d_rnn)
        rec, _ = _lru(proj, rec, conv_w, conv_b3, wg, bg, lru_lambda, state_lru[:, l], l, n_prompt, dec_seq, d_rnn)
        att, *caches = _attention_prompt(proj, lamv, gsub, caches, l, depth, batch, seq, heads, col_q, lam_init)
        att = _attention_ctx(proj, att, lamv, gsub, cache_k2, cache_v2, cos, sin, l, n_prompt, dec_batch, dec_seq,
                             tq, heads, col_q, lam_init)
        x = _merge(x, mod, group_of_tile, gpost[1], rec, att, proj, col_gm, p_lru_b, p_attn_b, w_out_b,
                   l, 3, tm_merge)
        last = l == depth - 1
        if last:
            xs = _ffn((x,), mod, group_of_tile, gpre[2], gpost[2], w1, w2, l, 1, 6, tm, tf, n_prompt, True)
        else:
            xs = (_ffn_overlapped(x, mod, group_of_tile, gpre[2], gpost[2], w1, w2, l, 1, 6, tm, tf),)
        new_s.append(st_p)

    y_prompt = xs[0].reshape(batch, seq, d)
    y_sample = xs[1].reshape(dec_batch, dec_seq, d)
    new_k, new_v = (a.reshape(batch, depth, seq, heads, -1) for a in caches)
    return (y_prompt, y_sample, new_k, new_v, jnp.stack(new_s, axis=1))
```
